```python
import math
import jax, jax.numpy as jnp
from jax import lax
import numpy as np

D_MODEL = 1024
BATCH = 1
SEQ = 16384
DEPTH = 4

FOX_HEADS = 4
FOX_HEAD_DIM = 64
FOX_WIDTH = FOX_HEADS * FOX_HEAD_DIM
Q_BLOCK = 128

GLA_HEADS = 4
GLA_DK = 32
GLA_DV = 64
GLA_KW = GLA_HEADS * GLA_DK
GLA_VW = GLA_HEADS * GLA_DV
GLA_RANK = 16
GLA_TAU = 16.0
GLA_CHUNK = 64

MOBA_HEADS = 4
MOBA_HEAD_DIM = 64
MOBA_WIDTH = MOBA_HEADS * MOBA_HEAD_DIM
MOBA_BLOCK = 256
MOBA_TOPK = 3

N_BRANCH = 3
BRANCH_WIDTH = 256

N_EXPERTS = 32
TOP_K = 4
D_EXPERT = D_MODEL
SWIGLU_LIMIT = 7.0
SWIGLU_ALPHA = 1.702
MOE_BLOCK = 128

ALPHA = (2 * DEPTH) ** 0.25
BETA = (8 * DEPTH) ** -0.25
LN_EPS = 1e-5
RMS_EPS = 1e-6

SPLIT_SIZES = (
    FOX_WIDTH, FOX_WIDTH, FOX_WIDTH, FOX_HEADS,
    GLA_KW, GLA_KW, GLA_VW, GLA_RANK, GLA_VW,
    MOBA_WIDTH, MOBA_WIDTH, MOBA_WIDTH,
    N_BRANCH * D_MODEL,
)
SPLIT_POINTS = tuple(int(p) for p in np.cumsum(SPLIT_SIZES)[:-1])
D_IN = sum(SPLIT_SIZES)

kernel_name = 'hybrid_fox_gla_moba_moe_deepnorm'

F32 = jnp.float32


def layer_norm(x, g, b):
    xf = x.astype(F32)
    mu = jnp.mean(xf, axis=-1, keepdims=True)
    var = jnp.mean(jnp.square(xf - mu), axis=-1, keepdims=True)
    return ((xf - mu) * lax.rsqrt(var + LN_EPS)).astype(x.dtype) * g + b


def head_rms_norm(o, g):
    B, S = o.shape[0], o.shape[1]
    of = o.astype(F32)
    of = of * lax.rsqrt(jnp.mean(of * of, axis=-1, keepdims=True) + RMS_EPS)
    return of.astype(o.dtype).reshape(B, S, -1) * g


def forgetting_attention(q, k, v, f_logit):
    B, S, H, Dh = q.shape
    c = jnp.cumsum(jax.nn.log_sigmoid(f_logit.astype(F32)), axis=1)
    cT = c.transpose(0, 2, 1)
    n_blk = S // Q_BLOCK
    qb = q.reshape(B, n_blk, Q_BLOCK, H, Dh).transpose(1, 0, 2, 3, 4)
    cb = c.reshape(B, n_blk, Q_BLOCK, H).transpose(1, 0, 2, 3)
    pos_k = jnp.arange(S)
    scale = Dh ** -0.5

    def block(args):
        i, q_i, c_i = args
        s = jnp.einsum('bqhd,bkhd->bhqk', q_i, k).astype(F32) * scale
        s = s + c_i.transpose(0, 2, 1)[..., None] - cT[:, :, None, :]
        pos_q = i * Q_BLOCK + jnp.arange(Q_BLOCK)
        s = jnp.where(pos_k[None, :] <= pos_q[:, None], s, -jnp.inf)
        p = jax.nn.softmax(s, axis=-1).astype(v.dtype)
        return jnp.einsum('bhqk,bkhd->bqhd', p, v)

    o = lax.map(block, (jnp.arange(n_blk), qb, cb))
    return o.transpose(1, 0, 2, 3, 4).reshape(B, S, H * Dh)


def gated_linear_attention(q, k, v, log_a):
    B, S, H, dk = q.shape
    dv = v.shape[-1]
    C = GLA_CHUNK
    n = S // C

    def to_chunks(t):
        return t.reshape(B, n, C, H, t.shape[-1]).transpose(1, 0, 3, 2, 4)

    qc, kc, vc, ac = to_chunks(q), to_chunks(k), to_chunks(v), to_chunks(log_a)
    causal = jnp.tril(jnp.ones((C, C), dtype=bool))
    scale = dk ** -0.5

    def step(state, inp):
        q_i, k_i, v_i, a_i = inp
        b = jnp.cumsum(a_i, axis=2)
        qf = q_i.astype(F32) * scale
        kf = k_i.astype(F32)
        vf = v_i.astype(F32)
        o_inter = jnp.einsum('bhck,bhkv->bhcv', qf * jnp.exp(b), state)
        diff = jnp.where(causal[..., None], b[:, :, :, None, :] - b[:, :, None, :, :], -jnp.inf)
        A = jnp.einsum('bhtk,bhsk,bhtsk->bhts', qf, kf, jnp.exp(diff))
        o_intra = jnp.einsum('bhts,bhsv->bhtv', A, vf)
        b_last = b[:, :, -1:, :]
        new_state = state * jnp.exp(b_last[:, :, 0, :])[..., None] + jnp.einsum(
            'bhck,bhcv->bhkv', kf * jnp.exp(b_last - b), vf)
        return new_state, o_inter + o_intra

    state0 = jnp.zeros((B, H, dk, dv), F32)
    _, o = lax.scan(step, state0, (qc, kc, vc, ac))
    return o.transpose(1, 0, 3, 2, 4).reshape(B, S, H, dv).astype(v.dtype)


def block_sparse_attention(q, k, v):
    B, S, H, Dh = q.shape
    L = MOBA_BLOCK
    Sp = -(-S // L) * L
    pad = ((0, 0), (0, Sp - S), (0, 0), (0, 0))
    nb = Sp // L
    n_gate = max(nb, MOBA_TOPK)
    kb = jnp.pad(k, pad).reshape(B, nb, L, H, Dh).transpose(0, 3, 1, 2, 4)
    vb = jnp.pad(v, pad).reshape(B, nb, L, H, Dh).transpose(0, 3, 1, 2, 4)
    k_mean = jnp.mean(kb.astype(F32), axis=3)
    k_mean = jnp.pad(k_mean, ((0, 0), (0, 0), (0, n_gate - nb), (0, 0)))
    n_q = S // Q_BLOCK
    qb = q.reshape(B, n_q, Q_BLOCK, H, Dh).transpose(1, 0, 2, 3, 4)
    scale = Dh ** -0.5
    bi = jnp.arange(B)[:, None, None, None]
    hi = jnp.arange(H)[None, None, :, None]
    slot = jnp.arange(MOBA_TOPK)
    blk_ids = jnp.arange(n_gate)

    def block(args):
        i, q_i = args
        pos_q = i * Q_BLOCK + jnp.arange(Q_BLOCK)
        own = (i * Q_BLOCK) // L
        gate = jnp.einsum('bqhd,bhnd->bqhn', q_i.astype(F32), k_mean)
        gate = jnp.where(blk_ids < own, gate, -jnp.inf)
        _, idx = lax.top_k(gate, MOBA_TOPK)
        idx = jnp.minimum(idx, nb - 1)
        valid = slot < own
        k_sel = kb[bi, hi, idx]
        v_sel = vb[bi, hi, idx]
        s_sel = jnp.einsum('bqhd,bqhnld->bqhnl', q_i, k_sel).astype(F32) * scale
        s_sel = jnp.where(valid[:, None], s_sel, -jnp.inf).reshape(B, Q_BLOCK, H, MOBA_TOPK * L)
        k_own = lax.dynamic_slice_in_dim(kb, own, 1, axis=2)[:, :, 0]
        v_own = lax.dynamic_slice_in_dim(vb, own, 1, axis=2)[:, :, 0]
        s_own = jnp.einsum('bqhd,bhld->bqhl', q_i, k_own).astype(F32) * scale
        pos_k = own * L + jnp.arange(L)
        s_own = jnp.where((pos_k[None, :] <= pos_q[:, None])[:, None, :], s_own, -jnp.inf)
        p = jax.nn.softmax(jnp.concatenate([s_sel, s_own], axis=-1), axis=-1).astype(v.dtype)
        p_sel = p[..., :MOBA_TOPK * L].reshape(B, Q_BLOCK, H, MOBA_TOPK, L)
        p_own = p[..., MOBA_TOPK * L:]
        return (jnp.einsum('bqhnl,bqhnld->bqhd', p_sel, v_sel)
                + jnp.einsum('bqhl,bhld->bqhd', p_own, v_own))

    o = lax.map(block, (jnp.arange(n_q), qb))
    return o.transpose(1, 0, 2, 3, 4).reshape(B, S, H * Dh)


def mixer(x, w_in, fox_fb, gla_a_up, gla_a_b, gla_norm_g, w_branch, w_out):
    B, S, _ = x.shape
    (fq, fk, fv, ff, gq, gk, gv, ga, gr, mq, mk, mv, gates) = jnp.split(
        x @ w_in, SPLIT_POINTS, axis=-1)

    def heads(t, h):
        return t.reshape(B, S, h, -1)

    y_fox = forgetting_attention(heads(fq, FOX_HEADS), heads(fk, FOX_HEADS), heads(fv, FOX_HEADS),
                                 ff + fox_fb)
    log_a = jax.nn.log_sigmoid((ga @ gla_a_up + gla_a_b).astype(F32)) / GLA_TAU
    o_gla = gated_linear_attention(heads(gq, GLA_HEADS), heads(gk, GLA_HEADS), heads(gv, GLA_HEADS),
                                   heads(log_a, GLA_HEADS))
    y_gla = head_rms_norm(o_gla, gla_norm_g) * jax.nn.silu(gr)
    y_moba = block_sparse_attention(heads(mq, MOBA_HEADS), heads(mk, MOBA_HEADS), heads(mv, MOBA_HEADS))
    ys = jnp.stack([y_fox, y_gla, y_moba], axis=2)
    up = jnp.einsum('bsnc,ncd->bsnd', ys, w_branch)
    g = jax.nn.sigmoid(gates.reshape(B, S, N_BRANCH, D_MODEL))
    merged = jnp.sum(g * up, axis=2)
    return merged @ w_out


def moe(x, w_router, b_router, w_gu, b_gu, w_dn, b_dn):
    B, S, D = x.shape
    T = B * S
    x2 = x.reshape(T, D)
    logits = (x2 @ w_router + b_router).astype(F32)
    top_v, top_i = lax.top_k(logits, TOP_K)
    gates = jax.nn.softmax(top_v, axis=-1).astype(x.dtype)
    n_assign = T * TOP_K
    flat_e = top_i.reshape(-1)
    flat_tok = jnp.repeat(jnp.arange(T, dtype=jnp.int32), TOP_K)
    order = jnp.argsort(flat_e)
    sorted_e = flat_e[order]
    sorted_tok = flat_tok[order]
    sorted_gate = gates.reshape(-1)[order]
    counts = jnp.bincount(flat_e, length=N_EXPERTS)
    padded = (counts + MOE_BLOCK - 1) // MOE_BLOCK * MOE_BLOCK
    pad_end = jnp.cumsum(padded)
    pad_start = pad_end - padded
    start = jnp.cumsum(counts) - counts
    dest = pad_start[sorted_e] + (jnp.arange(n_assign) - start[sorted_e])
    n_blocks = (n_assign + N_EXPERTS * (MOE_BLOCK - 1) + MOE_BLOCK - 1) // MOE_BLOCK
    buf_tok = jnp.zeros((n_blocks * MOE_BLOCK,), jnp.int32).at[dest].set(sorted_tok)
    block_expert = jnp.clip(jnp.searchsorted(pad_end, jnp.arange(n_blocks) * MOE_BLOCK, side='right'),
                            0, N_EXPERTS - 1)
    xin = x2[buf_tok].reshape(n_blocks, MOE_BLOCK, D)

    def expert_block(args):
        xb, e = args
        h = xb @ w_gu[e] + b_gu[e]
        gate = jnp.minimum(h[:, 0::2], SWIGLU_LIMIT)
        upv = jnp.clip(h[:, 1::2], -SWIGLU_LIMIT, SWIGLU_LIMIT)
        act = (upv + 1.0) * (gate * jax.nn.sigmoid(SWIGLU_ALPHA * gate))
        return act @ w_dn[e] + b_dn[e]

    yb = lax.map(expert_block, (xin, block_expert)).reshape(n_blocks * MOE_BLOCK, D)
    y_assign = yb[dest] * sorted_gate[:, None]
    out = jnp.zeros((T, D), x.dtype).at[sorted_tok].add(y_assign)
    return out.reshape(B, S, D)


def setup_inputs(seed: int = 0) -> dict:
    key = jax.random.key(seed)
    ks = jax.random.split(key, 20)
    nrm = jax.random.normal
    x = nrm(ks[0], (BATCH, SEQ, D_MODEL), F32)
    w_in = nrm(ks[1], (DEPTH, D_MODEL, D_IN), F32) * D_MODEL ** -0.5
    fox_fb = (jnp.linspace(1.0, 5.0, FOX_HEADS, dtype=F32)[None, :]
              + 0.01 * nrm(ks[2], (DEPTH, FOX_HEADS), F32))
    gla_a_up = nrm(ks[3], (DEPTH, GLA_RANK, GLA_KW), F32) * GLA_RANK ** -0.5
    gla_a_b = 0.01 * nrm(ks[4], (DEPTH, GLA_KW), F32)
    gla_norm_g = 1.0 + 0.01 * nrm(ks[5], (DEPTH, GLA_VW), F32)
    w_branch = nrm(ks[6], (DEPTH, N_BRANCH, BRANCH_WIDTH, D_MODEL), F32) * BRANCH_WIDTH ** -0.5
    w_out = nrm(ks[7], (DEPTH, D_MODEL, D_MODEL), F32) * (D_MODEL ** -0.5 * BETA)
    ln1_g = 1.0 + 0.01 * nrm(ks[8], (DEPTH, D_MODEL), F32)
    ln1_b = 0.01 * nrm(ks[9], (DEPTH, D_MODEL), F32)
    w_router = nrm(ks[10], (DEPTH, D_MODEL, N_EXPERTS), F32) * D_MODEL ** -0.5
    b_router = 0.01 * nrm(ks[11], (DEPTH, N_EXPERTS), F32)
    w_gu = nrm(ks[12], (DEPTH, N_EXPERTS, D_MODEL, 2 * D_EXPERT), F32) * D_MODEL ** -0.5
    b_gu = 0.01 * nrm(ks[13], (DEPTH, N_EXPERTS, 2 * D_EXPERT), F32)
    w_dn = nrm(ks[14], (DEPTH, N_EXPERTS, D_EXPERT, D_MODEL), F32) * (D_EXPERT ** -0.5 * BETA)
    b_dn = 0.01 * nrm(ks[15], (DEPTH, N_EXPERTS, D_MODEL), F32)
    ln2_g = 1.0 + 0.01 * nrm(ks[16], (DEPTH, D_MODEL), F32)
    ln2_b = 0.01 * nrm(ks[17], (DEPTH, D_MODEL), F32)
    return {'x': x, 'w_in': w_in, 'fox_fb': fox_fb, 'gla_a_up': gla_a_up, 'gla_a_b': gla_a_b,
            'gla_norm_g': gla_norm_g, 'w_branch': w_branch, 'w_out': w_out,
            'ln1_g': ln1_g, 'ln1_b': ln1_b, 'w_router': w_router, 'b_router': b_router,
            'w_gu': w_gu, 'b_gu': b_gu, 'w_dn': w_dn, 'b_dn': b_dn,
            'ln2_g': ln2_g, 'ln2_b': ln2_b}


def reference(x, w_in, fox_fb, gla_a_up, gla_a_b, gla_norm_g, w_branch, w_out,
              ln1_g, ln1_b, w_router, b_router, w_gu, b_gu, w_dn, b_dn, ln2_g, ln2_b):
    for l in range(DEPTH):
        m = mixer(x, w_in[l], fox_fb[l], gla_a_up[l], gla_a_b[l], gla_norm_g[l], w_branch[l], w_out[l])
        x = layer_norm(ALPHA * x + m, ln1_g[l], ln1_b[l])
        f = moe(x, w_router[l], b_router[l], w_gu[l], b_gu[l], w_dn[l], b_dn[l])
        x = layer_norm(ALPHA * x + f, ln2_g[l], ln2_b[l])
    return x
```

```python
import functools

import jax
import jax.numpy as jnp
from jax import lax
from jax.experimental import pallas as pl
from jax.experimental.pallas import tpu as pltpu

F32 = jnp.float32
BF16 = jnp.bfloat16

D_MODEL = 1024
DEPTH = 4
N_HEADS = 4
HEAD_DIM = 64
GLA_DK = 32
GLA_KW = N_HEADS * GLA_DK
GLA_RANK = 16
GLA_TAU = 16.0
MOBA_BLOCK = 256
MOBA_TOPK = 3
BRANCH_WIDTH = 256
N_EXPERTS = 32
TOP_K = 4
SWIGLU_LIMIT = 7.0
SWIGLU_ALPHA = 1.702
ALPHA = (2 * DEPTH) ** 0.25
LN_EPS = 1e-5
RMS_EPS = 1e-6

LANES = 128
ROW_TILE = 8
NEG = -1e30
UNSELECTED = -32768.0
VMEM_LIMIT = 48 * 1024 * 1024

P_GATES = 0
P_FOX = 3 * D_MODEL
P_MOBA = P_FOX + 3 * BRANCH_WIDTH
P_GLA = P_MOBA + 3 * BRANCH_WIDTH
P_WIDTH = P_GLA + 2 * GLA_KW + 2 * BRANCH_WIDTH
TAIL_FF = 0
TAIL_GA = 4

EXPERT_BLOCK = 256


def _params(sem):
    return pltpu.CompilerParams(dimension_semantics=sem, vmem_limit_bytes=VMEM_LIMIT)


def _split3(x):
    hi = x.astype(BF16)
    r1 = x - hi.astype(F32)
    mid = r1.astype(BF16)
    lo = (r1 - mid.astype(F32)).astype(BF16)
    return hi, mid, lo


def _split2(x):
    hi = x.astype(BF16)
    lo = (x - hi.astype(F32)).astype(BF16)
    return hi, lo


def _dot(a, b):
    return jnp.dot(a, b, preferred_element_type=F32)


def _dot_nt(a, b):
    return lax.dot_general(a, b, (((1,), (1,)), ((), ())), preferred_element_type=F32)


def _dot_tn(a, b):
    return lax.dot_general(a, b, (((0,), (0,)), ((), ())), preferred_element_type=F32)


def _log_sigmoid(t):
    return jnp.minimum(t, 0.0) - jnp.log1p(jnp.exp(-jnp.abs(t)))


def _sigmoid(t):
    return 1.0 / (1.0 + jnp.exp(-t))


def _iota(shape, axis):
    return lax.broadcasted_iota(jnp.int32, shape, axis)


def _inproj_kernel(x_ref, w_ref, wt_ref, p_ref, t_ref, xb_ref):
    @pl.when(pl.program_id(1) == 0)
    def _():
        xb_ref[...] = x_ref[...].astype(BF16)
        t_ref[...] = _dot(xb_ref[...], wt_ref[...])

    p_ref[...] = _dot(xb_ref[...], w_ref[...]).astype(BF16)


def _inproj(x, w, wt, tm=1024, tn=768):
    s = x.shape[0]
    return pl.pallas_call(
        _inproj_kernel,
        grid=(s // tm, P_WIDTH // tn),
        in_specs=[pl.BlockSpec((tm, D_MODEL), lambda i, j: (i, 0)),
                  pl.BlockSpec((D_MODEL, tn), lambda i, j: (0, j)),
                  pl.BlockSpec((D_MODEL, LANES), lambda i, j: (0, 0))],
        out_specs=[pl.BlockSpec((tm, tn), lambda i, j: (i, j)),
                   pl.BlockSpec((tm, LANES), lambda i, j: (i, 0))],
        out_shape=[jax.ShapeDtypeStruct((s, P_WIDTH), BF16),
                   jax.ShapeDtypeStruct((s, LANES), F32)],
        scratch_shapes=[pltpu.VMEM((tm, D_MODEL), BF16)],
        compiler_params=_params(("parallel", "arbitrary")),
    )(x, w, wt)


def _fox_prep_kernel(q_ref, k_ref, v_ref, t_ref, fb_ref, qa_ref, ka_ref, va_ref, carry_ref):
    t = q_ref.shape[0]

    @pl.when(pl.program_id(0) == 0)
    def _():
        carry_ref[...] = jnp.zeros_like(carry_ref)

    ls = _log_sigmoid(t_ref[...] + fb_ref[...])
    tri = (_iota((t, t), 0) >= _iota((t, t), 1)).astype(BF16)
    hi, mid, lo = _split3(ls)
    c = _dot(tri, hi) + _dot(tri, mid) + _dot(tri, lo) + carry_ref[...]
    carry_ref[...] = c[t - 1:t, :]

    lane = _iota((t, HEAD_DIM), 1)
    q_aug = jnp.where(lane < 3, 1.0, 0.0).astype(BF16)
    v_aug = jnp.where(lane == 0, 1.0, 0.0).astype(BF16)
    q = q_ref[...]
    k = k_ref[...]
    v = v_ref[...]
    for h in range(N_HEADS):
        sl = slice(h * HEAD_DIM, (h + 1) * HEAD_DIM)
        nhi, nmid, nlo = (piece.astype(F32) for piece in _split3(-c[:, h:h + 1]))
        k_aug = jnp.where(lane == 0, nhi, jnp.where(lane == 1, nmid, jnp.where(lane == 2, nlo, 0.0)))
        qa_ref[h] = jnp.concatenate([q[:, sl] * 0.125, q_aug], axis=1)
        ka_ref[h] = jnp.concatenate([k[:, sl], k_aug.astype(BF16)], axis=1)
        va_ref[h] = jnp.concatenate([v[:, sl], v_aug], axis=1)


def _fox_prep(p, tail, fb_row, t=512):
    s = p.shape[0]
    cb = P_FOX // BRANCH_WIDTH
    aug = jax.ShapeDtypeStruct((N_HEADS, s, LANES), BF16)
    aug_spec = pl.BlockSpec((N_HEADS, t, LANES), lambda i: (0, i, 0))
    return pl.pallas_call(
        _fox_prep_kernel,
        grid=(s // t,),
        in_specs=[pl.BlockSpec((t, BRANCH_WIDTH), lambda i: (i, cb)),
                  pl.BlockSpec((t, BRANCH_WIDTH), lambda i: (i, cb + 1)),
                  pl.BlockSpec((t, BRANCH_WIDTH), lambda i: (i, cb + 2)),
                  pl.BlockSpec((t, LANES), lambda i: (i, 0)),
                  pl.BlockSpec((1, LANES), lambda i: (0, 0))],
        out_specs=[aug_spec, aug_spec, aug_spec],
        out_shape=[aug, aug, aug],
        scratch_shapes=[pltpu.VMEM((1, LANES), F32)],
        compiler_params=_params(("arbitrary",)),
    )(p, p, p, tail, fb_row)


def _moba_prep_kernel(q_ref, k_ref, v_ref, qa_ref, ka_ref, va_ref, kmean_ref):
    i = pl.program_id(0)
    t = q_ref.shape[0]
    nbl = kmean_ref.shape[0]

    @pl.when(i == 0)
    def _():
        kmean_ref[...] = jnp.zeros_like(kmean_ref)

    q = q_ref[...]
    k = k_ref[...]
    v = v_ref[...]
    km = kmean_ref[...]
    lane = _iota((t, HEAD_DIM), 1)
    v_aug = jnp.where(lane == 0, 1.0, 0.0).astype(BF16)
    k_aug = jnp.where(lane == i, 1.0, 0.0).astype(BF16)
    for h in range(N_HEADS):
        sl = slice(h * HEAD_DIM, (h + 1) * HEAD_DIM)
        kh_hi, kh_lo = _split2(km[:HEAD_DIM, sl])
        gate = _dot_nt(q[:, sl], kh_hi) + _dot_nt(q[:, sl], kh_lo)
        g = jnp.where(lane < i, gate, NEG)
        bias = jnp.where(lane == i, 0.0, UNSELECTED)
        for _ in range(MOBA_TOPK):
            mx = jnp.max(g, axis=1, keepdims=True)
            idx = jnp.min(jnp.where(g == mx, lane, HEAD_DIM), axis=1, keepdims=True)
            pick = jnp.logical_and(lane == idx, mx > 0.5 * NEG)
            bias = jnp.where(pick, 0.0, bias)
            g = jnp.where(lane == idx, NEG, g)
        qa_ref[h] = jnp.concatenate([q[:, sl] * 0.125, bias.astype(BF16)], axis=1)
        ka_ref[h] = jnp.concatenate([k[:, sl], k_aug], axis=1)
        va_ref[h] = jnp.concatenate([v[:, sl], v_aug], axis=1)
    kmean = jnp.mean(k.astype(F32), axis=0, keepdims=True)
    kmean_ref[...] = jnp.where(_iota((nbl, BRANCH_WIDTH), 0) == i, kmean, km)


def _moba_prep(p):
    s = p.shape[0]
    t = MOBA_BLOCK
    assert s % t == 0 and s // t <= HEAD_DIM, "block one-hot must fit the 64 augmentation lanes"
    cb = P_MOBA // BRANCH_WIDTH
    aug = jax.ShapeDtypeStruct((N_HEADS, s, LANES), BF16)
    aug_spec = pl.BlockSpec((N_HEADS, t, LANES), lambda i: (0, i, 0))
    return pl.pallas_call(
        _moba_prep_kernel,
        grid=(s // t,),
        in_specs=[pl.BlockSpec((t, BRANCH_WIDTH), lambda i: (i, cb)),
                  pl.BlockSpec((t, BRANCH_WIDTH), lambda i: (i, cb + 1)),
                  pl.BlockSpec((t, BRANCH_WIDTH), lambda i: (i, cb + 2))],
        out_specs=[aug_spec, aug_spec, aug_spec],
        out_shape=[aug, aug, aug],
        scratch_shapes=[pltpu.VMEM((HEAD_DIM, BRANCH_WIDTH), F32)],
        compiler_params=_params(("arbitrary",)),
    )(p, p, p)


def _flash_kernel(q_ref, k_ref, v_ref, o_ref, *, tq, tk):
    i = pl.program_id(1)
    q = q_ref[0]

    def tile(j, carry, masked):
        m, acc = carry
        k = k_ref[0, pl.ds(pl.multiple_of(j * tk, tk), tk), :]
        v = v_ref[0, pl.ds(pl.multiple_of(j * tk, tk), tk), :]
        s = _dot_nt(q, k)
        if masked:
            row = _iota((tq, tk), 0) + i * tq
            col = _iota((tq, tk), 1) + j * tk
            s = jnp.where(col <= row, s, NEG)
        m_new = jnp.maximum(m, jnp.max(s, axis=1, keepdims=True))
        p = jnp.exp(s - m_new)
        acc = jnp.exp(m - m_new) * acc + _dot(p.astype(BF16), v)
        return m_new, acc

    carry = (jnp.full((tq, 1), NEG, F32), jnp.zeros((tq, LANES), F32))
    n_diag = tq // tk
    carry = lax.fori_loop(0, i * n_diag, lambda j, c: tile(j, c, False), carry)
    for d in range(n_diag):
        carry = tile(i * n_diag + d, carry, True)
    _, acc = carry
    o_ref[0] = (acc / acc[:, HEAD_DIM:HEAD_DIM + 1]).astype(o_ref.dtype)


def _flash(qa, ka, va, tq=256, tk=256):
    nh, s, _ = qa.shape
    return pl.pallas_call(
        functools.partial(_flash_kernel, tq=tq, tk=tk),
        grid=(nh, s // tq),
        in_specs=[pl.BlockSpec((1, tq, LANES), lambda h, i: (h, i, 0)),
                  pl.BlockSpec((1, s, LANES), lambda h, i: (h, 0, 0)),
                  pl.BlockSpec((1, s, LANES), lambda h, i: (h, 0, 0))],
        out_specs=pl.BlockSpec((1, tq, LANES), lambda h, i: (h, i, 0)),
        out_shape=jax.ShapeDtypeStruct((nh, s, LANES), BF16),
        compiler_params=_params(("parallel", "parallel")),
    )(qa, ka, va)


GLA_SUB = 16


def _gla_kernel(q_ref, k_ref, v_ref, r_ref, t_ref, aup_ref, ab_ref, g_ref, y_ref,
                st_ref, b_ref, o_ref):
    t = q_ref.shape[0]
    c = GLA_SUB

    @pl.when(pl.program_id(0) == 0)
    def _():
        st_ref[...] = jnp.zeros_like(st_ref)

    t_hi, t_lo = _split2(t_ref[...])
    a_hi, a_lo = _split2(aup_ref[...])
    z = _dot(t_hi, a_hi) + _dot(t_lo, a_hi) + _dot(t_hi, a_lo) + ab_ref[...]
    log_a = _log_sigmoid(z) * (1.0 / GLA_TAU)
    row = _iota((t, t), 0)
    col = _iota((t, t), 1)
    tri = jnp.logical_and(row // c == col // c, row >= col).astype(BF16)
    hi, mid, lo = _split3(log_a)
    b_ref[...] = _dot(tri, hi) + _dot(tri, mid) + _dot(tri, lo)

    expand = (_iota((GLA_KW, BRANCH_WIDTH), 0) // GLA_DK
              == _iota((GLA_KW, BRANCH_WIDTH), 1) // HEAD_DIM).astype(BF16)
    st_mask = (_iota((BRANCH_WIDTH, GLA_KW), 0) // HEAD_DIM
               == _iota((BRANCH_WIDTH, GLA_KW), 1) // GLA_DK).astype(F32)
    srow = _iota((c, GLA_KW), 0)
    orow = _iota((c, BRANCH_WIDTH), 0)
    scale = GLA_DK ** -0.5

    def step(r, _):
        r0 = pl.multiple_of(r * c, c)
        qs = q_ref[pl.ds(r0, c), :].astype(F32) * scale
        ks = k_ref[pl.ds(r0, c), :].astype(F32)
        vb = v_ref[pl.ds(r0, c), :]
        vf = vb.astype(F32)
        bs = b_ref[pl.ds(r0, c), :]
        b_last = bs[c - 1:c, :]
        st = st_ref[...]

        o_inter = _dot_nt((qs * jnp.exp(bs)).astype(BF16), st.astype(BF16))

        pieces = []
        for tt in range(c):
            e = jnp.exp(jnp.where(srow <= tt, bs[tt:tt + 1, :] - bs, NEG))
            pieces.append(e * qs[tt:tt + 1, :] * ks)
        pm = jnp.concatenate(pieces, axis=0).astype(BF16)
        a = _dot(pm, expand)
        o_intra = jnp.zeros((c, BRANCH_WIDTH), F32)
        for tt in range(c):
            o_t = jnp.sum(a[tt * c:(tt + 1) * c, :] * vf, axis=0, keepdims=True)
            o_intra = jnp.where(orow == tt, o_t, o_intra)
        o_ref[pl.ds(r0, c), :] = o_inter + o_intra

        ke = (ks * jnp.exp(b_last - bs)).astype(BF16)
        st_ref[...] = st * jnp.exp(b_last) + _dot_tn(vb, ke) * st_mask
        return 0

    lax.fori_loop(0, t // c, step, 0)

    o = o_ref[...]
    ones_bd = (_iota((BRANCH_WIDTH, BRANCH_WIDTH), 0) // HEAD_DIM
               == _iota((BRANCH_WIDTH, BRANCH_WIDTH), 1) // HEAD_DIM).astype(BF16)
    sq_hi, sq_mid, sq_lo = _split3(o * o)
    ms = (_dot(sq_hi, ones_bd) + _dot(sq_mid, ones_bd) + _dot(sq_lo, ones_bd)) * (1.0 / HEAD_DIM)
    gr = r_ref[...].astype(F32)
    y = o * lax.rsqrt(ms + RMS_EPS) * g_ref[...] * (gr * _sigmoid(gr))
    y_ref[...] = y.astype(y_ref.dtype)


def _gla(p, tail, aup_pad, ab_row, g_row, t=512):
    s = p.shape[0]
    cq = P_GLA // GLA_KW
    cv = (P_GLA + 2 * GLA_KW) // BRANCH_WIDTH
    return pl.pallas_call(
        _gla_kernel,
        grid=(s // t,),
        in_specs=[pl.BlockSpec((t, GLA_KW), lambda i: (i, cq)),
                  pl.BlockSpec((t, GLA_KW), lambda i: (i, cq + 1)),
                  pl.BlockSpec((t, BRANCH_WIDTH), lambda i: (i, cv)),
                  pl.BlockSpec((t, BRANCH_WIDTH), lambda i: (i, cv + 1)),
                  pl.BlockSpec((t, LANES), lambda i: (i, 0)),
                  pl.BlockSpec((LANES, GLA_KW), lambda i: (0, 0)),
                  pl.BlockSpec((1, GLA_KW), lambda i: (0, 0)),
                  pl.BlockSpec((1, BRANCH_WIDTH), lambda i: (0, 0))],
        out_specs=pl.BlockSpec((t, BRANCH_WIDTH), lambda i: (i, 0)),
        out_shape=jax.ShapeDtypeStruct((s, BRANCH_WIDTH), BF16),
        scratch_shapes=[pltpu.VMEM((BRANCH_WIDTH, GLA_KW), F32),
                        pltpu.VMEM((t, GLA_KW), F32),
                        pltpu.VMEM((t, BRANCH_WIDTH), F32)],
        compiler_params=_params(("arbitrary",)),
    )(p, p, p, p, tail, aup_pad, ab_row, g_row)


def _layer_norm(z, g, b):
    mu = jnp.mean(z, axis=1, keepdims=True)
    zc = z - mu
    var = jnp.mean(zc * zc, axis=1, keepdims=True)
    return zc * lax.rsqrt(var + LN_EPS) * g + b


def _store_row_tiles(ref, val):
    n = val.shape[0]
    for a in range(ROW_TILE):
        ref[pl.ds(a, n, stride=ROW_TILE), :] = val[:, a * LANES:(a + 1) * LANES]


def _load_row_tiles(ref, n, base=0):
    return jnp.concatenate(
        [ref[pl.ds(base + a, n, stride=ROW_TILE), :] for a in range(ROW_TILE)], axis=1)


def _merge_kernel(fox_ref, gla_ref, moba_ref, g0_ref, g1_ref, g2_ref, x_ref, wb_ref, wo_ref,
                  lg_ref, lb_ref, wrh_ref, wrl_ref, br_ref,
                  x1_ref, sel_ref, idx_ref, gw_ref):
    tm = x_ref.shape[0]

    def heads(ref):
        return jnp.concatenate([ref[h][:, :HEAD_DIM] for h in range(N_HEADS)], axis=1)

    merged = _sigmoid(g0_ref[...].astype(F32)) * _dot(heads(fox_ref), wb_ref[0])
    merged += _sigmoid(g1_ref[...].astype(F32)) * _dot(gla_ref[...], wb_ref[1])
    merged += _sigmoid(g2_ref[...].astype(F32)) * _dot(heads(moba_ref), wb_ref[2])
    z = ALPHA * x_ref[...] + _dot(merged.astype(BF16), wo_ref[...])
    x1 = _layer_norm(z, lg_ref[...], lb_ref[...])
    _store_row_tiles(x1_ref, x1)

    x_hi, x_lo = _split2(x1)
    logits = (_dot(x_hi, wrh_ref[...]) + _dot(x_lo, wrh_ref[...]) + _dot(x_hi, wrl_ref[...])
              + br_ref[...])
    lane = _iota((tm, LANES), 1)
    lg = jnp.where(lane < N_EXPERTS, logits, NEG)
    sel = jnp.zeros((tm, LANES), F32)
    idxf = jnp.zeros((tm, LANES), F32)
    ew = jnp.zeros((tm, LANES), F32)
    top = None
    for r in range(TOP_K):
        mx = jnp.max(lg, axis=1, keepdims=True)
        idx = jnp.min(jnp.where(lg == mx, lane, LANES), axis=1, keepdims=True)
        hit = lane == idx
        top = mx if top is None else top
        sel = jnp.where(hit, 1.0, sel)
        idxf = jnp.where(lane == r, idx.astype(F32), idxf)
        ew = jnp.where(lane == r, jnp.exp(mx - top), ew)
        lg = jnp.where(hit, NEG, lg)
    sel_ref[...] = sel
    idx_ref[...] = idxf
    gw_ref[...] = ew / jnp.sum(ew, axis=1, keepdims=True)


def _merge(fox_o, y_gla, moba_o, p, x, wb, wo, lg, lb, wrh, wrl, br, tm=256):
    s = x.shape[0]
    head_spec = pl.BlockSpec((N_HEADS, tm, LANES), lambda i: (0, i, 0))
    row = lambda n: pl.BlockSpec((1, n), lambda i: (0, 0))
    small = jax.ShapeDtypeStruct((s, LANES), F32)
    small_spec = pl.BlockSpec((tm, LANES), lambda i: (i, 0))
    return pl.pallas_call(
        _merge_kernel,
        grid=(s // tm,),
        in_specs=[head_spec,
                  pl.BlockSpec((tm, BRANCH_WIDTH), lambda i: (i, 0)),
                  head_spec,
                  pl.BlockSpec((tm, D_MODEL), lambda i: (i, 0)),
                  pl.BlockSpec((tm, D_MODEL), lambda i: (i, 1)),
                  pl.BlockSpec((tm, D_MODEL), lambda i: (i, 2)),
                  pl.BlockSpec((tm, D_MODEL), lambda i: (i, 0)),
                  pl.BlockSpec((3, BRANCH_WIDTH, D_MODEL), lambda i: (0, 0, 0)),
                  pl.BlockSpec((D_MODEL, D_MODEL), lambda i: (0, 0)),
                  row(D_MODEL), row(D_MODEL),
                  pl.BlockSpec((D_MODEL, LANES), lambda i: (0, 0)),
                  pl.BlockSpec((D_MODEL, LANES), lambda i: (0, 0)),
                  row(LANES)],
        out_specs=[pl.BlockSpec((tm * ROW_TILE, LANES), lambda i: (i, 0)),
                   small_spec, small_spec, small_spec],
        out_shape=[jax.ShapeDtypeStruct((s * ROW_TILE, LANES), F32), small, small, small],
        compiler_params=_params(("parallel",)),
    )(fox_o, y_gla, moba_o, p, p, p, x, wb, wo, lg, lb, wrh, wrl, br)


def _rank_kernel(sel_ref, idx_ref, rank_ref, cnt_ref, carry_ref):
    t = sel_ref.shape[0]

    @pl.when(pl.program_id(0) == 0)
    def _():
        carry_ref[...] = jnp.zeros_like(carry_ref)

    sel = sel_ref[...].astype(BF16)
    stri = (_iota((t, t), 0) > _iota((t, t), 1)).astype(BF16)
    excl = _dot(stri, sel) + carry_ref[...]
    total = excl[t - 1:t, :] + sel_ref[t - 1:t, :]
    carry_ref[...] = total
    cnt_ref[...] = total
    lane = _iota((t, LANES), 1).astype(F32)
    idxf = idx_ref[...]
    rank = jnp.zeros((t, LANES), F32)
    for r in range(TOP_K):
        mine = jnp.sum(jnp.where(lane == idxf[:, r:r + 1], excl, 0.0), axis=1, keepdims=True)
        rank = jnp.where(lane == float(r), mine, rank)
    rank_ref[...] = rank


def _rank(sel, idxf, t=512):
    s = sel.shape[0]
    spec = pl.BlockSpec((t, LANES), lambda i: (i, 0))
    return pl.pallas_call(
        _rank_kernel,
        grid=(s // t,),
        in_specs=[spec, spec],
        out_specs=[spec, pl.BlockSpec((1, LANES), lambda i: (0, 0))],
        out_shape=[jax.ShapeDtypeStruct((s, LANES), F32), jax.ShapeDtypeStruct((1, LANES), F32)],
        scratch_shapes=[pltpu.VMEM((1, LANES), F32)],
        compiler_params=_params(("arbitrary",)),
    )(sel, idxf)


def _dispatch_kernel(dest_ref, x_ref, xin_ref, sem):
    i = pl.program_id(0)
    n = dest_ref.shape[0] // TOP_K

    def row_copy(src_row, dst_row):
        return pltpu.make_async_copy(x_ref.at[src_row], xin_ref.at[dst_row], sem)

    def issue(t, _):
        for k in range(TOP_K):
            row_copy(i * n + t, dest_ref[t * TOP_K + k]).start()
        return 0

    lax.fori_loop(0, n, issue, 0)

    def drain(t, _):
        for k in range(TOP_K):
            row_copy(i * n + t, dest_ref[t * TOP_K + k]).wait()
        return 0

    lax.fori_loop(0, n, drain, 0)


def _dispatch(dest_flat, x1_rows, n_rows, td=512):
    s = x1_rows.shape[0]
    return pl.pallas_call(
        _dispatch_kernel,
        grid=(s // td,),
        in_specs=[pl.BlockSpec((td * TOP_K,), lambda i: (i,), memory_space=pltpu.SMEM),
                  pl.BlockSpec(memory_space=pl.ANY)],
        out_specs=pl.BlockSpec(memory_space=pl.ANY),
        out_shape=jax.ShapeDtypeStruct((n_rows, ROW_TILE, LANES), F32),
        scratch_shapes=[pltpu.SemaphoreType.DMA(())],
        compiler_params=pltpu.CompilerParams(dimension_semantics=("arbitrary",),
                                             has_side_effects=True),
    )(dest_flat, x1_rows)


def _expert_kernel(be_ref, nv_ref, xin_ref, wg_ref, wu_ref, bg_ref, bu_ref, wd_ref, bd_ref, y_ref):
    b = pl.program_id(0)
    blk = y_ref.shape[0] // ROW_TILE
    nvalid = nv_ref[b]

    @pl.when(nvalid > 0)
    def _():
        x = _load_row_tiles(xin_ref, blk)
        x = jnp.where(_iota((blk, D_MODEL), 0) < nvalid, x, 0.0).astype(BF16)
        gate = jnp.minimum(_dot(x, wg_ref[0]) + bg_ref[0], SWIGLU_LIMIT)
        up = jnp.clip(_dot(x, wu_ref[0]) + bu_ref[0], -SWIGLU_LIMIT, SWIGLU_LIMIT)
        act = (up + 1.0) * (gate * _sigmoid(SWIGLU_ALPHA * gate))
        _store_row_tiles(y_ref, _dot(act.astype(BF16), wd_ref[0]) + bd_ref[0])

    @pl.when(nvalid <= 0)
    def _():
        y_ref[...] = jnp.zeros_like(y_ref)


def _experts(block_expert, block_valid, xin2d, wg, wu, bg, bu, wd, bd):
    blk = EXPERT_BLOCK
    nb = block_expert.shape[0]
    w_spec = pl.BlockSpec((1, D_MODEL, D_MODEL), lambda b, be, nv: (be[b], 0, 0))
    b_spec = pl.BlockSpec((1, 1, D_MODEL), lambda b, be, nv: (be[b], 0, 0))
    x_spec = pl.BlockSpec((blk * ROW_TILE, LANES), lambda b, be, nv: (b, 0))
    return pl.pallas_call(
        _expert_kernel,
        grid_spec=pltpu.PrefetchScalarGridSpec(
            num_scalar_prefetch=2,
            grid=(nb,),
            in_specs=[x_spec, w_spec, w_spec, b_spec, b_spec, w_spec, b_spec],
            out_specs=x_spec),
        out_shape=jax.ShapeDtypeStruct((nb * blk * ROW_TILE, LANES), F32),
        compiler_params=_params(("arbitrary",)),
    )(block_expert, block_valid, xin2d, wg, wu, bg, bu, wd, bd)


def _combine_kernel(dest_ref, yb_ref, gw_ref, x1_ref, lg_ref, lb_ref, out_ref, buf_ref, sem):
    tc = out_ref.shape[0]

    def row_copy(t, k):
        dst = buf_ref.at[k, pl.ds(pl.multiple_of(t * ROW_TILE, ROW_TILE), ROW_TILE), :]
        return pltpu.make_async_copy(yb_ref.at[dest_ref[t * TOP_K + k]], dst, sem)

    def issue(t, _):
        for k in range(TOP_K):
            row_copy(t, k).start()
        return 0

    lax.fori_loop(0, tc, issue, 0)

    def drain(t, _):
        for k in range(TOP_K):
            row_copy(t, k).wait()
        return 0

    lax.fori_loop(0, tc, drain, 0)

    gw = gw_ref[...]
    f = jnp.zeros((tc, D_MODEL), F32)
    for k in range(TOP_K):
        f += gw[:, k:k + 1] * _load_row_tiles(buf_ref.at[k], tc)
    z = ALPHA * _load_row_tiles(x1_ref, tc) + f
    out_ref[...] = _layer_norm(z, lg_ref[...], lb_ref[...])


def _combine(dest_flat, yb_rows, gw, x1_2d, lg, lb, tc=256):
    s = gw.shape[0]
    row = pl.BlockSpec((1, D_MODEL), lambda i: (0, 0))
    return pl.pallas_call(
        _combine_kernel,
        grid=(s // tc,),
        in_specs=[pl.BlockSpec((tc * TOP_K,), lambda i: (i,), memory_space=pltpu.SMEM),
                  pl.BlockSpec(memory_space=pl.ANY),
                  pl.BlockSpec((tc, LANES), lambda i: (i, 0)),
                  pl.BlockSpec((tc * ROW_TILE, LANES), lambda i: (i, 0)),
                  row, row],
        out_specs=pl.BlockSpec((tc, D_MODEL), lambda i: (i, 0)),
        out_shape=jax.ShapeDtypeStruct((s, D_MODEL), F32),
        scratch_shapes=[pltpu.VMEM((TOP_K, tc * ROW_TILE, LANES), F32),
                        pltpu.SemaphoreType.DMA(())],
        compiler_params=_params(("arbitrary",)),
    )(dest_flat, yb_rows, gw, x1_2d, lg, lb)


def _permute_w_in(w_in):
    sizes = (256, 256, 256, 4, 128, 128, 256, 16, 256, 256, 256, 256, 3 * D_MODEL)
    offs = [0]
    for n in sizes:
        offs.append(offs[-1] + n)
    fq, fk, fv, ff, gq, gk, gv, ga, gr, mq, mk, mv, gates = (
        w_in[:, offs[i]:offs[i + 1]] for i in range(len(sizes)))
    main = jnp.concatenate([gates, fq, fk, fv, mq, mk, mv, gq, gk, gv, gr], axis=1).astype(BF16)
    pad = jnp.zeros((D_MODEL, LANES - 4 - GLA_RANK), w_in.dtype)
    tail = jnp.concatenate([ff, ga, pad], axis=1).astype(BF16)
    return main, tail


def _pad_row(v, n=LANES, fill=0.0):
    return jnp.pad(v.astype(F32), (0, n - v.shape[0]), constant_values=fill)[None, :]


def _mixer_layer(x, w_in, fox_fb, gla_a_up, gla_a_b, gla_norm_g, w_branch, w_out, ln_g, ln_b,
                 w_router, b_router):
    w_main, w_tail = _permute_w_in(w_in)
    p, tail = _inproj(x, w_main, w_tail)

    fox_o = _flash(*_fox_prep(p, tail, _pad_row(fox_fb)))
    moba_o = _flash(*_moba_prep(p))
    aup_pad = jnp.zeros((LANES, GLA_KW), F32).at[TAIL_GA:TAIL_GA + GLA_RANK].set(gla_a_up)
    y_gla = _gla(p, tail, aup_pad, gla_a_b[None, :], gla_norm_g[None, :])

    wr = jnp.pad(w_router, ((0, 0), (0, LANES - N_EXPERTS)))
    wr_hi = wr.astype(BF16)
    wr_lo = (wr - wr_hi.astype(F32)).astype(BF16)
    return _merge(fox_o, y_gla, moba_o, p, x, w_branch.astype(BF16), w_out.astype(BF16),
                  ln_g[None, :], ln_b[None, :], wr_hi, wr_lo, _pad_row(b_router))


def _moe_layer(x1_2d, sel, idxf, gw, w_gu, b_gu, w_dn, b_dn, ln_g, ln_b):
    s = sel.shape[0]
    blk = EXPERT_BLOCK
    rank, counts = _rank(sel, idxf)

    counts = counts[0, :N_EXPERTS].astype(jnp.int32)
    padded = (counts + blk - 1) // blk * blk
    pad_end = jnp.cumsum(padded)
    pad_start = pad_end - padded
    top_i = idxf[:, :TOP_K].astype(jnp.int32)
    dest = (pad_start[top_i] + rank[:, :TOP_K].astype(jnp.int32)).reshape(-1)
    n_blocks = (s * TOP_K + N_EXPERTS * (blk - 1)) // blk + 1
    block_start = jnp.arange(n_blocks, dtype=jnp.int32) * blk
    block_expert = jnp.clip(jnp.searchsorted(pad_end, block_start, side='right'),
                            0, N_EXPERTS - 1).astype(jnp.int32)
    block_valid = jnp.clip(pad_start[block_expert] + counts[block_expert] - block_start, 0, blk)
    block_valid = jnp.where(block_start < pad_end[-1], block_valid, 0).astype(jnp.int32)
    last_active = jnp.maximum(pad_end[-1] // blk - 1, 0)
    block_expert = jnp.where(block_start < pad_end[-1], block_expert, block_expert[last_active])

    x1_rows = x1_2d.reshape(s, ROW_TILE, LANES)
    xin = _dispatch(dest, x1_rows, n_blocks * blk)
    wg = w_gu[:, :, 0::2].astype(BF16)
    wu = w_gu[:, :, 1::2].astype(BF16)
    yb = _experts(block_expert, block_valid, xin.reshape(n_blocks * blk * ROW_TILE, LANES),
                  wg, wu, b_gu[:, None, 0::2], b_gu[:, None, 1::2], w_dn.astype(BF16),
                  b_dn[:, None, :])
    return _combine(dest, yb.reshape(n_blocks * blk, ROW_TILE, LANES), gw, x1_2d,
                    ln_g[None, :], ln_b[None, :])


def kernel(x, w_in, fox_fb, gla_a_up, gla_a_b, gla_norm_g, w_branch, w_out, ln1_g, ln1_b,
           w_router, b_router, w_gu, b_gu, w_dn, b_dn, ln2_g, ln2_b):
    b, s, d = x.shape
    assert b == 1 and d == D_MODEL
    h = x.reshape(s, d)
    for l in range(DEPTH):
        x1_2d, sel, idxf, gw = _mixer_layer(
            h, w_in[l], fox_fb[l], gla_a_up[l], gla_a_b[l], gla_norm_g[l], w_branch[l], w_out[l],
            ln1_g[l], ln1_b[l], w_router[l], b_router[l])
        h = _moe_layer(x1_2d, sel, idxf, gw, w_gu[l], b_gu[l], w_dn[l], b_dn[l],
                       ln2_g[l], ln2_b[l])
    return h.reshape(b, s, d)
```

```python
import functools

import jax
import jax.numpy as jnp
from jax import lax
from jax.experimental import pallas as pl
from jax.experimental.pallas import tpu as pltpu

F32 = jnp.float32
BF16 = jnp.bfloat16

D_MODEL = 1024
DEPTH = 4
N_HEADS = 4
HEAD_DIM = 64
GLA_DK = 32
GLA_KW = N_HEADS * GLA_DK
GLA_RANK = 16
GLA_TAU = 16.0
MOBA_BLOCK = 256
MOBA_TOPK = 3
BRANCH_WIDTH = 256
N_EXPERTS = 32
TOP_K = 4
SWIGLU_LIMIT = 7.0
SWIGLU_ALPHA = 1.702
ALPHA = (2 * DEPTH) ** 0.25
LN_EPS = 1e-5
RMS_EPS = 1e-6
LOG2E = 1.4426950408889634

LANES = 128
ROW_TILE = 8
NEG = -1e30
UNSELECTED = -32768.0
VMEM_LIMIT = 48 * 1024 * 1024

P_GATES = 0
P_FOX = 3 * D_MODEL
P_MOBA = P_FOX + 3 * BRANCH_WIDTH
P_GLA = P_MOBA + 3 * BRANCH_WIDTH
P_WIDTH = P_GLA + 2 * GLA_KW + 2 * BRANCH_WIDTH
TAIL_FF = 0
TAIL_GA = 4

EXPERT_BLOCK = 256


def _params(sem):
    return pltpu.CompilerParams(dimension_semantics=sem, vmem_limit_bytes=VMEM_LIMIT)


def _split3(x):
    hi = x.astype(BF16)
    r1 = x - hi.astype(F32)
    mid = r1.astype(BF16)
    lo = (r1 - mid.astype(F32)).astype(BF16)
    return hi, mid, lo


def _split2(x):
    hi = x.astype(BF16)
    lo = (x - hi.astype(F32)).astype(BF16)
    return hi, lo


def _dot(a, b):
    return jnp.dot(a, b, preferred_element_type=F32)


def _dot_nt(a, b):
    return lax.dot_general(a, b, (((1,), (1,)), ((), ())), preferred_element_type=F32)


def _dot_tn(a, b):
    return lax.dot_general(a, b, (((0,), (0,)), ((), ())), preferred_element_type=F32)


def _log_sigmoid(t):
    return jnp.minimum(t, 0.0) - jnp.log1p(jnp.exp(-jnp.abs(t)))


def _sigmoid(t):
    return 1.0 / (1.0 + jnp.exp(-t))


def _iota(shape, axis):
    return lax.broadcasted_iota(jnp.int32, shape, axis)


def _transpose_bf16(x):
    return x.astype(F32).T.astype(BF16)


def _aug_specs(s, t):
    aug = jax.ShapeDtypeStruct((N_HEADS, s, LANES), BF16)
    aug_spec = pl.BlockSpec((N_HEADS, t, LANES), lambda i: (0, i, 0))
    aug_t = jax.ShapeDtypeStruct((N_HEADS, LANES, s), BF16)
    aug_t_spec = pl.BlockSpec((N_HEADS, LANES, t), lambda i: (0, 0, i))
    return aug, aug_spec, aug_t, aug_t_spec


def _inproj_kernel(x_ref, w_ref, wt_ref, sc_ref, p_ref, t_ref, xb_ref):
    @pl.when(pl.program_id(1) == 0)
    def _():
        xb_ref[...] = x_ref[...].astype(BF16)
        t_ref[...] = _dot(xb_ref[...], wt_ref[...])

    p_ref[...] = (_dot(xb_ref[...], w_ref[...]) * sc_ref[...]).astype(BF16)


def _inproj(x, w, wt, tm=1024, tn=768):
    s = x.shape[0]
    col = jnp.arange(P_WIDTH)
    is_q = ((col >= P_FOX) & (col < P_FOX + BRANCH_WIDTH)) | ((col >= P_MOBA) & (col < P_MOBA + BRANCH_WIDTH))
    scale = jnp.where(is_q, HEAD_DIM ** -0.5 * LOG2E, 1.0).astype(F32)[None, :]
    return pl.pallas_call(
        _inproj_kernel,
        grid=(s // tm, P_WIDTH // tn),
        in_specs=[pl.BlockSpec((tm, D_MODEL), lambda i, j: (i, 0)),
                  pl.BlockSpec((D_MODEL, tn), lambda i, j: (0, j)),
                  pl.BlockSpec((D_MODEL, LANES), lambda i, j: (0, 0)),
                  pl.BlockSpec((1, tn), lambda i, j: (0, j))],
        out_specs=[pl.BlockSpec((tm, tn), lambda i, j: (i, j)),
                   pl.BlockSpec((tm, LANES), lambda i, j: (i, 0))],
        out_shape=[jax.ShapeDtypeStruct((s, P_WIDTH), BF16),
                   jax.ShapeDtypeStruct((s, LANES), F32)],
        scratch_shapes=[pltpu.VMEM((tm, D_MODEL), BF16)],
        compiler_params=_params(("parallel", "arbitrary")),
    )(x, w, wt, scale)


def _fox_prep_kernel(q_ref, k_ref, v_ref, t_ref, fb_ref, qa_ref, ka_ref, va_ref, carry_ref):
    t = q_ref.shape[0]

    @pl.when(pl.program_id(0) == 0)
    def _():
        carry_ref[...] = jnp.zeros_like(carry_ref)

    ls = _log_sigmoid(t_ref[...] + fb_ref[...])
    tri = (_iota((t, t), 0) >= _iota((t, t), 1)).astype(BF16)
    hi, mid, lo = _split3(ls)
    c = _dot(tri, hi) + _dot(tri, mid) + _dot(tri, lo) + carry_ref[...]
    carry_ref[...] = c[t - 1:t, :]

    lane = _iota((t, HEAD_DIM), 1)
    q_aug = jnp.where(lane < 3, 1.0, 0.0).astype(BF16)
    v_aug = jnp.where(lane == 0, 1.0, 0.0).astype(BF16)
    q = q_ref[...]
    k = k_ref[...]
    v = v_ref[...]
    for h in range(N_HEADS):
        sl = slice(h * HEAD_DIM, (h + 1) * HEAD_DIM)
        nhi, nmid, nlo = (piece.astype(F32) for piece in _split3(-LOG2E * c[:, h:h + 1]))
        k_aug = jnp.where(lane == 0, nhi, jnp.where(lane == 1, nmid, jnp.where(lane == 2, nlo, 0.0)))
        qa_ref[h] = _transpose_bf16(jnp.concatenate([q[:, sl], q_aug], axis=1))
        ka_ref[h] = jnp.concatenate([k[:, sl], k_aug.astype(BF16)], axis=1)
        va_ref[h] = _transpose_bf16(jnp.concatenate([v[:, sl], v_aug], axis=1))


def _fox_prep(p, tail, fb_row, t=512):
    s = p.shape[0]
    cb = P_FOX // BRANCH_WIDTH
    aug, aug_spec, aug_t, aug_t_spec = _aug_specs(s, t)
    return pl.pallas_call(
        _fox_prep_kernel,
        grid=(s // t,),
        in_specs=[pl.BlockSpec((t, BRANCH_WIDTH), lambda i: (i, cb)),
                  pl.BlockSpec((t, BRANCH_WIDTH), lambda i: (i, cb + 1)),
                  pl.BlockSpec((t, BRANCH_WIDTH), lambda i: (i, cb + 2)),
                  pl.BlockSpec((t, LANES), lambda i: (i, 0)),
                  pl.BlockSpec((1, LANES), lambda i: (0, 0))],
        out_specs=[aug_t_spec, aug_spec, aug_t_spec],
        out_shape=[aug_t, aug, aug_t],
        scratch_shapes=[pltpu.VMEM((1, LANES), F32)],
        compiler_params=_params(("arbitrary",)),
    )(p, p, p, tail, fb_row)


def _moba_prep_kernel(q_ref, k_ref, v_ref, qa_ref, ka_ref, va_ref, kmean_ref):
    i = pl.program_id(0)
    t = q_ref.shape[0]
    nbl = kmean_ref.shape[0]

    @pl.when(i == 0)
    def _():
        kmean_ref[...] = jnp.zeros_like(kmean_ref)

    q = q_ref[...]
    k = k_ref[...]
    v = v_ref[...]
    km = kmean_ref[...]
    lane = _iota((t, HEAD_DIM), 1)
    v_aug = jnp.where(lane == 0, 1.0, 0.0).astype(BF16)
    k_aug = jnp.where(lane == i, 1.0, 0.0).astype(BF16)
    for h in range(N_HEADS):
        sl = slice(h * HEAD_DIM, (h + 1) * HEAD_DIM)
        kh_hi, kh_lo = _split2(km[:HEAD_DIM, sl])
        gate = _dot_nt(q[:, sl], kh_hi) + _dot_nt(q[:, sl], kh_lo)
        g = jnp.where(lane < i, gate, NEG)
        bias = jnp.where(lane == i, 0.0, UNSELECTED)
        for _ in range(MOBA_TOPK):
            mx = jnp.max(g, axis=1, keepdims=True)
            idx = jnp.min(jnp.where(g == mx, lane, HEAD_DIM), axis=1, keepdims=True)
            pick = jnp.logical_and(lane == idx, mx > 0.5 * NEG)
            bias = jnp.where(pick, 0.0, bias)
            g = jnp.where(lane == idx, NEG, g)
        qa_ref[h] = _transpose_bf16(jnp.concatenate([q[:, sl], bias.astype(BF16)], axis=1))
        ka_ref[h] = jnp.concatenate([k[:, sl], k_aug], axis=1)
        va_ref[h] = _transpose_bf16(jnp.concatenate([v[:, sl], v_aug], axis=1))
    kmean = jnp.mean(k.astype(F32), axis=0, keepdims=True)
    kmean_ref[...] = jnp.where(_iota((nbl, BRANCH_WIDTH), 0) == i, kmean, km)


def _moba_prep(p):
    s = p.shape[0]
    t = MOBA_BLOCK
    assert s % t == 0 and s // t <= HEAD_DIM, "block one-hot must fit the 64 augmentation lanes"
    cb = P_MOBA // BRANCH_WIDTH
    aug, aug_spec, aug_t, aug_t_spec = _aug_specs(s, t)
    return pl.pallas_call(
        _moba_prep_kernel,
        grid=(s // t,),
        in_specs=[pl.BlockSpec((t, BRANCH_WIDTH), lambda i: (i, cb)),
                  pl.BlockSpec((t, BRANCH_WIDTH), lambda i: (i, cb + 1)),
                  pl.BlockSpec((t, BRANCH_WIDTH), lambda i: (i, cb + 2))],
        out_specs=[aug_t_spec, aug_spec, aug_t_spec],
        out_shape=[aug_t, aug, aug_t],
        scratch_shapes=[pltpu.VMEM((HEAD_DIM, BRANCH_WIDTH), F32)],
        compiler_params=_params(("arbitrary",)),
    )(p, p, p)


def _flash_kernel(qt_ref, k_ref, vt_ref, o_ref, *, tq, tk):
    i = pl.program_id(1)
    qt = qt_ref[0]

    def tile(j, carry, masked):
        m, acc = carry
        off = pl.multiple_of(j * tk, tk)
        s = _dot(k_ref[0, pl.ds(off, tk), :], qt)
        if masked:
            key = _iota((tk, tq), 0) + j * tk
            qry = _iota((tk, tq), 1) + i * tq
            s = jnp.where(key <= qry, s, NEG)
        m_new = jnp.maximum(m, jnp.max(s, axis=0, keepdims=True))
        p = jnp.exp2(s - m_new)
        acc = jnp.exp2(m - m_new) * acc + _dot(vt_ref[0, :, pl.ds(off, tk)], p.astype(BF16))
        return m_new, acc

    carry = (jnp.full((1, tq), NEG, F32), jnp.zeros((LANES, tq), F32))
    n_diag = tq // tk
    carry = lax.fori_loop(0, i * n_diag, lambda j, c: tile(j, c, False), carry)
    for d in range(n_diag):
        carry = tile(i * n_diag + d, carry, True)
    _, acc = carry
    o_ref[0] = (acc / acc[HEAD_DIM:HEAD_DIM + 1, :]).T.astype(o_ref.dtype)


def _flash(qt, ka, vt, tq=1024, tk=512):
    nh, s, _ = ka.shape
    tq = min(tq, s)
    return pl.pallas_call(
        functools.partial(_flash_kernel, tq=tq, tk=tk),
        grid=(nh, s // tq),
        in_specs=[pl.BlockSpec((1, LANES, tq), lambda h, i: (h, 0, i)),
                  pl.BlockSpec((1, s, LANES), lambda h, i: (h, 0, 0)),
                  pl.BlockSpec((1, LANES, s), lambda h, i: (h, 0, 0))],
        out_specs=pl.BlockSpec((1, tq, LANES), lambda h, i: (h, i, 0)),
        out_shape=jax.ShapeDtypeStruct((nh, s, LANES), BF16),
        compiler_params=_params(("parallel", "parallel")),
    )(qt, ka, vt)


GLA_SUB = 16


def _gla_kernel(q_ref, k_ref, v_ref, r_ref, t_ref, aup_ref, ab_ref, g_ref, y_ref,
                st_ref, b_ref, o_ref):
    t = q_ref.shape[0]
    c = GLA_SUB

    @pl.when(pl.program_id(0) == 0)
    def _():
        st_ref[...] = jnp.zeros_like(st_ref)

    t_hi, t_lo = _split2(t_ref[...])
    a_hi, a_lo = _split2(aup_ref[...])
    z = _dot(t_hi, a_hi) + _dot(t_lo, a_hi) + _dot(t_hi, a_lo) + ab_ref[...]
    log_a = _log_sigmoid(z) * (1.0 / GLA_TAU)
    row = _iota((t, t), 0)
    col = _iota((t, t), 1)
    tri = jnp.logical_and(row // c == col // c, row >= col).astype(BF16)
    hi, mid, lo = _split3(log_a)
    b_ref[...] = _dot(tri, hi) + _dot(tri, mid) + _dot(tri, lo)

    expand = (_iota((GLA_KW, BRANCH_WIDTH), 0) // GLA_DK
              == _iota((GLA_KW, BRANCH_WIDTH), 1) // HEAD_DIM).astype(BF16)
    st_mask = (_iota((BRANCH_WIDTH, GLA_KW), 0) // HEAD_DIM
               == _iota((BRANCH_WIDTH, GLA_KW), 1) // GLA_DK).astype(F32)
    srow = _iota((c, GLA_KW), 0)
    orow = _iota((c, BRANCH_WIDTH), 0)
    scale = GLA_DK ** -0.5

    def step(r, _):
        r0 = pl.multiple_of(r * c, c)
        qs = q_ref[pl.ds(r0, c), :].astype(F32) * scale
        ks = k_ref[pl.ds(r0, c), :].astype(F32)
        vb = v_ref[pl.ds(r0, c), :]
        vf = vb.astype(F32)
        bs = b_ref[pl.ds(r0, c), :]
        b_last = bs[c - 1:c, :]
        st = st_ref[...]

        o_inter = _dot_nt((qs * jnp.exp(bs)).astype(BF16), st.astype(BF16))

        pieces = []
        for tt in range(c):
            e = jnp.exp(jnp.where(srow <= tt, bs[tt:tt + 1, :] - bs, NEG))
            pieces.append(e * qs[tt:tt + 1, :] * ks)
        pm = jnp.concatenate(pieces, axis=0).astype(BF16)
        a = _dot(pm, expand)
        o_intra = jnp.zeros((c, BRANCH_WIDTH), F32)
        for tt in range(c):
            o_t = jnp.sum(a[tt * c:(tt + 1) * c, :] * vf, axis=0, keepdims=True)
            o_intra = jnp.where(orow == tt, o_t, o_intra)
        o_ref[pl.ds(r0, c), :] = o_inter + o_intra

        ke = (ks * jnp.exp(b_last - bs)).astype(BF16)
        st_ref[...] = st * jnp.exp(b_last) + _dot_tn(vb, ke) * st_mask
        return 0

    lax.fori_loop(0, t // c, step, 0)

    o = o_ref[...]
    ones_bd = (_iota((BRANCH_WIDTH, BRANCH_WIDTH), 0) // HEAD_DIM
               == _iota((BRANCH_WIDTH, BRANCH_WIDTH), 1) // HEAD_DIM).astype(BF16)
    sq_hi, sq_mid, sq_lo = _split3(o * o)
    ms = (_dot(sq_hi, ones_bd) + _dot(sq_mid, ones_bd) + _dot(sq_lo, ones_bd)) * (1.0 / HEAD_DIM)
    gr = r_ref[...].astype(F32)
    y = o * lax.rsqrt(ms + RMS_EPS) * g_ref[...] * (gr * _sigmoid(gr))
    y_ref[...] = y.astype(y_ref.dtype)


def _gla(p, tail, aup_pad, ab_row, g_row, t=512):
    s = p.shape[0]
    cq = P_GLA // GLA_KW
    cv = (P_GLA + 2 * GLA_KW) // BRANCH_WIDTH
    return pl.pallas_call(
        _gla_kernel,
        grid=(s // t,),
        in_specs=[pl.BlockSpec((t, GLA_KW), lambda i: (i, cq)),
                  pl.BlockSpec((t, GLA_KW), lambda i: (i, cq + 1)),
                  pl.BlockSpec((t, BRANCH_WIDTH), lambda i: (i, cv)),
                  pl.BlockSpec((t, BRANCH_WIDTH), lambda i: (i, cv + 1)),
                  pl.BlockSpec((t, LANES), lambda i: (i, 0)),
                  pl.BlockSpec((LANES, GLA_KW), lambda i: (0, 0)),
                  pl.BlockSpec((1, GLA_KW), lambda i: (0, 0)),
                  pl.BlockSpec((1, BRANCH_WIDTH), lambda i: (0, 0))],
        out_specs=pl.BlockSpec((t, BRANCH_WIDTH), lambda i: (i, 0)),
        out_shape=jax.ShapeDtypeStruct((s, BRANCH_WIDTH), BF16),
        scratch_shapes=[pltpu.VMEM((BRANCH_WIDTH, GLA_KW), F32),
                        pltpu.VMEM((t, GLA_KW), F32),
                        pltpu.VMEM((t, BRANCH_WIDTH), F32)],
        compiler_params=_params(("arbitrary",)),
    )(p, p, p, p, tail, aup_pad, ab_row, g_row)


def _layer_norm(z, g, b):
    mu = jnp.mean(z, axis=1, keepdims=True)
    zc = z - mu
    var = jnp.mean(zc * zc, axis=1, keepdims=True)
    return zc * lax.rsqrt(var + LN_EPS) * g + b


def _store_row_tiles(ref, val):
    n = val.shape[0]
    for a in range(ROW_TILE):
        ref[pl.ds(a, n, stride=ROW_TILE), :] = val[:, a * LANES:(a + 1) * LANES]


def _load_row_tiles(ref, n, base=0):
    return jnp.concatenate(
        [ref[pl.ds(base + a, n, stride=ROW_TILE), :] for a in range(ROW_TILE)], axis=1)


def _merge_kernel(fox_ref, gla_ref, moba_ref, g0_ref, g1_ref, g2_ref, x_ref, wb_ref, wo_ref,
                  lg_ref, lb_ref, wrh_ref, wrl_ref, br_ref,
                  x1_ref, sel_ref, idx_ref, gw_ref):
    tm = x_ref.shape[0]

    def heads(ref):
        return jnp.concatenate([ref[h][:, :HEAD_DIM] for h in range(N_HEADS)], axis=1)

    merged = _sigmoid(g0_ref[...].astype(F32)) * _dot(heads(fox_ref), wb_ref[0])
    merged += _sigmoid(g1_ref[...].astype(F32)) * _dot(gla_ref[...], wb_ref[1])
    merged += _sigmoid(g2_ref[...].astype(F32)) * _dot(heads(moba_ref), wb_ref[2])
    z = ALPHA * x_ref[...] + _dot(merged.astype(BF16), wo_ref[...])
    x1 = _layer_norm(z, lg_ref[...], lb_ref[...])
    _store_row_tiles(x1_ref, x1)

    x_hi, x_lo = _split2(x1)
    logits = (_dot(x_hi, wrh_ref[...]) + _dot(x_lo, wrh_ref[...]) + _dot(x_hi, wrl_ref[...])
              + br_ref[...])
    lane = _iota((tm, LANES), 1)
    lg = jnp.where(lane < N_EXPERTS, logits, NEG)
    sel = jnp.zeros((tm, LANES), F32)
    idxf = jnp.zeros((tm, LANES), F32)
    ew = jnp.zeros((tm, LANES), F32)
    top = None
    for r in range(TOP_K):
        mx = jnp.max(lg, axis=1, keepdims=True)
        idx = jnp.min(jnp.where(lg == mx, lane, LANES), axis=1, keepdims=True)
        hit = lane == idx
        top = mx if top is None else top
        sel = jnp.where(hit, 1.0, sel)
        idxf = jnp.where(lane == r, idx.astype(F32), idxf)
        ew = jnp.where(lane == r, jnp.exp(mx - top), ew)
        lg = jnp.where(hit, NEG, lg)
    sel_ref[...] = sel
    idx_ref[...] = idxf
    gw_ref[...] = ew / jnp.sum(ew, axis=1, keepdims=True)


def _merge(fox_o, y_gla, moba_o, p, x, wb, wo, lg, lb, wrh, wrl, br, tm=256):
    s = x.shape[0]
    head_spec = pl.BlockSpec((N_HEADS, tm, LANES), lambda i: (0, i, 0))
    row = lambda n: pl.BlockSpec((1, n), lambda i: (0, 0))
    small = jax.ShapeDtypeStruct((s, LANES), F32)
    small_spec = pl.BlockSpec((tm, LANES), lambda i: (i, 0))
    return pl.pallas_call(
        _merge_kernel,
        grid=(s // tm,),
        in_specs=[head_spec,
                  pl.BlockSpec((tm, BRANCH_WIDTH), lambda i: (i, 0)),
                  head_spec,
                  pl.BlockSpec((tm, D_MODEL), lambda i: (i, 0)),
                  pl.BlockSpec((tm, D_MODEL), lambda i: (i, 1)),
                  pl.BlockSpec((tm, D_MODEL), lambda i: (i, 2)),
                  pl.BlockSpec((tm, D_MODEL), lambda i: (i, 0)),
                  pl.BlockSpec((3, BRANCH_WIDTH, D_MODEL), lambda i: (0, 0, 0)),
                  pl.BlockSpec((D_MODEL, D_MODEL), lambda i: (0, 0)),
                  row(D_MODEL), row(D_MODEL),
                  pl.BlockSpec((D_MODEL, LANES), lambda i: (0, 0)),
                  pl.BlockSpec((D_MODEL, LANES), lambda i: (0, 0)),
                  row(LANES)],
        out_specs=[pl.BlockSpec((tm * ROW_TILE, LANES), lambda i: (i, 0)),
                   small_spec, small_spec, small_spec],
        out_shape=[jax.ShapeDtypeStruct((s * ROW_TILE, LANES), F32), small, small, small],
        compiler_params=_params(("parallel",)),
    )(fox_o, y_gla, moba_o, p, p, p, x, wb, wo, lg, lb, wrh, wrl, br)


def _rank_kernel(sel_ref, idx_ref, rank_ref, cnt_ref, carry_ref):
    t = sel_ref.shape[0]

    @pl.when(pl.program_id(0) == 0)
    def _():
        carry_ref[...] = jnp.zeros_like(carry_ref)

    sel = sel_ref[...].astype(BF16)
    stri = (_iota((t, t), 0) > _iota((t, t), 1)).astype(BF16)
    excl = _dot(stri, sel) + carry_ref[...]
    total = excl[t - 1:t, :] + sel_ref[t - 1:t, :]
    carry_ref[...] = total
    cnt_ref[...] = total
    lane = _iota((t, LANES), 1).astype(F32)
    idxf = idx_ref[...]
    rank = jnp.zeros((t, LANES), F32)
    for r in range(TOP_K):
        mine = jnp.sum(jnp.where(lane == idxf[:, r:r + 1], excl, 0.0), axis=1, keepdims=True)
        rank = jnp.where(lane == float(r), mine, rank)
    rank_ref[...] = rank


def _rank(sel, idxf, t=512):
    s = sel.shape[0]
    spec = pl.BlockSpec((t, LANES), lambda i: (i, 0))
    return pl.pallas_call(
        _rank_kernel,
        grid=(s // t,),
        in_specs=[spec, spec],
        out_specs=[spec, pl.BlockSpec((1, LANES), lambda i: (0, 0))],
        out_shape=[jax.ShapeDtypeStruct((s, LANES), F32), jax.ShapeDtypeStruct((1, LANES), F32)],
        scratch_shapes=[pltpu.VMEM((1, LANES), F32)],
        compiler_params=_params(("arbitrary",)),
    )(sel, idxf)


def _dispatch_kernel(dest_ref, x_ref, xin_ref, sem):
    n = dest_ref.shape[0] // TOP_K

    def row_copy(t, k):
        src = x_ref.at[pl.ds(pl.multiple_of(t * ROW_TILE, ROW_TILE), ROW_TILE), :]
        return pltpu.make_async_copy(src, xin_ref.at[dest_ref[t * TOP_K + k]], sem)

    def issue(t, _):
        for k in range(TOP_K):
            row_copy(t, k).start()
        return 0

    lax.fori_loop(0, n, issue, 0)

    def drain(t, _):
        for k in range(TOP_K):
            row_copy(t, k).wait()
        return 0

    lax.fori_loop(0, n, drain, 0)


def _dispatch(dest_flat, x1_2d, n_rows, td=512):
    s = x1_2d.shape[0] // ROW_TILE
    return pl.pallas_call(
        _dispatch_kernel,
        grid=(s // td,),
        in_specs=[pl.BlockSpec((td * TOP_K,), lambda i: (i,), memory_space=pltpu.SMEM),
                  pl.BlockSpec((td * ROW_TILE, LANES), lambda i: (i, 0))],
        out_specs=pl.BlockSpec(memory_space=pl.ANY),
        out_shape=jax.ShapeDtypeStruct((n_rows, ROW_TILE, LANES), F32),
        scratch_shapes=[pltpu.SemaphoreType.DMA(())],
        compiler_params=pltpu.CompilerParams(dimension_semantics=("arbitrary",),
                                             has_side_effects=True),
    )(dest_flat, x1_2d)


HALF = LANES // 2
W_CAST_ROWS = 64


def _expert_kernel(be_ref, nv_ref, xin_ref, wgu_ref, bgu_ref, wdn_ref, bdn_ref, y_ref,
                   wgu_s, wdn_s, perm_s):
    b = pl.program_id(0)
    blk = y_ref.shape[0] // ROW_TILE
    nvalid = nv_ref[b]
    new_expert = jnp.logical_or(b == 0, be_ref[b] != be_ref[jnp.maximum(b - 1, 0)])

    @pl.when(jnp.logical_and(new_expert, nvalid > 0))
    def _():
        def cast(r, _):
            r0 = pl.multiple_of(r * W_CAST_ROWS, W_CAST_ROWS)
            wgu_s[pl.ds(r0, W_CAST_ROWS), :] = wgu_ref[0, pl.ds(r0, W_CAST_ROWS), :].astype(BF16)
            return 0

        lax.fori_loop(0, D_MODEL // W_CAST_ROWS, cast, 0)
        for c in range(D_MODEL // LANES):
            cols = slice(c * LANES, (c + 1) * LANES)
            for g in range(D_MODEL // LANES):
                lo = g * LANES
                perm_s[c, pl.ds(lo, HALF, stride=2), :] = wdn_ref[0, lo:lo + HALF, cols]
                perm_s[c, pl.ds(lo + 1, HALF, stride=2), :] = wdn_ref[0, lo + HALF:lo + LANES, cols]
            wdn_s[:, cols] = perm_s[c].astype(BF16)

    @pl.when(nvalid > 0)
    def _():
        x = _load_row_tiles(xin_ref, blk)
        x = jnp.where(_iota((blk, D_MODEL), 0) < nvalid, x, 0.0).astype(BF16)
        even = _iota((blk, LANES), 1) % 2 == 0
        acts = []
        for g in range(D_MODEL // LANES):
            lo = g * 2 * LANES
            h = _dot(x, wgu_s[:, lo:lo + 2 * LANES]) + bgu_ref[0, :, lo:lo + 2 * LANES]
            h_a = h[:, :LANES]
            h_b = h[:, LANES:]
            gate = jnp.where(even, h_a, pltpu.roll(h_b, 1, 1))
            up = jnp.where(even, pltpu.roll(h_a, LANES - 1, 1), h_b)
            gate = jnp.minimum(gate, SWIGLU_LIMIT)
            up = jnp.clip(up, -SWIGLU_LIMIT, SWIGLU_LIMIT)
            acts.append(((up + 1.0) * (gate * _sigmoid(SWIGLU_ALPHA * gate))).astype(BF16))
        act = jnp.concatenate(acts, axis=1)
        _store_row_tiles(y_ref, _dot(act, wdn_s[...]) + bdn_ref[0])

    @pl.when(nvalid <= 0)
    def _():
        y_ref[...] = jnp.zeros_like(y_ref)


def _experts(block_expert, block_valid, xin2d, w_gu, b_gu, w_dn, b_dn):
    blk = EXPERT_BLOCK
    nb = block_expert.shape[0]
    x_spec = pl.BlockSpec((blk * ROW_TILE, LANES), lambda b, be, nv: (b, 0))
    return pl.pallas_call(
        _expert_kernel,
        grid_spec=pltpu.PrefetchScalarGridSpec(
            num_scalar_prefetch=2,
            grid=(nb,),
            in_specs=[x_spec,
                      pl.BlockSpec((1, D_MODEL, 2 * D_MODEL), lambda b, be, nv: (be[b], 0, 0)),
                      pl.BlockSpec((1, 1, 2 * D_MODEL), lambda b, be, nv: (be[b], 0, 0)),
                      pl.BlockSpec((1, D_MODEL, D_MODEL), lambda b, be, nv: (be[b], 0, 0)),
                      pl.BlockSpec((1, 1, D_MODEL), lambda b, be, nv: (be[b], 0, 0))],
            out_specs=x_spec,
            scratch_shapes=[pltpu.VMEM((D_MODEL, 2 * D_MODEL), BF16),
                            pltpu.VMEM((D_MODEL, D_MODEL), BF16),
                            pltpu.VMEM((D_MODEL // LANES, D_MODEL, LANES), F32)]),
        out_shape=jax.ShapeDtypeStruct((nb * blk * ROW_TILE, LANES), F32),
        compiler_params=pltpu.CompilerParams(dimension_semantics=("arbitrary",),
                                             vmem_limit_bytes=56 * 1024 * 1024),
    )(block_expert, block_valid, xin2d, w_gu, b_gu[:, None, :], w_dn, b_dn[:, None, :])


def _combine_kernel(dest_ref, yb_ref, gw_ref, x1_ref, lg_ref, lb_ref, out_ref, buf_ref, sem):
    tc = out_ref.shape[0]

    def row_copy(t, k):
        dst = buf_ref.at[k, pl.ds(pl.multiple_of(t * ROW_TILE, ROW_TILE), ROW_TILE), :]
        return pltpu.make_async_copy(yb_ref.at[dest_ref[t * TOP_K + k]], dst, sem)

    def issue(t, _):
        for k in range(TOP_K):
            row_copy(t, k).start()
        return 0

    lax.fori_loop(0, tc, issue, 0)

    def drain(t, _):
        for k in range(TOP_K):
            row_copy(t, k).wait()
        return 0

    lax.fori_loop(0, tc, drain, 0)

    gw = gw_ref[...]
    f = jnp.zeros((tc, D_MODEL), F32)
    for k in range(TOP_K):
        f += gw[:, k:k + 1] * _load_row_tiles(buf_ref.at[k], tc)
    z = ALPHA * _load_row_tiles(x1_ref, tc) + f
    out_ref[...] = _layer_norm(z, lg_ref[...], lb_ref[...])


def _combine(dest_flat, yb_rows, gw, x1_2d, lg, lb, tc=256):
    s = gw.shape[0]
    row = pl.BlockSpec((1, D_MODEL), lambda i: (0, 0))
    return pl.pallas_call(
        _combine_kernel,
        grid=(s // tc,),
        in_specs=[pl.BlockSpec((tc * TOP_K,), lambda i: (i,), memory_space=pltpu.SMEM),
                  pl.BlockSpec(memory_space=pl.ANY),
                  pl.BlockSpec((tc, LANES), lambda i: (i, 0)),
                  pl.BlockSpec((tc * ROW_TILE, LANES), lambda i: (i, 0)),
                  row, row],
        out_specs=pl.BlockSpec((tc, D_MODEL), lambda i: (i, 0)),
        out_shape=jax.ShapeDtypeStruct((s, D_MODEL), F32),
        scratch_shapes=[pltpu.VMEM((TOP_K, tc * ROW_TILE, LANES), F32),
                        pltpu.SemaphoreType.DMA(())],
        compiler_params=_params(("arbitrary",)),
    )(dest_flat, yb_rows, gw, x1_2d, lg, lb)


def _permute_w_in(w_in):
    sizes = (256, 256, 256, 4, 128, 128, 256, 16, 256, 256, 256, 256, 3 * D_MODEL)
    offs = [0]
    for n in sizes:
        offs.append(offs[-1] + n)
    fq, fk, fv, ff, gq, gk, gv, ga, gr, mq, mk, mv, gates = (
        w_in[:, offs[i]:offs[i + 1]] for i in range(len(sizes)))
    main = jnp.concatenate([gates, fq, fk, fv, mq, mk, mv, gq, gk, gv, gr], axis=1).astype(BF16)
    pad = jnp.zeros((D_MODEL, LANES - 4 - GLA_RANK), w_in.dtype)
    tail = jnp.concatenate([ff, ga, pad], axis=1).astype(BF16)
    return main, tail


def _pad_row(v, n=LANES, fill=0.0):
    return jnp.pad(v.astype(F32), (0, n - v.shape[0]), constant_values=fill)[None, :]


def _mixer_layer(x, w_in, fox_fb, gla_a_up, gla_a_b, gla_norm_g, w_branch, w_out, ln_g, ln_b,
                 w_router, b_router):
    w_main, w_tail = _permute_w_in(w_in)
    p, tail = _inproj(x, w_main, w_tail)

    fox_o = _flash(*_fox_prep(p, tail, _pad_row(fox_fb)))
    moba_o = _flash(*_moba_prep(p))
    aup_pad = jnp.zeros((LANES, GLA_KW), F32).at[TAIL_GA:TAIL_GA + GLA_RANK].set(gla_a_up)
    y_gla = _gla(p, tail, aup_pad, gla_a_b[None, :], gla_norm_g[None, :])

    wr = jnp.pad(w_router, ((0, 0), (0, LANES - N_EXPERTS)))
    wr_hi = wr.astype(BF16)
    wr_lo = (wr - wr_hi.astype(F32)).astype(BF16)
    return _merge(fox_o, y_gla, moba_o, p, x, w_branch.astype(BF16), w_out.astype(BF16),
                  ln_g[None, :], ln_b[None, :], wr_hi, wr_lo, _pad_row(b_router))


def _moe_layer(x1_2d, sel, idxf, gw, w_gu, b_gu, w_dn, b_dn, ln_g, ln_b):
    s = sel.shape[0]
    blk = EXPERT_BLOCK
    rank, counts = _rank(sel, idxf)

    counts = counts[0, :N_EXPERTS].astype(jnp.int32)
    padded = (counts + blk - 1) // blk * blk
    pad_end = jnp.cumsum(padded)
    pad_start = pad_end - padded
    top_i = idxf[:, :TOP_K].astype(jnp.int32)
    dest = (pad_start[top_i] + rank[:, :TOP_K].astype(jnp.int32)).reshape(-1)
    n_blocks = (s * TOP_K + N_EXPERTS * (blk - 1)) // blk + 1
    block_start = jnp.arange(n_blocks, dtype=jnp.int32) * blk
    block_expert = jnp.clip(jnp.searchsorted(pad_end, block_start, side='right'),
                            0, N_EXPERTS - 1).astype(jnp.int32)
    block_valid = jnp.clip(pad_start[block_expert] + counts[block_expert] - block_start, 0, blk)
    block_valid = jnp.where(block_start < pad_end[-1], block_valid, 0).astype(jnp.int32)
    last_active = jnp.maximum(pad_end[-1] // blk - 1, 0)
    block_expert = jnp.where(block_start < pad_end[-1], block_expert, block_expert[last_active])

    xin = _dispatch(dest, x1_2d, n_blocks * blk)
    yb = _experts(block_expert, block_valid, xin.reshape(n_blocks * blk * ROW_TILE, LANES),
                  w_gu, b_gu, w_dn, b_dn)
    return _combine(dest, yb.reshape(n_blocks * blk, ROW_TILE, LANES), gw, x1_2d,
                    ln_g[None, :], ln_b[None, :])


def kernel(x, w_in, fox_fb, gla_a_up, gla_a_b, gla_norm_g, w_branch, w_out, ln1_g, ln1_b,
           w_router, b_router, w_gu, b_gu, w_dn, b_dn, ln2_g, ln2_b):
    b, s, d = x.shape
    assert b == 1 and d == D_MODEL
    h = x.reshape(s, d)
    for l in range(DEPTH):
        x1_2d, sel, idxf, gw = _mixer_layer(
            h, w_in[l], fox_fb[l], gla_a_up[l], gla_a_b[l], gla_norm_g[l], w_branch[l], w_out[l],
            ln1_g[l], ln1_b[l], w_router[l], b_router[l])
        h = _moe_layer(x1_2d, sel, idxf, gw, w_gu[l], b_gu[l], w_dn[l], b_dn[l],
                       ln2_g[l], ln2_b[l])
    return h.reshape(b, s, d)
```

```python
import functools

import jax
import jax.numpy as jnp
from jax import lax
from jax.experimental import pallas as pl
from jax.experimental.pallas import tpu as pltpu

F32 = jnp.float32
BF16 = jnp.bfloat16

D_MODEL = 1024
DEPTH = 4
N_HEADS = 4
HEAD_DIM = 64
GLA_DK = 32
GLA_KW = N_HEADS * GLA_DK
GLA_RANK = 16
GLA_TAU = 16.0
MOBA_BLOCK = 256
MOBA_TOPK = 3
BRANCH_WIDTH = 256
N_EXPERTS = 32
TOP_K = 4
SWIGLU_LIMIT = 7.0
SWIGLU_ALPHA = 1.702
ALPHA = (2 * DEPTH) ** 0.25
LN_EPS = 1e-5
RMS_EPS = 1e-6
LOG2E = 1.4426950408889634

LANES = 128
ROW_TILE = 8
NEG = -1e30
UNSELECTED = -32768.0
VMEM_LIMIT = 48 * 1024 * 1024

P_GATES = 0
P_FOX = 3 * D_MODEL
P_MOBA = P_FOX + 3 * BRANCH_WIDTH
P_GLA = P_MOBA + 3 * BRANCH_WIDTH
P_WIDTH = P_GLA + 2 * GLA_KW + 2 * BRANCH_WIDTH
TAIL_FF = 0
TAIL_GA = 4

EXPERT_BLOCK = 256


def _params(sem):
    return pltpu.CompilerParams(dimension_semantics=sem, vmem_limit_bytes=VMEM_LIMIT)


def _split3(x):
    hi = x.astype(BF16)
    r1 = x - hi.astype(F32)
    mid = r1.astype(BF16)
    lo = (r1 - mid.astype(F32)).astype(BF16)
    return hi, mid, lo


def _split2(x):
    hi = x.astype(BF16)
    lo = (x - hi.astype(F32)).astype(BF16)
    return hi, lo


def _dot(a, b):
    return jnp.dot(a, b, preferred_element_type=F32)


def _dot_nt(a, b):
    return lax.dot_general(a, b, (((1,), (1,)), ((), ())), preferred_element_type=F32)


def _dot_tn(a, b):
    return lax.dot_general(a, b, (((0,), (0,)), ((), ())), preferred_element_type=F32)


def _log_sigmoid(t):
    return jnp.minimum(t, 0.0) - jnp.log1p(jnp.exp(-jnp.abs(t)))


def _sigmoid(t):
    return 1.0 / (1.0 + jnp.exp(-t))


def _iota(shape, axis):
    return lax.broadcasted_iota(jnp.int32, shape, axis)


def _transpose_bf16(x):
    return x.astype(F32).T.astype(BF16)


def _aug_specs(s, t):
    aug = jax.ShapeDtypeStruct((N_HEADS, s, LANES), BF16)
    aug_spec = pl.BlockSpec((N_HEADS, t, LANES), lambda i: (0, i, 0))
    aug_t = jax.ShapeDtypeStruct((N_HEADS, LANES, s), BF16)
    aug_t_spec = pl.BlockSpec((N_HEADS, LANES, t), lambda i: (0, 0, i))
    return aug, aug_spec, aug_t, aug_t_spec


def _inproj_kernel(x_ref, w_ref, wt_ref, sc_ref, p_ref, t_ref, xb_ref):
    @pl.when(pl.program_id(1) == 0)
    def _():
        xb_ref[...] = x_ref[...].astype(BF16)
        t_ref[...] = _dot(xb_ref[...], wt_ref[...])

    p_ref[...] = (_dot(xb_ref[...], w_ref[...]) * sc_ref[...]).astype(BF16)


def _inproj(x, w, wt, tm=1024, tn=768):
    s = x.shape[0]
    col = jnp.arange(P_WIDTH)
    is_q = ((col >= P_FOX) & (col < P_FOX + BRANCH_WIDTH)) | ((col >= P_MOBA) & (col < P_MOBA + BRANCH_WIDTH))
    scale = jnp.where(is_q, HEAD_DIM ** -0.5 * LOG2E, 1.0).astype(F32)[None, :]
    return pl.pallas_call(
        _inproj_kernel,
        grid=(s // tm, P_WIDTH // tn),
        in_specs=[pl.BlockSpec((tm, D_MODEL), lambda i, j: (i, 0)),
                  pl.BlockSpec((D_MODEL, tn), lambda i, j: (0, j)),
                  pl.BlockSpec((D_MODEL, LANES), lambda i, j: (0, 0)),
                  pl.BlockSpec((1, tn), lambda i, j: (0, j))],
        out_specs=[pl.BlockSpec((tm, tn), lambda i, j: (i, j)),
                   pl.BlockSpec((tm, LANES), lambda i, j: (i, 0))],
        out_shape=[jax.ShapeDtypeStruct((s, P_WIDTH), BF16),
                   jax.ShapeDtypeStruct((s, LANES), F32)],
        scratch_shapes=[pltpu.VMEM((tm, D_MODEL), BF16)],
        compiler_params=_params(("parallel", "arbitrary")),
    )(x, w, wt, scale)


def _fox_prep_kernel(q_ref, k_ref, v_ref, t_ref, fb_ref, qa_ref, ka_ref, va_ref, st_ref, carry_ref):
    t = q_ref.shape[0]

    @pl.when(pl.program_id(0) == 0)
    def _():
        carry_ref[...] = jnp.zeros_like(carry_ref)

    ls = _log_sigmoid(t_ref[...] + fb_ref[...])
    tri = (_iota((t, t), 0) >= _iota((t, t), 1)).astype(BF16)
    hi, mid, lo = _split3(ls)
    c = _dot(tri, hi) + _dot(tri, mid) + _dot(tri, lo) + carry_ref[...]
    carry_ref[...] = c[t - 1:t, :]

    c2 = LOG2E * c

    lane = _iota((t, HEAD_DIM), 1)
    q_aug = jnp.where(lane < 3, 1.0, 0.0).astype(BF16)
    v_aug = jnp.where(lane == 0, 1.0, 0.0).astype(BF16)
    q = q_ref[...]
    k = k_ref[...]
    v = v_ref[...]
    srow = _iota((ROW_TILE, LANES), 0)
    slane = _iota((ROW_TILE, LANES), 1)
    stat = jnp.where(srow == 2, c2[0:1, :], jnp.where(srow == 3, c2[t - 1:t, :], 0.0))
    for h in range(N_HEADS):
        sl = slice(h * HEAD_DIM, (h + 1) * HEAD_DIM)
        nhi, nmid, nlo = (piece.astype(F32) for piece in _split3(-c2[:, h:h + 1]))
        k_aug = jnp.where(lane == 0, nhi, jnp.where(lane == 1, nmid, jnp.where(lane == 2, nlo, 0.0)))
        qa_ref[h] = _transpose_bf16(jnp.concatenate([q[:, sl], q_aug], axis=1))
        ka_ref[h] = jnp.concatenate([k[:, sl], k_aug.astype(BF16)], axis=1)
        va_ref[h] = _transpose_bf16(jnp.concatenate([v[:, sl], v_aug], axis=1))
        for r, x in ((0, q), (1, k)):
            xf = x[:, sl].astype(F32)
            norm = jnp.sqrt(jnp.max(jnp.sum(xf * xf, axis=1, keepdims=True), axis=0, keepdims=True))
            stat = jnp.where(jnp.logical_and(srow == r, slane == h), norm, stat)
    st_ref[0] = stat


def _fox_prep(p, tail, fb_row, t=512):
    s = p.shape[0]
    cb = P_FOX // BRANCH_WIDTH
    aug, aug_spec, aug_t, aug_t_spec = _aug_specs(s, t)
    return pl.pallas_call(
        _fox_prep_kernel,
        grid=(s // t,),
        in_specs=[pl.BlockSpec((t, BRANCH_WIDTH), lambda i: (i, cb)),
                  pl.BlockSpec((t, BRANCH_WIDTH), lambda i: (i, cb + 1)),
                  pl.BlockSpec((t, BRANCH_WIDTH), lambda i: (i, cb + 2)),
                  pl.BlockSpec((t, LANES), lambda i: (i, 0)),
                  pl.BlockSpec((1, LANES), lambda i: (0, 0))],
        out_specs=[aug_t_spec, aug_spec, aug_t_spec,
                   pl.BlockSpec((1, ROW_TILE, LANES), lambda i: (i, 0, 0))],
        out_shape=[aug_t, aug, aug_t, jax.ShapeDtypeStruct((s // t, ROW_TILE, LANES), F32)],
        scratch_shapes=[pltpu.VMEM((1, LANES), F32)],
        compiler_params=_params(("arbitrary",)),
    )(p, p, p, tail, fb_row)


SKIP_LOG2 = 80.0


def _fox_first_tile(stats, s, tq, tk):
    tp = s // stats.shape[0]
    qn, kn, c_first, c_last = (stats[:, r, :N_HEADS] for r in range(4))
    nq, nk = s // tq, s // tk
    qn = jnp.max(qn.reshape(nq, tq // tp, N_HEADS), axis=1)
    c_q = c_first.reshape(nq, tq // tp, N_HEADS)[:, 0, :]
    c_k = c_last.reshape(nk, tk // tp, N_HEADS)[:, -1, :]
    bound = (2.0 * qn * jnp.max(kn, axis=0))[:, None, :] + c_q[:, None, :] - c_k[None, :, :]
    j = jnp.arange(nk, dtype=jnp.int32)[None, :, None]
    first = jnp.min(jnp.where(bound >= -SKIP_LOG2, j, nk), axis=1)
    return first.T.reshape(-1).astype(jnp.int32)


def _moba_prep_kernel(q_ref, k_ref, v_ref, qa_ref, ka_ref, va_ref, kmean_ref):
    i = pl.program_id(0)
    t = q_ref.shape[0]
    nbl = kmean_ref.shape[0]

    @pl.when(i == 0)
    def _():
        kmean_ref[...] = jnp.zeros_like(kmean_ref)

    q = q_ref[...]
    k = k_ref[...]
    v = v_ref[...]
    km = kmean_ref[...]
    lane = _iota((t, HEAD_DIM), 1)
    v_aug = jnp.where(lane == 0, 1.0, 0.0).astype(BF16)
    k_aug = jnp.where(lane == i, 1.0, 0.0).astype(BF16)
    for h in range(N_HEADS):
        sl = slice(h * HEAD_DIM, (h + 1) * HEAD_DIM)
        kh_hi, kh_lo = _split2(km[:HEAD_DIM, sl])
        gate = _dot_nt(q[:, sl], kh_hi) + _dot_nt(q[:, sl], kh_lo)
        g = jnp.where(lane < i, gate, NEG)
        bias = jnp.where(lane == i, 0.0, UNSELECTED)
        for _ in range(MOBA_TOPK):
            mx = jnp.max(g, axis=1, keepdims=True)
            idx = jnp.min(jnp.where(g == mx, lane, HEAD_DIM), axis=1, keepdims=True)
            pick = jnp.logical_and(lane == idx, mx > 0.5 * NEG)
            bias = jnp.where(pick, 0.0, bias)
            g = jnp.where(lane == idx, NEG, g)
        qa_ref[h] = _transpose_bf16(jnp.concatenate([q[:, sl], bias.astype(BF16)], axis=1))
        ka_ref[h] = jnp.concatenate([k[:, sl], k_aug], axis=1)
        va_ref[h] = _transpose_bf16(jnp.concatenate([v[:, sl], v_aug], axis=1))
    kmean = jnp.mean(k.astype(F32), axis=0, keepdims=True)
    kmean_ref[...] = jnp.where(_iota((nbl, BRANCH_WIDTH), 0) == i, kmean, km)


def _moba_prep(p):
    s = p.shape[0]
    t = MOBA_BLOCK
    assert s % t == 0 and s // t <= HEAD_DIM, "block one-hot must fit the 64 augmentation lanes"
    cb = P_MOBA // BRANCH_WIDTH
    aug, aug_spec, aug_t, aug_t_spec = _aug_specs(s, t)
    return pl.pallas_call(
        _moba_prep_kernel,
        grid=(s // t,),
        in_specs=[pl.BlockSpec((t, BRANCH_WIDTH), lambda i: (i, cb)),
                  pl.BlockSpec((t, BRANCH_WIDTH), lambda i: (i, cb + 1)),
                  pl.BlockSpec((t, BRANCH_WIDTH), lambda i: (i, cb + 2))],
        out_specs=[aug_t_spec, aug_spec, aug_t_spec],
        out_shape=[aug_t, aug, aug_t],
        scratch_shapes=[pltpu.VMEM((HEAD_DIM, BRANCH_WIDTH), F32)],
        compiler_params=_params(("arbitrary",)),
    )(p, p, p)


def _flash_kernel(first_ref, qt_ref, k_ref, vt_ref, o_ref, s0_ref, s1_ref, *, tq, tk):
    h = pl.program_id(0)
    i = pl.program_id(1)
    qt = qt_ref[0]
    n_diag = tq // tk
    n_full = i * n_diag

    def scores(j):
        return _dot(k_ref[0, pl.ds(pl.multiple_of(j * tk, tk), tk), :], qt)

    def absorb(j, s, m, acc, masked):
        if masked:
            key = _iota((tk, tq), 0) + j * tk
            qry = _iota((tk, tq), 1) + i * tq
            s = jnp.where(key <= qry, s, NEG)
        m_new = jnp.maximum(m, jnp.max(s, axis=0, keepdims=True))
        p = jnp.exp2(s - m_new).astype(BF16)
        vt = vt_ref[0, :, pl.ds(pl.multiple_of(j * tk, tk), tk)]
        return m_new, jnp.exp2(m - m_new) * acc + _dot(vt, p)

    j0 = jnp.minimum(first_ref[h * pl.num_programs(1) + i], n_full)
    m = jnp.full((1, tq), NEG, F32)
    acc = jnp.zeros((LANES, tq), F32)

    odd = (n_full - j0) % 2

    def plain(j, carry):
        return absorb(j, scores(j), *carry, False)

    m, acc = lax.fori_loop(j0, j0 + odd, plain, (m, acc))
    j0 = j0 + odd

    s0_ref[...] = scores(j0)

    def pair(g, carry):
        j = j0 + 2 * g
        s1_ref[...] = scores(j + 1)
        carry = absorb(j, s0_ref[...], *carry, False)
        s0_ref[...] = scores(j + 2)
        return absorb(j + 1, s1_ref[...], *carry, False)

    m, acc = lax.fori_loop(0, (n_full - j0) // 2, pair, (m, acc))
    s = s0_ref[...]
    for d in range(n_diag):
        s_next = scores(n_full + d + 1) if d + 1 < n_diag else None
        m, acc = absorb(n_full + d, s, m, acc, True)
        s = s_next
    o_ref[0] = (acc / acc[HEAD_DIM:HEAD_DIM + 1, :]).T.astype(o_ref.dtype)


def _flash(first_tile, qt, ka, vt, tq, tk):
    nh, s, _ = ka.shape
    return pl.pallas_call(
        functools.partial(_flash_kernel, tq=tq, tk=tk),
        grid_spec=pltpu.PrefetchScalarGridSpec(
            num_scalar_prefetch=1,
            grid=(nh, s // tq),
            in_specs=[pl.BlockSpec((1, LANES, tq), lambda h, i, f: (h, 0, i)),
                      pl.BlockSpec((1, s, LANES), lambda h, i, f: (h, 0, 0)),
                      pl.BlockSpec((1, LANES, s), lambda h, i, f: (h, 0, 0))],
            out_specs=pl.BlockSpec((1, tq, LANES), lambda h, i, f: (h, i, 0)),
            scratch_shapes=[pltpu.VMEM((tk, tq), F32), pltpu.VMEM((tk, tq), F32)]),
        out_shape=jax.ShapeDtypeStruct((nh, s, LANES), BF16),
        compiler_params=_params(("parallel", "parallel")),
    )(first_tile, qt, ka, vt)


def _flash_tiles(s):
    tq = min(1024, s)
    return tq, min(512, tq)


GLA_SUB = 16


def _gla_kernel(q_ref, k_ref, v_ref, r_ref, t_ref, aup_ref, ab_ref, g_ref, y_ref,
                st_ref, b_ref, o_ref):
    t = q_ref.shape[0]
    c = GLA_SUB

    @pl.when(pl.program_id(0) == 0)
    def _():
        st_ref[...] = jnp.zeros_like(st_ref)

    t_hi, t_lo = _split2(t_ref[...])
    a_hi, a_lo = _split2(aup_ref[...])
    z = _dot(t_hi, a_hi) + _dot(t_lo, a_hi) + _dot(t_hi, a_lo) + ab_ref[...]
    log_a = _log_sigmoid(z) * (1.0 / GLA_TAU)
    row = _iota((t, t), 0)
    col = _iota((t, t), 1)
    tri = jnp.logical_and(row // c == col // c, row >= col).astype(BF16)
    hi, mid, lo = _split3(log_a)
    b_ref[...] = _dot(tri, hi) + _dot(tri, mid) + _dot(tri, lo)

    expand = (_iota((GLA_KW, BRANCH_WIDTH), 0) // GLA_DK
              == _iota((GLA_KW, BRANCH_WIDTH), 1) // HEAD_DIM).astype(BF16)
    st_mask = (_iota((BRANCH_WIDTH, GLA_KW), 0) // HEAD_DIM
               == _iota((BRANCH_WIDTH, GLA_KW), 1) // GLA_DK).astype(F32)
    srow = _iota((c, GLA_KW), 0)
    orow = _iota((c, BRANCH_WIDTH), 0)
    scale = GLA_DK ** -0.5

    def step(r, _):
        r0 = pl.multiple_of(r * c, c)
        qs = q_ref[pl.ds(r0, c), :].astype(F32) * scale
        ks = k_ref[pl.ds(r0, c), :].astype(F32)
        vb = v_ref[pl.ds(r0, c), :]
        vf = vb.astype(F32)
        bs = b_ref[pl.ds(r0, c), :]
        b_last = bs[c - 1:c, :]
        st = st_ref[...]

        o_inter = _dot_nt((qs * jnp.exp(bs)).astype(BF16), st.astype(BF16))

        pieces = []
        for tt in range(c):
            e = jnp.exp(jnp.where(srow <= tt, bs[tt:tt + 1, :] - bs, NEG))
            pieces.append(e * qs[tt:tt + 1, :] * ks)
        pm = jnp.concatenate(pieces, axis=0).astype(BF16)
        a = _dot(pm, expand)
        o_intra = jnp.zeros((c, BRANCH_WIDTH), F32)
        for tt in range(c):
            o_t = jnp.sum(a[tt * c:(tt + 1) * c, :] * vf, axis=0, keepdims=True)
            o_intra = jnp.where(orow == tt, o_t, o_intra)
        o_ref[pl.ds(r0, c), :] = o_inter + o_intra

        ke = (ks * jnp.exp(b_last - bs)).astype(BF16)
        st_ref[...] = st * jnp.exp(b_last) + _dot_tn(vb, ke) * st_mask
        return 0

    lax.fori_loop(0, t // c, step, 0)

    o = o_ref[...]
    ones_bd = (_iota((BRANCH_WIDTH, BRANCH_WIDTH), 0) // HEAD_DIM
               == _iota((BRANCH_WIDTH, BRANCH_WIDTH), 1) // HEAD_DIM).astype(BF16)
    sq_hi, sq_mid, sq_lo = _split3(o * o)
    ms = (_dot(sq_hi, ones_bd) + _dot(sq_mid, ones_bd) + _dot(sq_lo, ones_bd)) * (1.0 / HEAD_DIM)
    gr = r_ref[...].astype(F32)
    y = o * lax.rsqrt(ms + RMS_EPS) * g_ref[...] * (gr * _sigmoid(gr))
    y_ref[...] = y.astype(y_ref.dtype)


def _gla(p, tail, aup_pad, ab_row, g_row, t=512):
    s = p.shape[0]
    cq = P_GLA // GLA_KW
    cv = (P_GLA + 2 * GLA_KW) // BRANCH_WIDTH
    return pl.pallas_call(
        _gla_kernel,
        grid=(s // t,),
        in_specs=[pl.BlockSpec((t, GLA_KW), lambda i: (i, cq)),
                  pl.BlockSpec((t, GLA_KW), lambda i: (i, cq + 1)),
                  pl.BlockSpec((t, BRANCH_WIDTH), lambda i: (i, cv)),
                  pl.BlockSpec((t, BRANCH_WIDTH), lambda i: (i, cv + 1)),
                  pl.BlockSpec((t, LANES), lambda i: (i, 0)),
                  pl.BlockSpec((LANES, GLA_KW), lambda i: (0, 0)),
                  pl.BlockSpec((1, GLA_KW), lambda i: (0, 0)),
                  pl.BlockSpec((1, BRANCH_WIDTH), lambda i: (0, 0))],
        out_specs=pl.BlockSpec((t, BRANCH_WIDTH), lambda i: (i, 0)),
        out_shape=jax.ShapeDtypeStruct((s, BRANCH_WIDTH), BF16),
        scratch_shapes=[pltpu.VMEM((BRANCH_WIDTH, GLA_KW), F32),
                        pltpu.VMEM((t, GLA_KW), F32),
                        pltpu.VMEM((t, BRANCH_WIDTH), F32)],
        compiler_params=_params(("arbitrary",)),
    )(p, p, p, p, tail, aup_pad, ab_row, g_row)


def _layer_norm(z, g, b):
    mu = jnp.mean(z, axis=1, keepdims=True)
    zc = z - mu
    var = jnp.mean(zc * zc, axis=1, keepdims=True)
    return zc * lax.rsqrt(var + LN_EPS) * g + b


def _store_row_tiles(ref, val):
    n = val.shape[0]
    for a in range(ROW_TILE):
        ref[pl.ds(a, n, stride=ROW_TILE), :] = val[:, a * LANES:(a + 1) * LANES]


def _load_row_tiles(ref, n, base=0):
    return jnp.concatenate(
        [ref[pl.ds(base + a, n, stride=ROW_TILE), :] for a in range(ROW_TILE)], axis=1)


def _merge_kernel(fox_ref, gla_ref, moba_ref, g0_ref, g1_ref, g2_ref, x_ref, wb_ref, wo_ref,
                  lg_ref, lb_ref, wrh_ref, wrl_ref, br_ref,
                  x1_ref, sel_ref, idx_ref, gw_ref):
    tm = x_ref.shape[0]

    def heads(ref):
        return jnp.concatenate([ref[h][:, :HEAD_DIM] for h in range(N_HEADS)], axis=1)

    merged = _sigmoid(g0_ref[...].astype(F32)) * _dot(heads(fox_ref), wb_ref[0])
    merged += _sigmoid(g1_ref[...].astype(F32)) * _dot(gla_ref[...], wb_ref[1])
    merged += _sigmoid(g2_ref[...].astype(F32)) * _dot(heads(moba_ref), wb_ref[2])
    z = ALPHA * x_ref[...] + _dot(merged.astype(BF16), wo_ref[...])
    x1 = _layer_norm(z, lg_ref[...], lb_ref[...])
    _store_row_tiles(x1_ref, x1)

    x_hi, x_lo = _split2(x1)
    logits = (_dot(x_hi, wrh_ref[...]) + _dot(x_lo, wrh_ref[...]) + _dot(x_hi, wrl_ref[...])
              + br_ref[...])
    lane = _iota((tm, LANES), 1)
    lg = jnp.where(lane < N_EXPERTS, logits, NEG)
    sel = jnp.zeros((tm, LANES), F32)
    idxf = jnp.zeros((tm, LANES), F32)
    ew = jnp.zeros((tm, LANES), F32)
    top = None
    for r in range(TOP_K):
        mx = jnp.max(lg, axis=1, keepdims=True)
        idx = jnp.min(jnp.where(lg == mx, lane, LANES), axis=1, keepdims=True)
        hit = lane == idx
        top = mx if top is None else top
        sel = jnp.where(hit, 1.0, sel)
        idxf = jnp.where(lane == r, idx.astype(F32), idxf)
        ew = jnp.where(lane == r, jnp.exp(mx - top), ew)
        lg = jnp.where(hit, NEG, lg)
    sel_ref[...] = sel
    idx_ref[...] = idxf
    gw_ref[...] = ew / jnp.sum(ew, axis=1, keepdims=True)


def _merge(fox_o, y_gla, moba_o, p, x, wb, wo, lg, lb, wrh, wrl, br, tm=256):
    s = x.shape[0]
    head_spec = pl.BlockSpec((N_HEADS, tm, LANES), lambda i: (0, i, 0))
    row = lambda n: pl.BlockSpec((1, n), lambda i: (0, 0))
    small = jax.ShapeDtypeStruct((s, LANES), F32)
    small_spec = pl.BlockSpec((tm, LANES), lambda i: (i, 0))
    return pl.pallas_call(
        _merge_kernel,
        grid=(s // tm,),
        in_specs=[head_spec,
                  pl.BlockSpec((tm, BRANCH_WIDTH), lambda i: (i, 0)),
                  head_spec,
                  pl.BlockSpec((tm, D_MODEL), lambda i: (i, 0)),
                  pl.BlockSpec((tm, D_MODEL), lambda i: (i, 1)),
                  pl.BlockSpec((tm, D_MODEL), lambda i: (i, 2)),
                  pl.BlockSpec((tm, D_MODEL), lambda i: (i, 0)),
                  pl.BlockSpec((3, BRANCH_WIDTH, D_MODEL), lambda i: (0, 0, 0)),
                  pl.BlockSpec((D_MODEL, D_MODEL), lambda i: (0, 0)),
                  row(D_MODEL), row(D_MODEL),
                  pl.BlockSpec((D_MODEL, LANES), lambda i: (0, 0)),
                  pl.BlockSpec((D_MODEL, LANES), lambda i: (0, 0)),
                  row(LANES)],
        out_specs=[pl.BlockSpec((tm * ROW_TILE, LANES), lambda i: (i, 0)),
                   small_spec, small_spec, small_spec],
        out_shape=[jax.ShapeDtypeStruct((s * ROW_TILE, LANES), F32), small, small, small],
        compiler_params=_params(("parallel",)),
    )(fox_o, y_gla, moba_o, p, p, p, x, wb, wo, lg, lb, wrh, wrl, br)


def _rank_kernel(sel_ref, idx_ref, dest_ref, cnt_ref, carry_ref, start_ref):
    phase = pl.program_id(0)
    i = pl.program_id(1)
    t = sel_ref.shape[0]
    sel = sel_ref[...]
    chosen = jnp.sum(sel, axis=0, keepdims=True)

    @pl.when(jnp.logical_and(phase == 0, i == 0))
    def _():
        carry_ref[...] = jnp.zeros_like(carry_ref)

    @pl.when(jnp.logical_and(phase == 1, i == 0))
    def _():
        counts = carry_ref[...]
        cnt_ref[...] = counts
        blocks = jnp.floor((counts + (EXPERT_BLOCK - 1)) * (1.0 / EXPERT_BLOCK))
        before = (_iota((LANES, LANES), 0) < _iota((LANES, LANES), 1)).astype(BF16)
        first = _dot(jnp.broadcast_to(blocks, (ROW_TILE, LANES)).astype(BF16), before)
        start_ref[...] = first[0:1, :] * float(EXPERT_BLOCK)
        carry_ref[...] = jnp.zeros_like(carry_ref)

    @pl.when(phase == 1)
    def _():
        stri = (_iota((t, t), 0) > _iota((t, t), 1)).astype(BF16)
        pos = _dot(stri, sel.astype(BF16)) + carry_ref[...] + start_ref[...]
        lane = _iota((t, LANES), 1).astype(F32)
        idxf = idx_ref[...]
        dest = jnp.zeros((t, LANES), F32)
        for r in range(TOP_K):
            mine = jnp.sum(jnp.where(lane == idxf[:, r:r + 1], pos, 0.0), axis=1, keepdims=True)
            dest = jnp.where(lane == float(r), mine, dest)
        dest_ref[...] = dest.T[:ROW_TILE, :].astype(jnp.int32)

    carry_ref[...] += chosen


def _rank(sel, idxf, t=512):
    s = sel.shape[0]
    spec = pl.BlockSpec((t, LANES), lambda ph, i: (i, 0))
    return pl.pallas_call(
        _rank_kernel,
        grid=(2, s // t),
        in_specs=[spec, spec],
        out_specs=[pl.BlockSpec((ROW_TILE, t), lambda ph, i: (0, i * ph)),
                   pl.BlockSpec((1, LANES), lambda ph, i: (0, 0))],
        out_shape=[jax.ShapeDtypeStruct((ROW_TILE, s), jnp.int32),
                   jax.ShapeDtypeStruct((1, LANES), F32)],
        scratch_shapes=[pltpu.VMEM((1, LANES), F32), pltpu.VMEM((1, LANES), F32)],
        compiler_params=_params(("arbitrary", "arbitrary")),
    )(sel, idxf)


DMA_UNROLL = 4


def _dispatch_kernel(dest_ref, x_ref, xin_ref, sem):
    n = dest_ref.shape[1]

    def issue(g, _):
        for u in range(DMA_UNROLL):
            t = g * DMA_UNROLL + u
            for k in range(TOP_K):
                pltpu.make_async_copy(x_ref.at[t], xin_ref.at[dest_ref[k, t]], sem).start()
        return 0

    lax.fori_loop(0, n // DMA_UNROLL, issue, 0)
    for k in range(TOP_K):
        pltpu.make_async_copy(x_ref, xin_ref.at[pl.ds(0, n)], sem).wait()


def _dispatch(dest, x1_rows, n_rows, td=512):
    s = x1_rows.shape[0]
    return pl.pallas_call(
        _dispatch_kernel,
        grid=(s // td,),
        in_specs=[pl.BlockSpec((ROW_TILE, td), lambda i: (0, i), memory_space=pltpu.SMEM),
                  pl.BlockSpec((td, ROW_TILE, LANES), lambda i: (i, 0, 0))],
        out_specs=pl.BlockSpec(memory_space=pl.ANY),
        out_shape=jax.ShapeDtypeStruct((n_rows, ROW_TILE, LANES), F32),
        scratch_shapes=[pltpu.SemaphoreType.DMA(())],
        compiler_params=pltpu.CompilerParams(dimension_semantics=("arbitrary",),
                                             has_side_effects=True),
    )(dest, x1_rows)


HALF = LANES // 2
W_CAST_ROWS = 64


def _expert_kernel(be_ref, nv_ref, xin_ref, wgu_ref, bgu_ref, wdn_ref, bdn_ref, y_ref,
                   wgu_s, wdn_s, perm_s):
    b = pl.program_id(0)
    blk = y_ref.shape[0] // ROW_TILE
    nvalid = nv_ref[b]
    new_expert = jnp.logical_or(b == 0, be_ref[b] != be_ref[jnp.maximum(b - 1, 0)])

    @pl.when(jnp.logical_and(new_expert, nvalid > 0))
    def _():
        def cast(r, _):
            r0 = pl.multiple_of(r * W_CAST_ROWS, W_CAST_ROWS)
            wgu_s[pl.ds(r0, W_CAST_ROWS), :] = wgu_ref[0, pl.ds(r0, W_CAST_ROWS), :].astype(BF16)
            return 0

        lax.fori_loop(0, D_MODEL // W_CAST_ROWS, cast, 0)
        for c in range(D_MODEL // LANES):
            cols = slice(c * LANES, (c + 1) * LANES)
            for g in range(D_MODEL // LANES):
                lo = g * LANES
                perm_s[c, pl.ds(lo, HALF, stride=2), :] = wdn_ref[0, lo:lo + HALF, cols]
                perm_s[c, pl.ds(lo + 1, HALF, stride=2), :] = wdn_ref[0, lo + HALF:lo + LANES, cols]
            wdn_s[:, cols] = perm_s[c].astype(BF16)

    @pl.when(nvalid > 0)
    def _():
        x = _load_row_tiles(xin_ref, blk)
        x = jnp.where(_iota((blk, D_MODEL), 0) < nvalid, x, 0.0).astype(BF16)
        even = _iota((blk, LANES), 1) % 2 == 0
        acts = []
        for g in range(D_MODEL // LANES):
            lo = g * 2 * LANES
            h = _dot(x, wgu_s[:, lo:lo + 2 * LANES]) + bgu_ref[0, :, lo:lo + 2 * LANES]
            h_a = h[:, :LANES]
            h_b = h[:, LANES:]
            gate = jnp.where(even, h_a, pltpu.roll(h_b, 1, 1))
            up = jnp.where(even, pltpu.roll(h_a, LANES - 1, 1), h_b)
            gate = jnp.minimum(gate, SWIGLU_LIMIT)
            up = jnp.clip(up, -SWIGLU_LIMIT, SWIGLU_LIMIT)
            acts.append(((up + 1.0) * (gate * _sigmoid(SWIGLU_ALPHA * gate))).astype(BF16))
        act = jnp.concatenate(acts, axis=1)
        _store_row_tiles(y_ref, _dot(act, wdn_s[...]) + bdn_ref[0])

    @pl.when(nvalid <= 0)
    def _():
        y_ref[...] = jnp.zeros_like(y_ref)


def _experts(block_expert, block_valid, xin2d, w_gu, b_gu, w_dn, b_dn, layer):
    blk = EXPERT_BLOCK
    nb = block_expert.shape[0]
    x_spec = pl.BlockSpec((blk * ROW_TILE, LANES), lambda b, be, nv: (b, 0))
    which = lambda b, be, nv: (layer * N_EXPERTS + be[b], 0, 0)
    return pl.pallas_call(
        _expert_kernel,
        grid_spec=pltpu.PrefetchScalarGridSpec(
            num_scalar_prefetch=2,
            grid=(nb,),
            in_specs=[x_spec,
                      pl.BlockSpec((1, D_MODEL, 2 * D_MODEL), which),
                      pl.BlockSpec((1, 1, 2 * D_MODEL), which),
                      pl.BlockSpec((1, D_MODEL, D_MODEL), which),
                      pl.BlockSpec((1, 1, D_MODEL), which)],
            out_specs=x_spec,
            scratch_shapes=[pltpu.VMEM((D_MODEL, 2 * D_MODEL), BF16),
                            pltpu.VMEM((D_MODEL, D_MODEL), BF16),
                            pltpu.VMEM((D_MODEL // LANES, D_MODEL, LANES), F32)]),
        out_shape=jax.ShapeDtypeStruct((nb * blk * ROW_TILE, LANES), F32),
        compiler_params=pltpu.CompilerParams(dimension_semantics=("arbitrary",),
                                             vmem_limit_bytes=56 * 1024 * 1024),
    )(block_expert, block_valid, xin2d, w_gu, b_gu, w_dn, b_dn)


def _combine_kernel(dest_ref, next_ref, yb_ref, gw_ref, x1_ref, lg_ref, lb_ref, out_ref,
                    buf_ref, sem):
    i = pl.program_id(0)
    tc = out_ref.shape[0]
    slot = i % 2

    def gather(rows_ref, into):
        def body(g, _):
            for u in range(DMA_UNROLL):
                t = g * DMA_UNROLL + u
                for k in range(TOP_K):
                    pltpu.make_async_copy(yb_ref.at[rows_ref[k, t]], buf_ref.at[into, k, t],
                                          sem.at[into]).start()
            return 0

        lax.fori_loop(0, tc // DMA_UNROLL, body, 0)

    @pl.when(i == 0)
    def _():
        gather(dest_ref, 0)

    @pl.when(i + 1 < pl.num_programs(0))
    def _():
        gather(next_ref, 1 - slot)

    for k in range(TOP_K):
        pltpu.make_async_copy(yb_ref.at[pl.ds(0, tc)], buf_ref.at[slot, k], sem.at[slot]).wait()

    gw = gw_ref[...]
    f = jnp.zeros((tc, D_MODEL), F32)
    for k in range(TOP_K):
        rows = jnp.concatenate([buf_ref[slot, k, :, a, :] for a in range(ROW_TILE)], axis=1)
        f += gw[:, k:k + 1] * rows
    z = ALPHA * _load_row_tiles(x1_ref, tc) + f
    out_ref[...] = _layer_norm(z, lg_ref[...], lb_ref[...])


def _combine(dest, yb_rows, gw, x1_2d, lg, lb, tc=256):
    s = gw.shape[0]
    n = s // tc
    row = pl.BlockSpec((1, D_MODEL), lambda i: (0, 0))
    return pl.pallas_call(
        _combine_kernel,
        grid=(n,),
        in_specs=[pl.BlockSpec((ROW_TILE, tc), lambda i: (0, i), memory_space=pltpu.SMEM),
                  pl.BlockSpec((ROW_TILE, tc), lambda i: (0, jnp.minimum(i + 1, n - 1)),
                               memory_space=pltpu.SMEM),
                  pl.BlockSpec(memory_space=pl.ANY),
                  pl.BlockSpec((tc, LANES), lambda i: (i, 0)),
                  pl.BlockSpec((tc * ROW_TILE, LANES), lambda i: (i, 0)),
                  row, row],
        out_specs=pl.BlockSpec((tc, D_MODEL), lambda i: (i, 0)),
        out_shape=jax.ShapeDtypeStruct((s, D_MODEL), F32),
        scratch_shapes=[pltpu.VMEM((2, TOP_K, tc, ROW_TILE, LANES), F32),
                        pltpu.SemaphoreType.DMA((2,))],
        compiler_params=_params(("arbitrary",)),
    )(dest, dest, yb_rows, gw, x1_2d, lg, lb)


def _permute_w_in(w_in):
    sizes = (256, 256, 256, 4, 128, 128, 256, 16, 256, 256, 256, 256, 3 * D_MODEL)
    offs = [0]
    for n in sizes:
        offs.append(offs[-1] + n)
    fq, fk, fv, ff, gq, gk, gv, ga, gr, mq, mk, mv, gates = (
        w_in[:, offs[i]:offs[i + 1]] for i in range(len(sizes)))
    main = jnp.concatenate([gates, fq, fk, fv, mq, mk, mv, gq, gk, gv, gr], axis=1).astype(BF16)
    pad = jnp.zeros((D_MODEL, LANES - 4 - GLA_RANK), w_in.dtype)
    tail = jnp.concatenate([ff, ga, pad], axis=1).astype(BF16)
    return main, tail


def _pad_row(v, n=LANES, fill=0.0):
    return jnp.pad(v.astype(F32), (0, n - v.shape[0]), constant_values=fill)[None, :]


def _mixer_layer(x, w_in, fox_fb, gla_a_up, gla_a_b, gla_norm_g, w_branch, w_out, ln_g, ln_b,
                 w_router, b_router):
    w_main, w_tail = _permute_w_in(w_in)
    p, tail = _inproj(x, w_main, w_tail)

    s = x.shape[0]
    tq, tk = _flash_tiles(s)
    fox_qt, fox_k, fox_vt, fox_stats = _fox_prep(p, tail, _pad_row(fox_fb))
    fox_o = _flash(_fox_first_tile(fox_stats, s, tq, tk), fox_qt, fox_k, fox_vt, tq, tk)
    moba_o = _flash(jnp.zeros((N_HEADS * (s // tq),), jnp.int32), *_moba_prep(p), tq, tk)
    aup_pad = jnp.zeros((LANES, GLA_KW), F32).at[TAIL_GA:TAIL_GA + GLA_RANK].set(gla_a_up)
    y_gla = _gla(p, tail, aup_pad, gla_a_b[None, :], gla_norm_g[None, :])

    wr = jnp.pad(w_router, ((0, 0), (0, LANES - N_EXPERTS)))
    wr_hi = wr.astype(BF16)
    wr_lo = (wr - wr_hi.astype(F32)).astype(BF16)
    return _merge(fox_o, y_gla, moba_o, p, x, w_branch.astype(BF16), w_out.astype(BF16),
                  ln_g[None, :], ln_b[None, :], wr_hi, wr_lo, _pad_row(b_router))


def _moe_layer(x1_2d, sel, idxf, gw, w_gu, b_gu, w_dn, b_dn, ln_g, ln_b, layer):
    s = sel.shape[0]
    blk = EXPERT_BLOCK
    dest, counts = _rank(sel, idxf)

    counts = counts[0, :N_EXPERTS].astype(jnp.int32)
    padded = (counts + blk - 1) // blk * blk
    pad_end = jnp.cumsum(padded)
    pad_start = pad_end - padded
    n_blocks = (s * TOP_K + N_EXPERTS * (blk - 1)) // blk + 1
    block_start = jnp.arange(n_blocks, dtype=jnp.int32) * blk
    active = block_start < pad_end[-1]
    block_expert = jnp.sum((block_start[:, None] >= pad_end[None, :]).astype(jnp.int32), axis=1)
    last_expert = jnp.max(jnp.where(counts > 0, jnp.arange(N_EXPERTS, dtype=jnp.int32), 0))
    block_expert = jnp.where(active, block_expert, last_expert)
    block_valid = jnp.clip(pad_start[block_expert] + counts[block_expert] - block_start, 0, blk)
    block_valid = jnp.where(active, block_valid, 0).astype(jnp.int32)

    xin = _dispatch(dest, x1_2d.reshape(s, ROW_TILE, LANES), n_blocks * blk)
    yb = _experts(block_expert, block_valid, xin.reshape(n_blocks * blk * ROW_TILE, LANES),
                  w_gu, b_gu, w_dn, b_dn, layer)
    return _combine(dest, yb.reshape(n_blocks * blk, ROW_TILE, LANES), gw, x1_2d,
                    ln_g[None, :], ln_b[None, :])


def kernel(x, w_in, fox_fb, gla_a_up, gla_a_b, gla_norm_g, w_branch, w_out, ln1_g, ln1_b,
           w_router, b_router, w_gu, b_gu, w_dn, b_dn, ln2_g, ln2_b):
    b, s, d = x.shape
    assert b == 1 and d == D_MODEL
    h = x.reshape(s, d)
    n_le = DEPTH * N_EXPERTS
    w_gu = w_gu.reshape(n_le, D_MODEL, 2 * D_MODEL)
    b_gu = b_gu.reshape(n_le, 1, 2 * D_MODEL)
    w_dn = w_dn.reshape(n_le, D_MODEL, D_MODEL)
    b_dn = b_dn.reshape(n_le, 1, D_MODEL)
    for l in range(DEPTH):
        x1_2d, sel, idxf, gw = _mixer_layer(
            h, w_in[l], fox_fb[l], gla_a_up[l], gla_a_b[l], gla_norm_g[l], w_branch[l], w_out[l],
            ln1_g[l], ln1_b[l], w_router[l], b_router[l])
        h = _moe_layer(x1_2d, sel, idxf, gw, w_gu, b_gu, w_dn, b_dn, ln2_g[l], ln2_b[l], l)
    return h.reshape(b, s, d)
```

```python
import functools

import jax
import jax.numpy as jnp
from jax import lax
from jax.experimental import pallas as pl
from jax.experimental.pallas import tpu as pltpu

F32 = jnp.float32
BF16 = jnp.bfloat16

D_MODEL = 1024
DEPTH = 4
N_HEADS = 4
HEAD_DIM = 64
GLA_DK = 32
GLA_KW = N_HEADS * GLA_DK
GLA_RANK = 16
GLA_TAU = 16.0
MOBA_BLOCK = 256
MOBA_TOPK = 3
BRANCH_WIDTH = 256
N_EXPERTS = 32
TOP_K = 4
SWIGLU_LIMIT = 7.0
SWIGLU_ALPHA = 1.702
ALPHA = (2 * DEPTH) ** 0.25
LN_EPS = 1e-5
RMS_EPS = 1e-6
LOG2E = 1.4426950408889634

LANES = 128
ROW_TILE = 8
NEG = -1e30
UNSELECTED = -32768.0
VMEM_LIMIT = 48 * 1024 * 1024

P_GATES = 0
P_FOX = 3 * D_MODEL
P_MOBA = P_FOX + 3 * BRANCH_WIDTH
P_GLA = P_MOBA + 3 * BRANCH_WIDTH
P_WIDTH = P_GLA + 2 * GLA_KW + 2 * BRANCH_WIDTH
TAIL_FF = 0
TAIL_GA = 4

EXPERT_BLOCK = 256


def _params(sem):
    return pltpu.CompilerParams(dimension_semantics=sem, vmem_limit_bytes=VMEM_LIMIT)


def _split3(x):
    hi = x.astype(BF16)
    r1 = x - hi.astype(F32)
    mid = r1.astype(BF16)
    lo = (r1 - mid.astype(F32)).astype(BF16)
    return hi, mid, lo


def _split2(x):
    hi = x.astype(BF16)
    lo = (x - hi.astype(F32)).astype(BF16)
    return hi, lo


def _dot(a, b):
    return jnp.dot(a, b, preferred_element_type=F32)


def _dot_nt(a, b):
    return lax.dot_general(a, b, (((1,), (1,)), ((), ())), preferred_element_type=F32)


def _dot_tn(a, b):
    return lax.dot_general(a, b, (((0,), (0,)), ((), ())), preferred_element_type=F32)


def _log_sigmoid(t):
    return jnp.minimum(t, 0.0) - jnp.log1p(jnp.exp(-jnp.abs(t)))


def _sigmoid(t):
    return 0.5 * jnp.tanh(0.5 * t) + 0.5


def _iota(shape, axis):
    return lax.broadcasted_iota(jnp.int32, shape, axis)


def _transpose_bf16(x):
    return x.astype(F32).T.astype(BF16)


def _values_t(v):
    ones_row = jnp.where(_iota((HEAD_DIM, v.shape[0]), 0) == 0, 1.0, 0.0).astype(BF16)
    return jnp.concatenate([_transpose_bf16(v), ones_row], axis=0)


def _aug_specs(s, t):
    aug = jax.ShapeDtypeStruct((N_HEADS, s, LANES), BF16)
    aug_spec = pl.BlockSpec((N_HEADS, t, LANES), lambda i: (0, i, 0))
    aug_t = jax.ShapeDtypeStruct((N_HEADS, LANES, s), BF16)
    aug_t_spec = pl.BlockSpec((N_HEADS, LANES, t), lambda i: (0, 0, i))
    return aug, aug_spec, aug_t, aug_t_spec


def _inproj_kernel(x_ref, w_ref, wt_ref, sc_ref, p_ref, t_ref, xb_ref):
    @pl.when(pl.program_id(1) == 0)
    def _():
        xb_ref[...] = x_ref[...].astype(BF16)
        t_ref[...] = _dot(xb_ref[...], wt_ref[...])

    p_ref[...] = (_dot(xb_ref[...], w_ref[...]) * sc_ref[...]).astype(BF16)


def _inproj(x, w, wt, tm=1024, tn=768):
    s = x.shape[0]
    col = jnp.arange(P_WIDTH)
    is_q = ((col >= P_FOX) & (col < P_FOX + BRANCH_WIDTH)) | ((col >= P_MOBA) & (col < P_MOBA + BRANCH_WIDTH))
    scale = jnp.where(is_q, HEAD_DIM ** -0.5 * LOG2E, 1.0).astype(F32)[None, :]
    return pl.pallas_call(
        _inproj_kernel,
        grid=(s // tm, P_WIDTH // tn),
        in_specs=[pl.BlockSpec((tm, D_MODEL), lambda i, j: (i, 0)),
                  pl.BlockSpec((D_MODEL, tn), lambda i, j: (0, j)),
                  pl.BlockSpec((D_MODEL, LANES), lambda i, j: (0, 0)),
                  pl.BlockSpec((1, tn), lambda i, j: (0, j))],
        out_specs=[pl.BlockSpec((tm, tn), lambda i, j: (i, j)),
                   pl.BlockSpec((tm, LANES), lambda i, j: (i, 0))],
        out_shape=[jax.ShapeDtypeStruct((s, P_WIDTH), BF16),
                   jax.ShapeDtypeStruct((s, LANES), F32)],
        scratch_shapes=[pltpu.VMEM((tm, D_MODEL), BF16)],
        compiler_params=_params(("parallel", "arbitrary")),
    )(x, w, wt, scale)


def _fox_prep_kernel(q_ref, k_ref, v_ref, t_ref, fb_ref, qa_ref, ka_ref, va_ref, st_ref, carry_ref):
    t = q_ref.shape[0]

    @pl.when(pl.program_id(0) == 0)
    def _():
        carry_ref[...] = jnp.zeros_like(carry_ref)

    ls = _log_sigmoid(t_ref[...] + fb_ref[...])
    tri = (_iota((t, t), 0) >= _iota((t, t), 1)).astype(BF16)
    hi, mid, lo = _split3(ls)
    c = _dot(tri, hi) + _dot(tri, mid) + _dot(tri, lo) + carry_ref[...]
    carry_ref[...] = c[t - 1:t, :]

    c2 = LOG2E * c

    lane = _iota((t, HEAD_DIM), 1)
    q_aug = jnp.where(_iota((HEAD_DIM, t), 0) < 3, 1.0, 0.0).astype(BF16)
    q = q_ref[...]
    k = k_ref[...]
    v = v_ref[...]
    srow = _iota((ROW_TILE, LANES), 0)
    slane = _iota((ROW_TILE, LANES), 1)
    stat = jnp.where(srow == 2, c2[0:1, :], jnp.where(srow == 3, c2[t - 1:t, :], 0.0))
    for h in range(N_HEADS):
        sl = slice(h * HEAD_DIM, (h + 1) * HEAD_DIM)
        nhi, nmid, nlo = (piece.astype(F32) for piece in _split3(-c2[:, h:h + 1]))
        k_aug = jnp.where(lane == 0, nhi, jnp.where(lane == 1, nmid, jnp.where(lane == 2, nlo, 0.0)))
        qa_ref[h] = jnp.concatenate([_transpose_bf16(q[:, sl]), q_aug], axis=0)
        ka_ref[h] = jnp.concatenate([k[:, sl], k_aug.astype(BF16)], axis=1)
        va_ref[h] = _values_t(v[:, sl])
        for r, x in ((0, q), (1, k)):
            xf = x[:, sl].astype(F32)
            norm = jnp.sqrt(jnp.max(jnp.sum(xf * xf, axis=1, keepdims=True), axis=0, keepdims=True))
            stat = jnp.where(jnp.logical_and(srow == r, slane == h), norm, stat)
    st_ref[0] = stat


def _fox_prep(p, tail, fb_row, t=512):
    s = p.shape[0]
    cb = P_FOX // BRANCH_WIDTH
    aug, aug_spec, aug_t, aug_t_spec = _aug_specs(s, t)
    return pl.pallas_call(
        _fox_prep_kernel,
        grid=(s // t,),
        in_specs=[pl.BlockSpec((t, BRANCH_WIDTH), lambda i: (i, cb)),
                  pl.BlockSpec((t, BRANCH_WIDTH), lambda i: (i, cb + 1)),
                  pl.BlockSpec((t, BRANCH_WIDTH), lambda i: (i, cb + 2)),
                  pl.BlockSpec((t, LANES), lambda i: (i, 0)),
                  pl.BlockSpec((1, LANES), lambda i: (0, 0))],
        out_specs=[aug_t_spec, aug_spec, aug_t_spec,
                   pl.BlockSpec((1, ROW_TILE, LANES), lambda i: (i, 0, 0))],
        out_shape=[aug_t, aug, aug_t, jax.ShapeDtypeStruct((s // t, ROW_TILE, LANES), F32)],
        scratch_shapes=[pltpu.VMEM((1, LANES), F32)],
        compiler_params=_params(("arbitrary",)),
    )(p, p, p, tail, fb_row)


SKIP_LOG2 = 80.0


def _fox_first_tile(stats, s, tq, tk):
    tp = s // stats.shape[0]
    qn, kn, c_first, c_last = (stats[:, r, :N_HEADS] for r in range(4))
    nq, nk = s // tq, s // tk
    qn = jnp.max(qn.reshape(nq, tq // tp, N_HEADS), axis=1)
    c_q = c_first.reshape(nq, tq // tp, N_HEADS)[:, 0, :]
    c_k = c_last.reshape(nk, tk // tp, N_HEADS)[:, -1, :]
    bound = (2.0 * qn * jnp.max(kn, axis=0))[:, None, :] + c_q[:, None, :] - c_k[None, :, :]
    j = jnp.arange(nk, dtype=jnp.int32)[None, :, None]
    first = jnp.min(jnp.where(bound >= -SKIP_LOG2, j, nk), axis=1)
    return first.T.reshape(-1).astype(jnp.int32)


def _moba_prep_kernel(q_ref, k_ref, v_ref, qa_ref, ka_ref, va_ref, kmean_ref):
    i = pl.program_id(0)
    t = q_ref.shape[0]
    nbl = kmean_ref.shape[0]

    @pl.when(i == 0)
    def _():
        kmean_ref[...] = jnp.zeros_like(kmean_ref)

    q = q_ref[...]
    k = k_ref[...]
    v = v_ref[...]
    km = kmean_ref[...]
    k_aug = jnp.where(_iota((t, HEAD_DIM), 1) == i, 1.0, 0.0).astype(BF16)
    blk = _iota((HEAD_DIM, t), 0)
    for h in range(N_HEADS):
        sl = slice(h * HEAD_DIM, (h + 1) * HEAD_DIM)
        qt = _transpose_bf16(q[:, sl])
        kh_hi, kh_lo = _split2(km[:HEAD_DIM, sl])
        gate = _dot(kh_hi, qt) + _dot(kh_lo, qt)
        g = jnp.where(blk < i, gate, NEG)
        bias = jnp.where(blk == i, 0.0, UNSELECTED)
        for _ in range(MOBA_TOPK):
            mx = jnp.max(g, axis=0, keepdims=True)
            idx = jnp.min(jnp.where(g == mx, blk, HEAD_DIM), axis=0, keepdims=True)
            pick = jnp.logical_and(blk == idx, mx > 0.5 * NEG)
            bias = jnp.where(pick, 0.0, bias)
            g = jnp.where(blk == idx, NEG, g)
        qa_ref[h] = jnp.concatenate([qt, bias.astype(BF16)], axis=0)
        ka_ref[h] = jnp.concatenate([k[:, sl], k_aug], axis=1)
        va_ref[h] = _values_t(v[:, sl])
    kmean = jnp.mean(k.astype(F32), axis=0, keepdims=True)
    kmean_ref[...] = jnp.where(_iota((nbl, BRANCH_WIDTH), 0) == i, kmean, km)


def _moba_prep(p):
    s = p.shape[0]
    t = MOBA_BLOCK
    assert s % t == 0 and s // t <= HEAD_DIM, "block one-hot must fit the 64 augmentation lanes"
    cb = P_MOBA // BRANCH_WIDTH
    aug, aug_spec, aug_t, aug_t_spec = _aug_specs(s, t)
    return pl.pallas_call(
        _moba_prep_kernel,
        grid=(s // t,),
        in_specs=[pl.BlockSpec((t, BRANCH_WIDTH), lambda i: (i, cb)),
                  pl.BlockSpec((t, BRANCH_WIDTH), lambda i: (i, cb + 1)),
                  pl.BlockSpec((t, BRANCH_WIDTH), lambda i: (i, cb + 2))],
        out_specs=[aug_t_spec, aug_spec, aug_t_spec],
        out_shape=[aug_t, aug, aug_t],
        scratch_shapes=[pltpu.VMEM((HEAD_DIM, BRANCH_WIDTH), F32)],
        compiler_params=_params(("arbitrary",)),
    )(p, p, p)


def _flash_kernel(first_ref, qt_ref, k_ref, vt_ref, o_ref, s0_ref, s1_ref, *, tq, tk):
    h = pl.program_id(0)
    i = pl.program_id(1)
    qt = qt_ref[0]
    n_diag = tq // tk
    n_full = i * n_diag

    def scores(j):
        return _dot(k_ref[0, pl.ds(pl.multiple_of(j * tk, tk), tk), :], qt)

    def absorb(j, s, m, acc, masked):
        if masked:
            key = _iota((tk, tq), 0) + j * tk
            qry = _iota((tk, tq), 1) + i * tq
            s = jnp.where(key <= qry, s, NEG)
        m_new = jnp.maximum(m, jnp.max(s, axis=0, keepdims=True))
        p = jnp.exp2(s - m_new).astype(BF16)
        vt = vt_ref[0, :, pl.ds(pl.multiple_of(j * tk, tk), tk)]
        return m_new, jnp.exp2(m - m_new) * acc + _dot(vt, p)

    j0 = jnp.minimum(first_ref[h * pl.num_programs(1) + i], n_full)
    m = jnp.full((1, tq), NEG, F32)
    acc = jnp.zeros((LANES, tq), F32)

    odd = (n_full - j0) % 2

    def plain(j, carry):
        return absorb(j, scores(j), *carry, False)

    m, acc = lax.fori_loop(j0, j0 + odd, plain, (m, acc))
    j0 = j0 + odd

    s0_ref[...] = scores(j0)

    def pair(g, carry):
        j = j0 + 2 * g
        s1_ref[...] = scores(j + 1)
        carry = absorb(j, s0_ref[...], *carry, False)
        s0_ref[...] = scores(j + 2)
        return absorb(j + 1, s1_ref[...], *carry, False)

    m, acc = lax.fori_loop(0, (n_full - j0) // 2, pair, (m, acc))
    s = s0_ref[...]
    for d in range(n_diag):
        s_next = scores(n_full + d + 1) if d + 1 < n_diag else None
        m, acc = absorb(n_full + d, s, m, acc, True)
        s = s_next
    o_ref[0] = (acc / acc[HEAD_DIM:HEAD_DIM + 1, :]).T.astype(o_ref.dtype)


def _flash(first_tile, qt, ka, vt, tq, tk):
    nh, s, _ = ka.shape
    return pl.pallas_call(
        functools.partial(_flash_kernel, tq=tq, tk=tk),
        grid_spec=pltpu.PrefetchScalarGridSpec(
            num_scalar_prefetch=1,
            grid=(nh, s // tq),
            in_specs=[pl.BlockSpec((1, LANES, tq), lambda h, i, f: (h, 0, i)),
                      pl.BlockSpec((1, s, LANES), lambda h, i, f: (h, 0, 0)),
                      pl.BlockSpec((1, LANES, s), lambda h, i, f: (h, 0, 0))],
            out_specs=pl.BlockSpec((1, tq, LANES), lambda h, i, f: (h, i, 0)),
            scratch_shapes=[pltpu.VMEM((tk, tq), F32), pltpu.VMEM((tk, tq), F32)]),
        out_shape=jax.ShapeDtypeStruct((nh, s, LANES), BF16),
        compiler_params=_params(("parallel", "parallel")),
    )(first_tile, qt, ka, vt)


def _flash_tiles(s):
    tq = min(1024, s)
    return tq, min(512, tq)


GLA_SUB = 16


def _gla_kernel(q_ref, k_ref, v_ref, r_ref, t_ref, aup_ref, ab_ref, g_ref, y_ref,
                st_ref, b_ref, o_ref):
    t = q_ref.shape[0]
    c = GLA_SUB

    @pl.when(pl.program_id(0) == 0)
    def _():
        st_ref[...] = jnp.zeros_like(st_ref)

    t_hi, t_lo = _split2(t_ref[...])
    a_hi, a_lo = _split2(aup_ref[...])
    z = _dot(t_hi, a_hi) + _dot(t_lo, a_hi) + _dot(t_hi, a_lo) + ab_ref[...]
    log_a = _log_sigmoid(z) * (1.0 / GLA_TAU)
    row = _iota((t, t), 0)
    col = _iota((t, t), 1)
    tri = jnp.logical_and(row // c == col // c, row >= col).astype(BF16)
    hi, mid, lo = _split3(log_a)
    b_ref[...] = _dot(tri, hi) + _dot(tri, mid) + _dot(tri, lo)

    expand = (_iota((GLA_KW, BRANCH_WIDTH), 0) // GLA_DK
              == _iota((GLA_KW, BRANCH_WIDTH), 1) // HEAD_DIM).astype(BF16)
    st_mask = (_iota((BRANCH_WIDTH, GLA_KW), 0) // HEAD_DIM
               == _iota((BRANCH_WIDTH, GLA_KW), 1) // GLA_DK).astype(F32)
    srow = _iota((c, GLA_KW), 0)
    orow = _iota((c, BRANCH_WIDTH), 0)
    scale = GLA_DK ** -0.5

    def step(r, _):
        r0 = pl.multiple_of(r * c, c)
        qs = q_ref[pl.ds(r0, c), :].astype(F32) * scale
        ks = k_ref[pl.ds(r0, c), :].astype(F32)
        vb = v_ref[pl.ds(r0, c), :]
        vf = vb.astype(F32)
        bs = b_ref[pl.ds(r0, c), :]
        b_last = bs[c - 1:c, :]
        st = st_ref[...]

        o_inter = _dot_nt((qs * jnp.exp(bs)).astype(BF16), st.astype(BF16))

        pieces = []
        for tt in range(c):
            e = jnp.exp(jnp.where(srow <= tt, bs[tt:tt + 1, :] - bs, NEG))
            pieces.append(e * qs[tt:tt + 1, :] * ks)
        pm = jnp.concatenate(pieces, axis=0).astype(BF16)
        a = _dot(pm, expand)
        o_intra = jnp.zeros((c, BRANCH_WIDTH), F32)
        for tt in range(c):
            o_t = jnp.sum(a[tt * c:(tt + 1) * c, :] * vf, axis=0, keepdims=True)
            o_intra = jnp.where(orow == tt, o_t, o_intra)
        o_ref[pl.ds(r0, c), :] = o_inter + o_intra

        ke = (ks * jnp.exp(b_last - bs)).astype(BF16)
        st_ref[...] = st * jnp.exp(b_last) + _dot_tn(vb, ke) * st_mask
        return 0

    lax.fori_loop(0, t // c, step, 0)

    o = o_ref[...]
    ones_bd = (_iota((BRANCH_WIDTH, BRANCH_WIDTH), 0) // HEAD_DIM
               == _iota((BRANCH_WIDTH, BRANCH_WIDTH), 1) // HEAD_DIM).astype(BF16)
    sq_hi, sq_mid, sq_lo = _split3(o * o)
    ms = (_dot(sq_hi, ones_bd) + _dot(sq_mid, ones_bd) + _dot(sq_lo, ones_bd)) * (1.0 / HEAD_DIM)
    gr = r_ref[...].astype(F32)
    y = o * lax.rsqrt(ms + RMS_EPS) * g_ref[...] * (gr * _sigmoid(gr))
    y_ref[...] = y.astype(y_ref.dtype)


def _gla(p, tail, aup_pad, ab_row, g_row, t=512):
    s = p.shape[0]
    cq = P_GLA // GLA_KW
    cv = (P_GLA + 2 * GLA_KW) // BRANCH_WIDTH
    return pl.pallas_call(
        _gla_kernel,
        grid=(s // t,),
        in_specs=[pl.BlockSpec((t, GLA_KW), lambda i: (i, cq)),
                  pl.BlockSpec((t, GLA_KW), lambda i: (i, cq + 1)),
                  pl.BlockSpec((t, BRANCH_WIDTH), lambda i: (i, cv)),
                  pl.BlockSpec((t, BRANCH_WIDTH), lambda i: (i, cv + 1)),
                  pl.BlockSpec((t, LANES), lambda i: (i, 0)),
                  pl.BlockSpec((LANES, GLA_KW), lambda i: (0, 0)),
                  pl.BlockSpec((1, GLA_KW), lambda i: (0, 0)),
                  pl.BlockSpec((1, BRANCH_WIDTH), lambda i: (0, 0))],
        out_specs=pl.BlockSpec((t, BRANCH_WIDTH), lambda i: (i, 0)),
        out_shape=jax.ShapeDtypeStruct((s, BRANCH_WIDTH), BF16),
        scratch_shapes=[pltpu.VMEM((BRANCH_WIDTH, GLA_KW), F32),
                        pltpu.VMEM((t, GLA_KW), F32),
                        pltpu.VMEM((t, BRANCH_WIDTH), F32)],
        compiler_params=_params(("arbitrary",)),
    )(p, p, p, p, tail, aup_pad, ab_row, g_row)


def _layer_norm(z, g, b):
    mu = jnp.mean(z, axis=1, keepdims=True)
    zc = z - mu
    var = jnp.mean(zc * zc, axis=1, keepdims=True)
    return zc * lax.rsqrt(var + LN_EPS) * g + b


def _store_row_tiles(ref, val):
    n = val.shape[0]
    for a in range(ROW_TILE):
        ref[pl.ds(a, n, stride=ROW_TILE), :] = val[:, a * LANES:(a + 1) * LANES]


def _load_row_tiles(ref, n, base=0):
    return jnp.concatenate(
        [ref[pl.ds(base + a, n, stride=ROW_TILE), :] for a in range(ROW_TILE)], axis=1)


def _merge_kernel(fox_ref, gla_ref, moba_ref, g0_ref, g1_ref, g2_ref, x_ref, wb_ref, wo_ref,
                  lg_ref, lb_ref, wrh_ref, wrl_ref, br_ref,
                  x1_ref, sel_ref, idx_ref, gw_ref):
    tm = x_ref.shape[0]

    def heads(ref):
        return jnp.concatenate([ref[h][:, :HEAD_DIM] for h in range(N_HEADS)], axis=1)

    merged = _sigmoid(g0_ref[...].astype(F32)) * _dot(heads(fox_ref), wb_ref[0])
    merged += _sigmoid(g1_ref[...].astype(F32)) * _dot(gla_ref[...], wb_ref[1])
    merged += _sigmoid(g2_ref[...].astype(F32)) * _dot(heads(moba_ref), wb_ref[2])
    z = ALPHA * x_ref[...] + _dot(merged.astype(BF16), wo_ref[...])
    x1 = _layer_norm(z, lg_ref[...], lb_ref[...])
    _store_row_tiles(x1_ref, x1)

    x_hi, x_lo = _split2(x1)
    logits = (_dot(x_hi, wrh_ref[...]) + _dot(x_lo, wrh_ref[...]) + _dot(x_hi, wrl_ref[...])
              + br_ref[...])
    lane = _iota((tm, LANES), 1)
    lg = jnp.where(lane < N_EXPERTS, logits, NEG)
    sel = jnp.zeros((tm, LANES), F32)
    idxf = jnp.zeros((tm, LANES), F32)
    ew = jnp.zeros((tm, LANES), F32)
    top = None
    for r in range(TOP_K):
        mx = jnp.max(lg, axis=1, keepdims=True)
        idx = jnp.min(jnp.where(lg == mx, lane, LANES), axis=1, keepdims=True)
        hit = lane == idx
        top = mx if top is None else top
        sel = jnp.where(hit, 1.0, sel)
        idxf = jnp.where(lane == r, idx.astype(F32), idxf)
        ew = jnp.where(lane == r, jnp.exp(mx - top), ew)
        lg = jnp.where(hit, NEG, lg)
    sel_ref[...] = sel
    idx_ref[...] = idxf
    gw_ref[...] = ew / jnp.sum(ew, axis=1, keepdims=True)


def _merge(fox_o, y_gla, moba_o, p, x, wb, wo, lg, lb, wrh, wrl, br, tm=256):
    s = x.shape[0]
    head_spec = pl.BlockSpec((N_HEADS, tm, LANES), lambda i: (0, i, 0))
    row = lambda n: pl.BlockSpec((1, n), lambda i: (0, 0))
    small = jax.ShapeDtypeStruct((s, LANES), F32)
    small_spec = pl.BlockSpec((tm, LANES), lambda i: (i, 0))
    return pl.pallas_call(
        _merge_kernel,
        grid=(s // tm,),
        in_specs=[head_spec,
                  pl.BlockSpec((tm, BRANCH_WIDTH), lambda i: (i, 0)),
                  head_spec,
                  pl.BlockSpec((tm, D_MODEL), lambda i: (i, 0)),
                  pl.BlockSpec((tm, D_MODEL), lambda i: (i, 1)),
                  pl.BlockSpec((tm, D_MODEL), lambda i: (i, 2)),
                  pl.BlockSpec((tm, D_MODEL), lambda i: (i, 0)),
                  pl.BlockSpec((3, BRANCH_WIDTH, D_MODEL), lambda i: (0, 0, 0)),
                  pl.BlockSpec((D_MODEL, D_MODEL), lambda i: (0, 0)),
                  row(D_MODEL), row(D_MODEL),
                  pl.BlockSpec((D_MODEL, LANES), lambda i: (0, 0)),
                  pl.BlockSpec((D_MODEL, LANES), lambda i: (0, 0)),
                  row(LANES)],
        out_specs=[pl.BlockSpec((tm * ROW_TILE, LANES), lambda i: (i, 0)),
                   small_spec, small_spec, small_spec],
        out_shape=[jax.ShapeDtypeStruct((s * ROW_TILE, LANES), F32), small, small, small],
        compiler_params=_params(("parallel",)),
    )(fox_o, y_gla, moba_o, p, p, p, x, wb, wo, lg, lb, wrh, wrl, br)


def _rank_kernel(sel_ref, idx_ref, dest_ref, cnt_ref, carry_ref, start_ref):
    phase = pl.program_id(0)
    i = pl.program_id(1)
    t = sel_ref.shape[0]
    sel = sel_ref[...]
    chosen = jnp.sum(sel, axis=0, keepdims=True)

    @pl.when(jnp.logical_and(phase == 0, i == 0))
    def _():
        carry_ref[...] = jnp.zeros_like(carry_ref)

    @pl.when(jnp.logical_and(phase == 1, i == 0))
    def _():
        counts = carry_ref[...]
        cnt_ref[...] = counts
        blocks = jnp.floor((counts + (EXPERT_BLOCK - 1)) * (1.0 / EXPERT_BLOCK))
        before = (_iota((LANES, LANES), 0) < _iota((LANES, LANES), 1)).astype(BF16)
        first = _dot(jnp.broadcast_to(blocks, (ROW_TILE, LANES)).astype(BF16), before)
        start_ref[...] = first[0:1, :] * float(EXPERT_BLOCK)
        carry_ref[...] = jnp.zeros_like(carry_ref)

    @pl.when(phase == 1)
    def _():
        stri = (_iota((t, t), 0) > _iota((t, t), 1)).astype(BF16)
        pos = _dot(stri, sel.astype(BF16)) + carry_ref[...] + start_ref[...]
        lane = _iota((t, LANES), 1).astype(F32)
        idxf = idx_ref[...]
        dest = jnp.zeros((t, LANES), F32)
        for r in range(TOP_K):
            mine = jnp.sum(jnp.where(lane == idxf[:, r:r + 1], pos, 0.0), axis=1, keepdims=True)
            dest = jnp.where(lane == float(r), mine, dest)
        dest_ref[...] = dest.T[:ROW_TILE, :].astype(jnp.int32)

    carry_ref[...] += chosen


def _rank(sel, idxf, t=512):
    s = sel.shape[0]
    spec = pl.BlockSpec((t, LANES), lambda ph, i: (i, 0))
    return pl.pallas_call(
        _rank_kernel,
        grid=(2, s // t),
        in_specs=[spec, spec],
        out_specs=[pl.BlockSpec((ROW_TILE, t), lambda ph, i: (0, i * ph)),
                   pl.BlockSpec((1, LANES), lambda ph, i: (0, 0))],
        out_shape=[jax.ShapeDtypeStruct((ROW_TILE, s), jnp.int32),
                   jax.ShapeDtypeStruct((1, LANES), F32)],
        scratch_shapes=[pltpu.VMEM((1, LANES), F32), pltpu.VMEM((1, LANES), F32)],
        compiler_params=_params(("arbitrary", "arbitrary")),
    )(sel, idxf)


DMA_UNROLL = 4


def _token_rows(ref, token):
    return ref.at[pl.ds(pl.multiple_of(token * ROW_TILE, ROW_TILE), ROW_TILE), :]


def _dispatch_kernel(dest_ref, x_ref, xin_ref, sem):
    n = dest_ref.shape[1]

    def issue(g, _):
        for u in range(DMA_UNROLL):
            t = g * DMA_UNROLL + u
            for k in range(TOP_K):
                pltpu.make_async_copy(_token_rows(x_ref, t), _token_rows(xin_ref, dest_ref[k, t]),
                                      sem).start()
        return 0

    lax.fori_loop(0, n // DMA_UNROLL, issue, 0)
    for k in range(TOP_K):
        pltpu.make_async_copy(x_ref, xin_ref.at[pl.ds(0, n * ROW_TILE), :], sem).wait()


def _dispatch(dest, x1_2d, n_rows, td=512):
    s = x1_2d.shape[0] // ROW_TILE
    return pl.pallas_call(
        _dispatch_kernel,
        grid=(s // td,),
        in_specs=[pl.BlockSpec((ROW_TILE, td), lambda i: (0, i), memory_space=pltpu.SMEM),
                  pl.BlockSpec((td * ROW_TILE, LANES), lambda i: (i, 0))],
        out_specs=pl.BlockSpec(memory_space=pl.ANY),
        out_shape=jax.ShapeDtypeStruct((n_rows * ROW_TILE, LANES), F32),
        scratch_shapes=[pltpu.SemaphoreType.DMA(())],
        compiler_params=pltpu.CompilerParams(dimension_semantics=("arbitrary",),
                                             has_side_effects=True),
    )(dest, x1_2d)


HALF = LANES // 2
W_CAST_ROWS = 64


def _expert_kernel(be_ref, nv_ref, xin_ref, wgu_ref, bgu_ref, wdn_ref, bdn_ref, y_ref,
                   wgu_s, wdn_s, perm_s):
    b = pl.program_id(0)
    blk = y_ref.shape[0] // ROW_TILE
    nvalid = nv_ref[b]
    new_expert = jnp.logical_or(b == 0, be_ref[b] != be_ref[jnp.maximum(b - 1, 0)])

    @pl.when(jnp.logical_and(new_expert, nvalid > 0))
    def _():
        def cast(r, _):
            r0 = pl.multiple_of(r * W_CAST_ROWS, W_CAST_ROWS)
            wgu_s[pl.ds(r0, W_CAST_ROWS), :] = wgu_ref[0, pl.ds(r0, W_CAST_ROWS), :].astype(BF16)
            return 0

        lax.fori_loop(0, D_MODEL // W_CAST_ROWS, cast, 0)
        for c in range(D_MODEL // LANES):
            cols = slice(c * LANES, (c + 1) * LANES)
            for g in range(D_MODEL // LANES):
                lo = g * LANES
                perm_s[c, pl.ds(lo, HALF, stride=2), :] = wdn_ref[0, lo:lo + HALF, cols]
                perm_s[c, pl.ds(lo + 1, HALF, stride=2), :] = wdn_ref[0, lo + HALF:lo + LANES, cols]
            wdn_s[:, cols] = perm_s[c].astype(BF16)

    @pl.when(nvalid > 0)
    def _():
        x = _load_row_tiles(xin_ref, blk)
        x = jnp.where(_iota((blk, D_MODEL), 0) < nvalid, x, 0.0).astype(BF16)
        even = _iota((blk, LANES), 1) % 2 == 0
        acts = []
        for g in range(D_MODEL // LANES):
            lo = g * 2 * LANES
            h = _dot(x, wgu_s[:, lo:lo + 2 * LANES]) + bgu_ref[0, :, lo:lo + 2 * LANES]
            h_a = h[:, :LANES]
            h_b = h[:, LANES:]
            gate = jnp.where(even, h_a, pltpu.roll(h_b, 1, 1))
            up = jnp.where(even, pltpu.roll(h_a, LANES - 1, 1), h_b)
            gate = jnp.minimum(gate, SWIGLU_LIMIT)
            up = jnp.clip(up, -SWIGLU_LIMIT, SWIGLU_LIMIT)
            acts.append(((up + 1.0) * (gate * _sigmoid(SWIGLU_ALPHA * gate))).astype(BF16))
        act = jnp.concatenate(acts, axis=1)
        _store_row_tiles(y_ref, _dot(act, wdn_s[...]) + bdn_ref[0])

    @pl.when(nvalid <= 0)
    def _():
        y_ref[...] = jnp.zeros_like(y_ref)


def _experts(block_expert, block_valid, xin2d, w_gu, b_gu, w_dn, b_dn, layer):
    blk = EXPERT_BLOCK
    nb = block_expert.shape[0]
    x_spec = pl.BlockSpec((blk * ROW_TILE, LANES), lambda b, be, nv: (b, 0))
    which = lambda b, be, nv: (layer * N_EXPERTS + be[b], 0, 0)
    return pl.pallas_call(
        _expert_kernel,
        grid_spec=pltpu.PrefetchScalarGridSpec(
            num_scalar_prefetch=2,
            grid=(nb,),
            in_specs=[x_spec,
                      pl.BlockSpec((1, D_MODEL, 2 * D_MODEL), which),
                      pl.BlockSpec((1, 1, 2 * D_MODEL), which),
                      pl.BlockSpec((1, D_MODEL, D_MODEL), which),
                      pl.BlockSpec((1, 1, D_MODEL), which)],
            out_specs=x_spec,
            scratch_shapes=[pltpu.VMEM((D_MODEL, 2 * D_MODEL), BF16),
                            pltpu.VMEM((D_MODEL, D_MODEL), BF16),
                            pltpu.VMEM((D_MODEL // LANES, D_MODEL, LANES), F32)]),
        out_shape=jax.ShapeDtypeStruct((nb * blk * ROW_TILE, LANES), F32),
        compiler_params=pltpu.CompilerParams(dimension_semantics=("arbitrary",),
                                             vmem_limit_bytes=56 * 1024 * 1024),
    )(block_expert, block_valid, xin2d, w_gu, b_gu, w_dn, b_dn)


def _combine_kernel(dest_ref, next_ref, yb_ref, gw_ref, x1_ref, lg_ref, lb_ref, out_ref,
                    buf_ref, sem):
    i = pl.program_id(0)
    tc = out_ref.shape[0]
    slot = i % 2

    def gather(rows_ref, into):
        def body(g, _):
            for u in range(DMA_UNROLL):
                t = g * DMA_UNROLL + u
                for k in range(TOP_K):
                    pltpu.make_async_copy(_token_rows(yb_ref, rows_ref[k, t]),
                                          _token_rows(buf_ref.at[into, k], t), sem.at[into]).start()
            return 0

        lax.fori_loop(0, tc // DMA_UNROLL, body, 0)

    @pl.when(i == 0)
    def _():
        gather(dest_ref, 0)

    @pl.when(i + 1 < pl.num_programs(0))
    def _():
        gather(next_ref, 1 - slot)

    for k in range(TOP_K):
        pltpu.make_async_copy(yb_ref.at[pl.ds(0, tc * ROW_TILE), :], buf_ref.at[slot, k],
                              sem.at[slot]).wait()

    gw = gw_ref[...]
    f = jnp.zeros((tc, D_MODEL), F32)
    for k in range(TOP_K):
        f += gw[:, k:k + 1] * _load_row_tiles(buf_ref.at[slot, k], tc)
    z = ALPHA * _load_row_tiles(x1_ref, tc) + f
    out_ref[...] = _layer_norm(z, lg_ref[...], lb_ref[...])


def _combine(dest, yb_rows, gw, x1_2d, lg, lb, tc=256):
    s = gw.shape[0]
    n = s // tc
    row = pl.BlockSpec((1, D_MODEL), lambda i: (0, 0))
    return pl.pallas_call(
        _combine_kernel,
        grid=(n,),
        in_specs=[pl.BlockSpec((ROW_TILE, tc), lambda i: (0, i), memory_space=pltpu.SMEM),
                  pl.BlockSpec((ROW_TILE, tc), lambda i: (0, jnp.minimum(i + 1, n - 1)),
                               memory_space=pltpu.SMEM),
                  pl.BlockSpec(memory_space=pl.ANY),
                  pl.BlockSpec((tc, LANES), lambda i: (i, 0)),
                  pl.BlockSpec((tc * ROW_TILE, LANES), lambda i: (i, 0)),
                  row, row],
        out_specs=pl.BlockSpec((tc, D_MODEL), lambda i: (i, 0)),
        out_shape=jax.ShapeDtypeStruct((s, D_MODEL), F32),
        scratch_shapes=[pltpu.VMEM((2, TOP_K, tc * ROW_TILE, LANES), F32),
                        pltpu.SemaphoreType.DMA((2,))],
        compiler_params=_params(("arbitrary",)),
    )(dest, dest, yb_rows, gw, x1_2d, lg, lb)


def _permute_w_in(w_in):
    sizes = (256, 256, 256, 4, 128, 128, 256, 16, 256, 256, 256, 256, 3 * D_MODEL)
    offs = [0]
    for n in sizes:
        offs.append(offs[-1] + n)
    fq, fk, fv, ff, gq, gk, gv, ga, gr, mq, mk, mv, gates = (
        w_in[:, offs[i]:offs[i + 1]] for i in range(len(sizes)))
    main = jnp.concatenate([gates, fq, fk, fv, mq, mk, mv, gq, gk, gv, gr], axis=1).astype(BF16)
    pad = jnp.zeros((D_MODEL, LANES - 4 - GLA_RANK), w_in.dtype)
    tail = jnp.concatenate([ff, ga, pad], axis=1).astype(BF16)
    return main, tail


def _pad_row(v, n=LANES, fill=0.0):
    return jnp.pad(v.astype(F32), (0, n - v.shape[0]), constant_values=fill)[None, :]


def _mixer_layer(x, w_in, fox_fb, gla_a_up, gla_a_b, gla_norm_g, w_branch, w_out, ln_g, ln_b,
                 w_router, b_router):
    w_main, w_tail = _permute_w_in(w_in)
    p, tail = _inproj(x, w_main, w_tail)

    s = x.shape[0]
    tq, tk = _flash_tiles(s)
    fox_qt, fox_k, fox_vt, fox_stats = _fox_prep(p, tail, _pad_row(fox_fb))
    fox_o = _flash(_fox_first_tile(fox_stats, s, tq, tk), fox_qt, fox_k, fox_vt, tq, tk)
    moba_o = _flash(jnp.zeros((N_HEADS * (s // tq),), jnp.int32), *_moba_prep(p), tq, tk)
    aup_pad = jnp.zeros((LANES, GLA_KW), F32).at[TAIL_GA:TAIL_GA + GLA_RANK].set(gla_a_up)
    y_gla = _gla(p, tail, aup_pad, gla_a_b[None, :], gla_norm_g[None, :])

    wr = jnp.pad(w_router, ((0, 0), (0, LANES - N_EXPERTS)))
    wr_hi = wr.astype(BF16)
    wr_lo = (wr - wr_hi.astype(F32)).astype(BF16)
    return _merge(fox_o, y_gla, moba_o, p, x, w_branch.astype(BF16), w_out.astype(BF16),
                  ln_g[None, :], ln_b[None, :], wr_hi, wr_lo, _pad_row(b_router))


def _moe_layer(x1_2d, sel, idxf, gw, w_gu, b_gu, w_dn, b_dn, ln_g, ln_b, layer):
    s = sel.shape[0]
    blk = EXPERT_BLOCK
    dest, counts = _rank(sel, idxf)

    counts = counts[0, :N_EXPERTS].astype(jnp.int32)
    padded = (counts + blk - 1) // blk * blk
    pad_end = jnp.cumsum(padded)
    pad_start = pad_end - padded
    n_blocks = (s * TOP_K + N_EXPERTS * (blk - 1)) // blk + 1
    block_start = jnp.arange(n_blocks, dtype=jnp.int32) * blk
    active = block_start < pad_end[-1]
    block_expert = jnp.sum((block_start[:, None] >= pad_end[None, :]).astype(jnp.int32), axis=1)
    last_expert = jnp.max(jnp.where(counts > 0, jnp.arange(N_EXPERTS, dtype=jnp.int32), 0))
    block_expert = jnp.where(active, block_expert, last_expert)
    block_valid = jnp.clip(pad_start[block_expert] + counts[block_expert] - block_start, 0, blk)
    block_valid = jnp.where(active, block_valid, 0).astype(jnp.int32)

    xin = _dispatch(dest, x1_2d, n_blocks * blk)
    yb = _experts(block_expert, block_valid, xin, w_gu, b_gu, w_dn, b_dn, layer)
    return _combine(dest, yb, gw, x1_2d, ln_g[None, :], ln_b[None, :])


def kernel(x, w_in, fox_fb, gla_a_up, gla_a_b, gla_norm_g, w_branch, w_out, ln1_g, ln1_b,
           w_router, b_router, w_gu, b_gu, w_dn, b_dn, ln2_g, ln2_b):
    b, s, d = x.shape
    assert b == 1 and d == D_MODEL
    h = x.reshape(s, d)
    n_le = DEPTH * N_EXPERTS
    w_gu = w_gu.reshape(n_le, D_MODEL, 2 * D_MODEL)
    b_gu = b_gu.reshape(n_le, 1, 2 * D_MODEL)
    w_dn = w_dn.reshape(n_le, D_MODEL, D_MODEL)
    b_dn = b_dn.reshape(n_le, 1, D_MODEL)
    for l in range(DEPTH):
        x1_2d, sel, idxf, gw = _mixer_layer(
            h, w_in[l], fox_fb[l], gla_a_up[l], gla_a_b[l], gla_norm_g[l], w_branch[l], w_out[l],
            ln1_g[l], ln1_b[l], w_router[l], b_router[l])
        h = _moe_layer(x1_2d, sel, idxf, gw, w_gu, b_gu, w_dn, b_dn, ln2_g[l], ln2_b[l], l)
    return h.reshape(b, s, d)
```

```python
import functools

import jax
import jax.numpy as jnp
from jax import lax
from jax.experimental import pallas as pl
from jax.experimental.pallas import tpu as pltpu

F32 = jnp.float32
BF16 = jnp.bfloat16

D_MODEL = 1024
DEPTH = 4
N_HEADS = 4
HEAD_DIM = 64
GLA_DK = 32
GLA_KW = N_HEADS * GLA_DK
GLA_RANK = 16
GLA_TAU = 16.0
MOBA_BLOCK = 256
MOBA_TOPK = 3
BRANCH_WIDTH = 256
N_EXPERTS = 32
TOP_K = 4
SWIGLU_LIMIT = 7.0
SWIGLU_ALPHA = 1.702
ALPHA = (2 * DEPTH) ** 0.25
LN_EPS = 1e-5
RMS_EPS = 1e-6
LOG2E = 1.4426950408889634

LANES = 128
ROW_TILE = 8
NEG = -1e30
UNSELECTED = -32768.0
VMEM_LIMIT = 48 * 1024 * 1024

P_GATES = 0
P_FOX = 3 * D_MODEL
P_MOBA = P_FOX + 3 * BRANCH_WIDTH
P_GLA = P_MOBA + 3 * BRANCH_WIDTH
P_WIDTH = P_GLA + 2 * GLA_KW + 2 * BRANCH_WIDTH
TAIL_FF = 0
TAIL_GA = 4

EXPERT_BLOCK = 256


def _params(sem):
    return pltpu.CompilerParams(dimension_semantics=sem, vmem_limit_bytes=VMEM_LIMIT)


def _split3(x):
    hi = x.astype(BF16)
    r1 = x - hi.astype(F32)
    mid = r1.astype(BF16)
    lo = (r1 - mid.astype(F32)).astype(BF16)
    return hi, mid, lo


def _split2(x):
    hi = x.astype(BF16)
    lo = (x - hi.astype(F32)).astype(BF16)
    return hi, lo


def _dot(a, b):
    return jnp.dot(a, b, preferred_element_type=F32)


def _dot_nt(a, b):
    return lax.dot_general(a, b, (((1,), (1,)), ((), ())), preferred_element_type=F32)


def _dot_tn(a, b):
    return lax.dot_general(a, b, (((0,), (0,)), ((), ())), preferred_element_type=F32)


def _log_sigmoid(t):
    return jnp.minimum(t, 0.0) - jnp.log1p(jnp.exp(-jnp.abs(t)))


def _sigmoid(t):
    return 0.5 * jnp.tanh(0.5 * t) + 0.5


def _iota(shape, axis):
    return lax.broadcasted_iota(jnp.int32, shape, axis)


def _transpose_bf16(x):
    return x.astype(F32).T.astype(BF16)


def _values_t(v):
    ones_row = jnp.where(_iota((HEAD_DIM, v.shape[0]), 0) == 0, 1.0, 0.0).astype(BF16)
    return jnp.concatenate([_transpose_bf16(v), ones_row], axis=0)


def _aug_specs(s, t):
    aug = jax.ShapeDtypeStruct((N_HEADS, s, LANES), BF16)
    aug_spec = pl.BlockSpec((N_HEADS, t, LANES), lambda i: (0, i, 0))
    aug_t = jax.ShapeDtypeStruct((N_HEADS, LANES, s), BF16)
    aug_t_spec = pl.BlockSpec((N_HEADS, LANES, t), lambda i: (0, 0, i))
    return aug, aug_spec, aug_t, aug_t_spec


def _inproj_kernel(x_ref, w_ref, wt_ref, sc_ref, p_ref, t_ref, xb_ref):
    @pl.when(pl.program_id(1) == 0)
    def _():
        xb_ref[...] = x_ref[...].astype(BF16)
        t_ref[...] = _dot(xb_ref[...], wt_ref[...])

    p_ref[...] = (_dot(xb_ref[...], w_ref[...]) * sc_ref[...]).astype(BF16)


def _inproj(x, w, wt, tm=1024, tn=768):
    s = x.shape[0]
    col = jnp.arange(P_WIDTH)
    is_q = ((col >= P_FOX) & (col < P_FOX + BRANCH_WIDTH)) | ((col >= P_MOBA) & (col < P_MOBA + BRANCH_WIDTH))
    scale = jnp.where(is_q, HEAD_DIM ** -0.5 * LOG2E, 1.0).astype(F32)[None, :]
    return pl.pallas_call(
        _inproj_kernel,
        grid=(s // tm, P_WIDTH // tn),
        in_specs=[pl.BlockSpec((tm, D_MODEL), lambda i, j: (i, 0)),
                  pl.BlockSpec((D_MODEL, tn), lambda i, j: (0, j)),
                  pl.BlockSpec((D_MODEL, LANES), lambda i, j: (0, 0)),
                  pl.BlockSpec((1, tn), lambda i, j: (0, j))],
        out_specs=[pl.BlockSpec((tm, tn), lambda i, j: (i, j)),
                   pl.BlockSpec((tm, LANES), lambda i, j: (i, 0))],
        out_shape=[jax.ShapeDtypeStruct((s, P_WIDTH), BF16),
                   jax.ShapeDtypeStruct((s, LANES), F32)],
        scratch_shapes=[pltpu.VMEM((tm, D_MODEL), BF16)],
        compiler_params=_params(("parallel", "arbitrary")),
    )(x, w, wt, scale)


def _fox_prep_kernel(q_ref, k_ref, v_ref, t_ref, fb_ref, qa_ref, ka_ref, va_ref, st_ref, carry_ref):
    t = q_ref.shape[0]

    @pl.when(pl.program_id(0) == 0)
    def _():
        carry_ref[...] = jnp.zeros_like(carry_ref)

    ls = _log_sigmoid(t_ref[...] + fb_ref[...])
    tri = (_iota((t, t), 0) >= _iota((t, t), 1)).astype(BF16)
    hi, mid, lo = _split3(ls)
    c = _dot(tri, hi) + _dot(tri, mid) + _dot(tri, lo) + carry_ref[...]
    carry_ref[...] = c[t - 1:t, :]

    c2 = LOG2E * c

    lane = _iota((t, HEAD_DIM), 1)
    q_aug = jnp.where(_iota((HEAD_DIM, t), 0) < 3, 1.0, 0.0).astype(BF16)
    q = q_ref[...]
    k = k_ref[...]
    v = v_ref[...]
    srow = _iota((ROW_TILE, LANES), 0)
    slane = _iota((ROW_TILE, LANES), 1)
    stat = jnp.where(srow == 2, c2[0:1, :], jnp.where(srow == 3, c2[t - 1:t, :], 0.0))
    for h in range(N_HEADS):
        sl = slice(h * HEAD_DIM, (h + 1) * HEAD_DIM)
        nhi, nmid, nlo = (piece.astype(F32) for piece in _split3(-c2[:, h:h + 1]))
        k_aug = jnp.where(lane == 0, nhi, jnp.where(lane == 1, nmid, jnp.where(lane == 2, nlo, 0.0)))
        qa_ref[h] = jnp.concatenate([_transpose_bf16(q[:, sl]), q_aug], axis=0)
        ka_ref[h] = jnp.concatenate([k[:, sl], k_aug.astype(BF16)], axis=1)
        va_ref[h] = _values_t(v[:, sl])
        for r, x in ((0, q), (1, k)):
            xf = x[:, sl].astype(F32)
            norm = jnp.sqrt(jnp.max(jnp.sum(xf * xf, axis=1, keepdims=True), axis=0, keepdims=True))
            stat = jnp.where(jnp.logical_and(srow == r, slane == h), norm, stat)
    st_ref[0] = stat


def _fox_prep(p, tail, fb_row, t=512):
    s = p.shape[0]
    cb = P_FOX // BRANCH_WIDTH
    aug, aug_spec, aug_t, aug_t_spec = _aug_specs(s, t)
    return pl.pallas_call(
        _fox_prep_kernel,
        grid=(s // t,),
        in_specs=[pl.BlockSpec((t, BRANCH_WIDTH), lambda i: (i, cb)),
                  pl.BlockSpec((t, BRANCH_WIDTH), lambda i: (i, cb + 1)),
                  pl.BlockSpec((t, BRANCH_WIDTH), lambda i: (i, cb + 2)),
                  pl.BlockSpec((t, LANES), lambda i: (i, 0)),
                  pl.BlockSpec((1, LANES), lambda i: (0, 0))],
        out_specs=[aug_t_spec, aug_spec, aug_t_spec,
                   pl.BlockSpec((1, ROW_TILE, LANES), lambda i: (i, 0, 0))],
        out_shape=[aug_t, aug, aug_t, jax.ShapeDtypeStruct((s // t, ROW_TILE, LANES), F32)],
        scratch_shapes=[pltpu.VMEM((1, LANES), F32)],
        compiler_params=_params(("arbitrary",)),
    )(p, p, p, tail, fb_row)


SKIP_LOG2 = 80.0


def _fox_first_tile(stats, s, tq, tk):
    tp = s // stats.shape[0]
    qn, kn, c_first, c_last = (stats[:, r, :N_HEADS] for r in range(4))
    nq, nk = s // tq, s // tk
    qn = jnp.max(qn.reshape(nq, tq // tp, N_HEADS), axis=1)
    c_q = c_first.reshape(nq, tq // tp, N_HEADS)[:, 0, :]
    c_k = c_last.reshape(nk, tk // tp, N_HEADS)[:, -1, :]
    bound = (2.0 * qn * jnp.max(kn, axis=0))[:, None, :] + c_q[:, None, :] - c_k[None, :, :]
    j = jnp.arange(nk, dtype=jnp.int32)[None, :, None]
    first = jnp.min(jnp.where(bound >= -SKIP_LOG2, j, nk), axis=1)
    return first.T.reshape(-1).astype(jnp.int32)


def _moba_prep_kernel(q_ref, k_ref, v_ref, qa_ref, ka_ref, va_ref, kmean_ref):
    i = pl.program_id(0)
    t = q_ref.shape[0]
    nbl = kmean_ref.shape[0]

    @pl.when(i == 0)
    def _():
        kmean_ref[...] = jnp.zeros_like(kmean_ref)

    q = q_ref[...]
    k = k_ref[...]
    v = v_ref[...]
    km = kmean_ref[...]
    k_aug = jnp.where(_iota((t, HEAD_DIM), 1) == i, 1.0, 0.0).astype(BF16)
    blk = _iota((HEAD_DIM, t), 0)
    for h in range(N_HEADS):
        sl = slice(h * HEAD_DIM, (h + 1) * HEAD_DIM)
        qt = _transpose_bf16(q[:, sl])
        kh_hi, kh_lo = _split2(km[:HEAD_DIM, sl])
        gate = _dot(kh_hi, qt) + _dot(kh_lo, qt)
        g = jnp.where(blk < i, gate, NEG)
        bias = jnp.where(blk == i, 0.0, UNSELECTED)
        for _ in range(MOBA_TOPK):
            mx = jnp.max(g, axis=0, keepdims=True)
            idx = jnp.min(jnp.where(g == mx, blk, HEAD_DIM), axis=0, keepdims=True)
            pick = jnp.logical_and(blk == idx, mx > 0.5 * NEG)
            bias = jnp.where(pick, 0.0, bias)
            g = jnp.where(blk == idx, NEG, g)
        qa_ref[h] = jnp.concatenate([qt, bias.astype(BF16)], axis=0)
        ka_ref[h] = jnp.concatenate([k[:, sl], k_aug], axis=1)
        va_ref[h] = _values_t(v[:, sl])
    kmean = jnp.mean(k.astype(F32), axis=0, keepdims=True)
    kmean_ref[...] = jnp.where(_iota((nbl, BRANCH_WIDTH), 0) == i, kmean, km)


def _moba_prep(p):
    s = p.shape[0]
    t = MOBA_BLOCK
    assert s % t == 0 and s // t <= HEAD_DIM, "block one-hot must fit the 64 augmentation lanes"
    cb = P_MOBA // BRANCH_WIDTH
    aug, aug_spec, aug_t, aug_t_spec = _aug_specs(s, t)
    return pl.pallas_call(
        _moba_prep_kernel,
        grid=(s // t,),
        in_specs=[pl.BlockSpec((t, BRANCH_WIDTH), lambda i: (i, cb)),
                  pl.BlockSpec((t, BRANCH_WIDTH), lambda i: (i, cb + 1)),
                  pl.BlockSpec((t, BRANCH_WIDTH), lambda i: (i, cb + 2))],
        out_specs=[aug_t_spec, aug_spec, aug_t_spec],
        out_shape=[aug_t, aug, aug_t],
        scratch_shapes=[pltpu.VMEM((HEAD_DIM, BRANCH_WIDTH), F32)],
        compiler_params=_params(("arbitrary",)),
    )(p, p, p)


def _flash_kernel(first_ref, qt_ref, k_ref, vt_ref, o_ref, s0_ref, s1_ref, *, tq, tk):
    h = pl.program_id(0)
    i = pl.program_id(1)
    qt = qt_ref[0]
    n_diag = tq // tk
    n_full = i * n_diag

    def scores(j):
        return _dot(k_ref[0, pl.ds(pl.multiple_of(j * tk, tk), tk), :], qt)

    def absorb(j, s, m, acc, masked):
        if masked:
            key = _iota((tk, tq), 0) + j * tk
            qry = _iota((tk, tq), 1) + i * tq
            s = jnp.where(key <= qry, s, NEG)
        m_new = jnp.maximum(m, jnp.max(s, axis=0, keepdims=True))
        p = jnp.exp2(s - m_new).astype(BF16)
        vt = vt_ref[0, :, pl.ds(pl.multiple_of(j * tk, tk), tk)]
        return m_new, jnp.exp2(m - m_new) * acc + _dot(vt, p)

    j0 = jnp.minimum(first_ref[h * pl.num_programs(1) + i], n_full)
    m = jnp.full((1, tq), NEG, F32)
    acc = jnp.zeros((LANES, tq), F32)

    odd = (n_full - j0) % 2

    def plain(j, carry):
        return absorb(j, scores(j), *carry, False)

    m, acc = lax.fori_loop(j0, j0 + odd, plain, (m, acc))
    j0 = j0 + odd

    s0_ref[...] = scores(j0)

    def pair(g, carry):
        j = j0 + 2 * g
        s1_ref[...] = scores(j + 1)
        carry = absorb(j, s0_ref[...], *carry, False)
        s0_ref[...] = scores(j + 2)
        return absorb(j + 1, s1_ref[...], *carry, False)

    m, acc = lax.fori_loop(0, (n_full - j0) // 2, pair, (m, acc))
    s = s0_ref[...]
    for d in range(n_diag):
        s_next = scores(n_full + d + 1) if d + 1 < n_diag else None
        m, acc = absorb(n_full + d, s, m, acc, True)
        s = s_next
    o_ref[0] = (acc / acc[HEAD_DIM:HEAD_DIM + 1, :]).T.astype(o_ref.dtype)


def _flash(first_tile, qt, ka, vt, tq, tk):
    nh, s, _ = ka.shape
    return pl.pallas_call(
        functools.partial(_flash_kernel, tq=tq, tk=tk),
        grid_spec=pltpu.PrefetchScalarGridSpec(
            num_scalar_prefetch=1,
            grid=(nh, s // tq),
            in_specs=[pl.BlockSpec((1, LANES, tq), lambda h, i, f: (h, 0, i)),
                      pl.BlockSpec((1, s, LANES), lambda h, i, f: (h, 0, 0)),
                      pl.BlockSpec((1, LANES, s), lambda h, i, f: (h, 0, 0))],
            out_specs=pl.BlockSpec((1, tq, LANES), lambda h, i, f: (h, i, 0)),
            scratch_shapes=[pltpu.VMEM((tk, tq), F32), pltpu.VMEM((tk, tq), F32)]),
        out_shape=jax.ShapeDtypeStruct((nh, s, LANES), BF16),
        compiler_params=_params(("parallel", "parallel")),
    )(first_tile, qt, ka, vt)


def _flash_tiles(s):
    tq = min(1024, s)
    return tq, min(512, tq)


GLA_SUB = 16
GLA_UNROLL = 2


def _gla_kernel(q_ref, k_ref, v_ref, r_ref, t_ref, aup_ref, ab_ref, g_ref, y_ref,
                st_ref, b_ref, o_ref):
    t = q_ref.shape[0]
    c = GLA_SUB

    @pl.when(pl.program_id(0) == 0)
    def _():
        st_ref[...] = jnp.zeros_like(st_ref)

    t_hi, t_lo = _split2(t_ref[...])
    a_hi, a_lo = _split2(aup_ref[...])
    z = _dot(t_hi, a_hi) + _dot(t_lo, a_hi) + _dot(t_hi, a_lo) + ab_ref[...]
    log_a = _log_sigmoid(z) * (1.0 / GLA_TAU)
    row = _iota((t, t), 0)
    col = _iota((t, t), 1)
    tri = jnp.logical_and(row // c == col // c, row >= col).astype(BF16)
    hi, mid, lo = _split3(log_a)
    b_ref[...] = _dot(tri, hi) + _dot(tri, mid) + _dot(tri, lo)

    expand = (_iota((GLA_KW, BRANCH_WIDTH), 0) // GLA_DK
              == _iota((GLA_KW, BRANCH_WIDTH), 1) // HEAD_DIM).astype(BF16)
    st_mask = (_iota((BRANCH_WIDTH, GLA_KW), 0) // HEAD_DIM
               == _iota((BRANCH_WIDTH, GLA_KW), 1) // GLA_DK).astype(F32)
    srow = _iota((c, GLA_KW), 0)
    orow = _iota((c, BRANCH_WIDTH), 0)
    scale = GLA_DK ** -0.5

    def step(r0, st):
        qs = q_ref[pl.ds(r0, c), :].astype(F32) * scale
        ks = k_ref[pl.ds(r0, c), :].astype(F32)
        vb = v_ref[pl.ds(r0, c), :]
        vf = vb.astype(F32)
        bs = b_ref[pl.ds(r0, c), :]
        b_last = bs[c - 1:c, :]

        o_inter = _dot_nt((qs * jnp.exp(bs)).astype(BF16), st.astype(BF16))

        pieces = []
        for tt in range(c):
            e = jnp.exp(jnp.where(srow <= tt, bs[tt:tt + 1, :] - bs, NEG))
            pieces.append(e * qs[tt:tt + 1, :] * ks)
        pm = jnp.concatenate(pieces, axis=0).astype(BF16)
        a = _dot(pm, expand)
        o_intra = jnp.zeros((c, BRANCH_WIDTH), F32)
        for tt in range(c):
            o_t = jnp.sum(a[tt * c:(tt + 1) * c, :] * vf, axis=0, keepdims=True)
            o_intra = jnp.where(orow == tt, o_t, o_intra)
        o_ref[pl.ds(r0, c), :] = o_inter + o_intra

        ke = (ks * jnp.exp(b_last - bs)).astype(BF16)
        return st * jnp.exp(b_last) + _dot_tn(vb, ke) * st_mask

    def steps(g, _):
        st = st_ref[...]
        for u in range(GLA_UNROLL):
            st = step(pl.multiple_of((g * GLA_UNROLL + u) * c, c), st)
        st_ref[...] = st
        return 0

    lax.fori_loop(0, t // (c * GLA_UNROLL), steps, 0)

    o = o_ref[...]
    ones_bd = (_iota((BRANCH_WIDTH, BRANCH_WIDTH), 0) // HEAD_DIM
               == _iota((BRANCH_WIDTH, BRANCH_WIDTH), 1) // HEAD_DIM).astype(BF16)
    sq_hi, sq_mid, sq_lo = _split3(o * o)
    ms = (_dot(sq_hi, ones_bd) + _dot(sq_mid, ones_bd) + _dot(sq_lo, ones_bd)) * (1.0 / HEAD_DIM)
    gr = r_ref[...].astype(F32)
    y = o * lax.rsqrt(ms + RMS_EPS) * g_ref[...] * (gr * _sigmoid(gr))
    y_ref[...] = y.astype(y_ref.dtype)


def _gla(p, tail, aup_pad, ab_row, g_row, t=512):
    s = p.shape[0]
    cq = P_GLA // GLA_KW
    cv = (P_GLA + 2 * GLA_KW) // BRANCH_WIDTH
    return pl.pallas_call(
        _gla_kernel,
        grid=(s // t,),
        in_specs=[pl.BlockSpec((t, GLA_KW), lambda i: (i, cq)),
                  pl.BlockSpec((t, GLA_KW), lambda i: (i, cq + 1)),
                  pl.BlockSpec((t, BRANCH_WIDTH), lambda i: (i, cv)),
                  pl.BlockSpec((t, BRANCH_WIDTH), lambda i: (i, cv + 1)),
                  pl.BlockSpec((t, LANES), lambda i: (i, 0)),
                  pl.BlockSpec((LANES, GLA_KW), lambda i: (0, 0)),
                  pl.BlockSpec((1, GLA_KW), lambda i: (0, 0)),
                  pl.BlockSpec((1, BRANCH_WIDTH), lambda i: (0, 0))],
        out_specs=pl.BlockSpec((t, BRANCH_WIDTH), lambda i: (i, 0)),
        out_shape=jax.ShapeDtypeStruct((s, BRANCH_WIDTH), BF16),
        scratch_shapes=[pltpu.VMEM((BRANCH_WIDTH, GLA_KW), F32),
                        pltpu.VMEM((t, GLA_KW), F32),
                        pltpu.VMEM((t, BRANCH_WIDTH), F32)],
        compiler_params=_params(("arbitrary",)),
    )(p, p, p, p, tail, aup_pad, ab_row, g_row)


def _layer_norm(z, g, b):
    mu = jnp.mean(z, axis=1, keepdims=True)
    zc = z - mu
    var = jnp.mean(zc * zc, axis=1, keepdims=True)
    return zc * lax.rsqrt(var + LN_EPS) * g + b


def _store_row_tiles(ref, val):
    n = val.shape[0]
    for a in range(ROW_TILE):
        ref[pl.ds(a, n, stride=ROW_TILE), :] = val[:, a * LANES:(a + 1) * LANES]


def _load_row_tiles(ref, n, base=0):
    return jnp.concatenate(
        [ref[pl.ds(base + a, n, stride=ROW_TILE), :] for a in range(ROW_TILE)], axis=1)


def _merge_kernel(fox_ref, gla_ref, moba_ref, g0_ref, g1_ref, g2_ref, x_ref, wb_ref, wo_ref,
                  lg_ref, lb_ref, wrh_ref, wrl_ref, br_ref,
                  x1_ref, sel_ref, idx_ref, gw_ref):
    tm = x_ref.shape[0]

    def heads(ref):
        return jnp.concatenate([ref[h][:, :HEAD_DIM] for h in range(N_HEADS)], axis=1)

    merged = _sigmoid(g0_ref[...].astype(F32)) * _dot(heads(fox_ref), wb_ref[0])
    merged += _sigmoid(g1_ref[...].astype(F32)) * _dot(gla_ref[...], wb_ref[1])
    merged += _sigmoid(g2_ref[...].astype(F32)) * _dot(heads(moba_ref), wb_ref[2])
    z = ALPHA * x_ref[...] + _dot(merged.astype(BF16), wo_ref[...])
    x1 = _layer_norm(z, lg_ref[...], lb_ref[...])
    _store_row_tiles(x1_ref, x1)

    x_hi, x_lo = _split2(x1)
    logits = (_dot(x_hi, wrh_ref[...]) + _dot(x_lo, wrh_ref[...]) + _dot(x_hi, wrl_ref[...])
              + br_ref[...])
    lane = _iota((tm, LANES), 1)
    lg = jnp.where(lane < N_EXPERTS, logits, NEG)
    sel = jnp.zeros((tm, LANES), F32)
    idxf = jnp.zeros((tm, LANES), F32)
    ew = jnp.zeros((tm, LANES), F32)
    top = None
    for r in range(TOP_K):
        mx = jnp.max(lg, axis=1, keepdims=True)
        idx = jnp.min(jnp.where(lg == mx, lane, LANES), axis=1, keepdims=True)
        hit = lane == idx
        top = mx if top is None else top
        sel = jnp.where(hit, 1.0, sel)
        idxf = jnp.where(lane == r, idx.astype(F32), idxf)
        ew = jnp.where(lane == r, jnp.exp(mx - top), ew)
        lg = jnp.where(hit, NEG, lg)
    sel_ref[...] = sel
    idx_ref[...] = idxf
    gw_ref[...] = ew / jnp.sum(ew, axis=1, keepdims=True)


def _merge(fox_o, y_gla, moba_o, p, x, wb, wo, lg, lb, wrh, wrl, br, tm=256):
    s = x.shape[0]
    head_spec = pl.BlockSpec((N_HEADS, tm, LANES), lambda i: (0, i, 0))
    row = lambda n: pl.BlockSpec((1, n), lambda i: (0, 0))
    small = jax.ShapeDtypeStruct((s, LANES), F32)
    small_spec = pl.BlockSpec((tm, LANES), lambda i: (i, 0))
    return pl.pallas_call(
        _merge_kernel,
        grid=(s // tm,),
        in_specs=[head_spec,
                  pl.BlockSpec((tm, BRANCH_WIDTH), lambda i: (i, 0)),
                  head_spec,
                  pl.BlockSpec((tm, D_MODEL), lambda i: (i, 0)),
                  pl.BlockSpec((tm, D_MODEL), lambda i: (i, 1)),
                  pl.BlockSpec((tm, D_MODEL), lambda i: (i, 2)),
                  pl.BlockSpec((tm, D_MODEL), lambda i: (i, 0)),
                  pl.BlockSpec((3, BRANCH_WIDTH, D_MODEL), lambda i: (0, 0, 0)),
                  pl.BlockSpec((D_MODEL, D_MODEL), lambda i: (0, 0)),
                  row(D_MODEL), row(D_MODEL),
                  pl.BlockSpec((D_MODEL, LANES), lambda i: (0, 0)),
                  pl.BlockSpec((D_MODEL, LANES), lambda i: (0, 0)),
                  row(LANES)],
        out_specs=[pl.BlockSpec((tm * ROW_TILE, LANES), lambda i: (i, 0)),
                   small_spec, small_spec, small_spec],
        out_shape=[jax.ShapeDtypeStruct((s * ROW_TILE, LANES), F32), small, small, small],
        compiler_params=_params(("parallel",)),
    )(fox_o, y_gla, moba_o, p, p, p, x, wb, wo, lg, lb, wrh, wrl, br)


def _rank_kernel(sel_ref, idx_ref, dest_ref, cnt_ref, carry_ref, start_ref):
    phase = pl.program_id(0)
    i = pl.program_id(1)
    t = sel_ref.shape[0]
    sel = sel_ref[...]
    chosen = jnp.sum(sel, axis=0, keepdims=True)

    @pl.when(jnp.logical_and(phase == 0, i == 0))
    def _():
        carry_ref[...] = jnp.zeros_like(carry_ref)

    @pl.when(jnp.logical_and(phase == 1, i == 0))
    def _():
        counts = carry_ref[...]
        cnt_ref[...] = counts
        blocks = jnp.floor((counts + (EXPERT_BLOCK - 1)) * (1.0 / EXPERT_BLOCK))
        before = (_iota((LANES, LANES), 0) < _iota((LANES, LANES), 1)).astype(BF16)
        first = _dot(jnp.broadcast_to(blocks, (ROW_TILE, LANES)).astype(BF16), before)
        start_ref[...] = first[0:1, :] * float(EXPERT_BLOCK)
        carry_ref[...] = jnp.zeros_like(carry_ref)

    @pl.when(phase == 1)
    def _():
        stri = (_iota((t, t), 0) > _iota((t, t), 1)).astype(BF16)
        pos = _dot(stri, sel.astype(BF16)) + carry_ref[...] + start_ref[...]
        lane = _iota((t, LANES), 1).astype(F32)
        idxf = idx_ref[...]
        dest = jnp.zeros((t, LANES), F32)
        for r in range(TOP_K):
            mine = jnp.sum(jnp.where(lane == idxf[:, r:r + 1], pos, 0.0), axis=1, keepdims=True)
            dest = jnp.where(lane == float(r), mine, dest)
        dest_ref[...] = dest.T[:ROW_TILE, :].astype(jnp.int32)

    carry_ref[...] += chosen


def _rank(sel, idxf, t=512):
    s = sel.shape[0]
    spec = pl.BlockSpec((t, LANES), lambda ph, i: (i, 0))
    return pl.pallas_call(
        _rank_kernel,
        grid=(2, s // t),
        in_specs=[spec, spec],
        out_specs=[pl.BlockSpec((ROW_TILE, t), lambda ph, i: (0, i * ph)),
                   pl.BlockSpec((1, LANES), lambda ph, i: (0, 0))],
        out_shape=[jax.ShapeDtypeStruct((ROW_TILE, s), jnp.int32),
                   jax.ShapeDtypeStruct((1, LANES), F32)],
        scratch_shapes=[pltpu.VMEM((1, LANES), F32), pltpu.VMEM((1, LANES), F32)],
        compiler_params=_params(("arbitrary", "arbitrary")),
    )(sel, idxf)


DMA_UNROLL = 4


def _token_rows(ref, token):
    return ref.at[pl.ds(pl.multiple_of(token * ROW_TILE, ROW_TILE), ROW_TILE), :]


def _dispatch_kernel(dest_ref, x_ref, xin_ref, sem):
    n = dest_ref.shape[1]

    def issue(g, _):
        for u in range(DMA_UNROLL):
            t = g * DMA_UNROLL + u
            for k in range(TOP_K):
                pltpu.make_async_copy(_token_rows(x_ref, t), _token_rows(xin_ref, dest_ref[k, t]),
                                      sem).start()
        return 0

    lax.fori_loop(0, n // DMA_UNROLL, issue, 0)
    for k in range(TOP_K):
        pltpu.make_async_copy(x_ref, xin_ref.at[pl.ds(0, n * ROW_TILE), :], sem).wait()


def _dispatch(dest, x1_2d, n_rows, td=512):
    s = x1_2d.shape[0] // ROW_TILE
    return pl.pallas_call(
        _dispatch_kernel,
        grid=(s // td,),
        in_specs=[pl.BlockSpec((ROW_TILE, td), lambda i: (0, i), memory_space=pltpu.SMEM),
                  pl.BlockSpec((td * ROW_TILE, LANES), lambda i: (i, 0))],
        out_specs=pl.BlockSpec(memory_space=pl.ANY),
        out_shape=jax.ShapeDtypeStruct((n_rows * ROW_TILE, LANES), F32),
        scratch_shapes=[pltpu.SemaphoreType.DMA(())],
        compiler_params=pltpu.CompilerParams(dimension_semantics=("arbitrary",),
                                             has_side_effects=True),
    )(dest, x1_2d)


HALF = LANES // 2
W_CAST_ROWS = 64


def _expert_kernel(be_ref, nv_ref, xin_ref, wgu_ref, bgu_ref, wdn_ref, bdn_ref, y_ref,
                   wgu_s, wdn_s, perm_s):
    b = pl.program_id(0)
    blk = y_ref.shape[0] // ROW_TILE
    nvalid = nv_ref[b]
    new_expert = jnp.logical_or(b == 0, be_ref[b] != be_ref[jnp.maximum(b - 1, 0)])

    @pl.when(jnp.logical_and(new_expert, nvalid > 0))
    def _():
        def cast(r, _):
            r0 = pl.multiple_of(r * W_CAST_ROWS, W_CAST_ROWS)
            wgu_s[pl.ds(r0, W_CAST_ROWS), :] = wgu_ref[0, pl.ds(r0, W_CAST_ROWS), :].astype(BF16)
            return 0

        lax.fori_loop(0, D_MODEL // W_CAST_ROWS, cast, 0)
        for c in range(D_MODEL // LANES):
            cols = slice(c * LANES, (c + 1) * LANES)
            for g in range(D_MODEL // LANES):
                lo = g * LANES
                perm_s[c, pl.ds(lo, HALF, stride=2), :] = wdn_ref[0, lo:lo + HALF, cols]
                perm_s[c, pl.ds(lo + 1, HALF, stride=2), :] = wdn_ref[0, lo + HALF:lo + LANES, cols]
            wdn_s[:, cols] = perm_s[c].astype(BF16)

    @pl.when(nvalid > 0)
    def _():
        x = _load_row_tiles(xin_ref, blk)
        x = jnp.where(_iota((blk, D_MODEL), 0) < nvalid, x, 0.0).astype(BF16)
        even = _iota((blk, LANES), 1) % 2 == 0
        acts = []
        for g in range(D_MODEL // LANES):
            lo = g * 2 * LANES
            h = _dot(x, wgu_s[:, lo:lo + 2 * LANES]) + bgu_ref[0, :, lo:lo + 2 * LANES]
            h_a = h[:, :LANES]
            h_b = h[:, LANES:]
            gate = jnp.where(even, h_a, pltpu.roll(h_b, 1, 1))
            up = jnp.where(even, pltpu.roll(h_a, LANES - 1, 1), h_b)
            gate = jnp.minimum(gate, SWIGLU_LIMIT)
            up = jnp.clip(up, -SWIGLU_LIMIT, SWIGLU_LIMIT)
            acts.append(((up + 1.0) * (gate * _sigmoid(SWIGLU_ALPHA * gate))).astype(BF16))
        act = jnp.concatenate(acts, axis=1)
        _store_row_tiles(y_ref, _dot(act, wdn_s[...]) + bdn_ref[0])

    @pl.when(nvalid <= 0)
    def _():
        y_ref[...] = jnp.zeros_like(y_ref)


def _experts(block_expert, block_valid, xin2d, w_gu, b_gu, w_dn, b_dn, layer):
    blk = EXPERT_BLOCK
    nb = block_expert.shape[0]
    x_spec = pl.BlockSpec((blk * ROW_TILE, LANES), lambda b, be, nv: (b, 0))
    which = lambda b, be, nv: (layer * N_EXPERTS + be[b], 0, 0)
    return pl.pallas_call(
        _expert_kernel,
        grid_spec=pltpu.PrefetchScalarGridSpec(
            num_scalar_prefetch=2,
            grid=(nb,),
            in_specs=[x_spec,
                      pl.BlockSpec((1, D_MODEL, 2 * D_MODEL), which),
                      pl.BlockSpec((1, 1, 2 * D_MODEL), which),
                      pl.BlockSpec((1, D_MODEL, D_MODEL), which),
                      pl.BlockSpec((1, 1, D_MODEL), which)],
            out_specs=x_spec,
            scratch_shapes=[pltpu.VMEM((D_MODEL, 2 * D_MODEL), BF16),
                            pltpu.VMEM((D_MODEL, D_MODEL), BF16),
                            pltpu.VMEM((D_MODEL // LANES, D_MODEL, LANES), F32)]),
        out_shape=jax.ShapeDtypeStruct((nb * blk * ROW_TILE, LANES), F32),
        compiler_params=pltpu.CompilerParams(dimension_semantics=("arbitrary",),
                                             vmem_limit_bytes=56 * 1024 * 1024),
    )(block_expert, block_valid, xin2d, w_gu, b_gu, w_dn, b_dn)


def _combine_kernel(dest_ref, next_ref, yb_ref, gw_ref, x1_ref, lg_ref, lb_ref, out_ref,
                    buf_ref, sem):
    i = pl.program_id(0)
    tc = out_ref.shape[0]
    slot = i % 2

    def gather(rows_ref, into):
        def body(g, _):
            for u in range(DMA_UNROLL):
                t = g * DMA_UNROLL + u
                for k in range(TOP_K):
                    pltpu.make_async_copy(_token_rows(yb_ref, rows_ref[k, t]),
                                          _token_rows(buf_ref.at[into, k], t), sem.at[into]).start()
            return 0

        lax.fori_loop(0, tc // DMA_UNROLL, body, 0)

    @pl.when(i == 0)
    def _():
        gather(dest_ref, 0)

    @pl.when(i + 1 < pl.num_programs(0))
    def _():
        gather(next_ref, 1 - slot)

    for k in range(TOP_K):
        pltpu.make_async_copy(yb_ref.at[pl.ds(0, tc * ROW_TILE), :], buf_ref.at[slot, k],
                              sem.at[slot]).wait()

    gw = gw_ref[...]
    f = jnp.zeros((tc, D_MODEL), F32)
    for k in range(TOP_K):
        f += gw[:, k:k + 1] * _load_row_tiles(buf_ref.at[slot, k], tc)
    z = ALPHA * _load_row_tiles(x1_ref, tc) + f
    out_ref[...] = _layer_norm(z, lg_ref[...], lb_ref[...])


def _combine(dest, yb_rows, gw, x1_2d, lg, lb, tc=256):
    s = gw.shape[0]
    n = s // tc
    row = pl.BlockSpec((1, D_MODEL), lambda i: (0, 0))
    return pl.pallas_call(
        _combine_kernel,
        grid=(n,),
        in_specs=[pl.BlockSpec((ROW_TILE, tc), lambda i: (0, i), memory_space=pltpu.SMEM),
                  pl.BlockSpec((ROW_TILE, tc), lambda i: (0, jnp.minimum(i + 1, n - 1)),
                               memory_space=pltpu.SMEM),
                  pl.BlockSpec(memory_space=pl.ANY),
                  pl.BlockSpec((tc, LANES), lambda i: (i, 0)),
                  pl.BlockSpec((tc * ROW_TILE, LANES), lambda i: (i, 0)),
                  row, row],
        out_specs=pl.BlockSpec((tc, D_MODEL), lambda i: (i, 0)),
        out_shape=jax.ShapeDtypeStruct((s, D_MODEL), F32),
        scratch_shapes=[pltpu.VMEM((2, TOP_K, tc * ROW_TILE, LANES), F32),
                        pltpu.SemaphoreType.DMA((2,))],
        compiler_params=_params(("arbitrary",)),
    )(dest, dest, yb_rows, gw, x1_2d, lg, lb)


W_IN_SIZES = (256, 256, 256, 4, 128, 128, 256, 16, 256, 256, 256, 256, 3 * D_MODEL)
W_IN_WIDTH = sum(W_IN_SIZES)


def _w_in_kernel(w_ref, main_ref, tail_ref):
    w = w_ref[0]
    offs = [0]
    for n in W_IN_SIZES:
        offs.append(offs[-1] + n)
    fq, fk, fv, ff, gq, gk, gv, ga, gr, mq, mk, mv, gates = (
        w[:, offs[i]:offs[i + 1]] for i in range(len(W_IN_SIZES)))
    main_ref[...] = jnp.concatenate(
        [gates, fq, fk, fv, mq, mk, mv, gq, gk, gv, gr], axis=1).astype(BF16)
    pad = jnp.zeros((w.shape[0], LANES - 4 - GLA_RANK), F32)
    tail_ref[...] = jnp.concatenate([ff, ga, pad], axis=1).astype(BF16)


def _permute_w_in(w_in, layer, tr=128):
    return pl.pallas_call(
        _w_in_kernel,
        grid=(D_MODEL // tr,),
        in_specs=[pl.BlockSpec((1, tr, W_IN_WIDTH), lambda i: (layer, i, 0))],
        out_specs=[pl.BlockSpec((tr, P_WIDTH), lambda i: (i, 0)),
                   pl.BlockSpec((tr, LANES), lambda i: (i, 0))],
        out_shape=[jax.ShapeDtypeStruct((D_MODEL, P_WIDTH), BF16),
                   jax.ShapeDtypeStruct((D_MODEL, LANES), BF16)],
        compiler_params=_params(("parallel",)),
    )(w_in)


def _pad_row(v, n=LANES, fill=0.0):
    return jnp.pad(v.astype(F32), (0, n - v.shape[0]), constant_values=fill)[None, :]


def _mixer_layer(x, w_in, layer, fox_fb, gla_a_up, gla_a_b, gla_norm_g, w_branch, w_out, ln_g, ln_b,
                 w_router, b_router):
    w_main, w_tail = _permute_w_in(w_in, layer)
    p, tail = _inproj(x, w_main, w_tail)

    s = x.shape[0]
    tq, tk = _flash_tiles(s)
    fox_qt, fox_k, fox_vt, fox_stats = _fox_prep(p, tail, _pad_row(fox_fb))
    fox_o = _flash(_fox_first_tile(fox_stats, s, tq, tk), fox_qt, fox_k, fox_vt, tq, tk)
    moba_o = _flash(jnp.zeros((N_HEADS * (s // tq),), jnp.int32), *_moba_prep(p), tq, tk)
    aup_pad = jnp.zeros((LANES, GLA_KW), F32).at[TAIL_GA:TAIL_GA + GLA_RANK].set(gla_a_up)
    y_gla = _gla(p, tail, aup_pad, gla_a_b[None, :], gla_norm_g[None, :])

    wr = jnp.pad(w_router, ((0, 0), (0, LANES - N_EXPERTS)))
    wr_hi = wr.astype(BF16)
    wr_lo = (wr - wr_hi.astype(F32)).astype(BF16)
    return _merge(fox_o, y_gla, moba_o, p, x, w_branch.astype(BF16), w_out.astype(BF16),
                  ln_g[None, :], ln_b[None, :], wr_hi, wr_lo, _pad_row(b_router))


def _moe_layer(x1_2d, sel, idxf, gw, w_gu, b_gu, w_dn, b_dn, ln_g, ln_b, layer):
    s = sel.shape[0]
    blk = EXPERT_BLOCK
    dest, counts = _rank(sel, idxf)

    counts = counts[0, :N_EXPERTS].astype(jnp.int32)
    padded = (counts + blk - 1) // blk * blk
    pad_end = jnp.cumsum(padded)
    pad_start = pad_end - padded
    n_blocks = (s * TOP_K + N_EXPERTS * (blk - 1)) // blk + 1
    block_start = jnp.arange(n_blocks, dtype=jnp.int32) * blk
    active = block_start < pad_end[-1]
    block_expert = jnp.sum((block_start[:, None] >= pad_end[None, :]).astype(jnp.int32), axis=1)
    last_expert = jnp.max(jnp.where(counts > 0, jnp.arange(N_EXPERTS, dtype=jnp.int32), 0))
    block_expert = jnp.where(active, block_expert, last_expert)
    block_valid = jnp.clip(pad_start[block_expert] + counts[block_expert] - block_start, 0, blk)
    block_valid = jnp.where(active, block_valid, 0).astype(jnp.int32)

    xin = _dispatch(dest, x1_2d, n_blocks * blk)
    yb = _experts(block_expert, block_valid, xin, w_gu, b_gu, w_dn, b_dn, layer)
    return _combine(dest, yb, gw, x1_2d, ln_g[None, :], ln_b[None, :])


def kernel(x, w_in, fox_fb, gla_a_up, gla_a_b, gla_norm_g, w_branch, w_out, ln1_g, ln1_b,
           w_router, b_router, w_gu, b_gu, w_dn, b_dn, ln2_g, ln2_b):
    b, s, d = x.shape
    assert b == 1 and d == D_MODEL
    h = x.reshape(s, d)
    n_le = DEPTH * N_EXPERTS
    w_gu = w_gu.reshape(n_le, D_MODEL, 2 * D_MODEL)
    b_gu = b_gu.reshape(n_le, 1, 2 * D_MODEL)
    w_dn = w_dn.reshape(n_le, D_MODEL, D_MODEL)
    b_dn = b_dn.reshape(n_le, 1, D_MODEL)
    for l in range(DEPTH):
        x1_2d, sel, idxf, gw = _mixer_layer(
            h, w_in, l, fox_fb[l], gla_a_up[l], gla_a_b[l], gla_norm_g[l], w_branch[l], w_out[l],
            ln1_g[l], ln1_b[l], w_router[l], b_router[l])
        h = _moe_layer(x1_2d, sel, idxf, gw, w_gu, b_gu, w_dn, b_dn, ln2_g[l], ln2_b[l], l)
    return h.reshape(b, s, d)
```

```python
import functools

import jax
import jax.numpy as jnp
from jax import lax
from jax.experimental import pallas as pl
from jax.experimental.pallas import tpu as pltpu

F32 = jnp.float32
BF16 = jnp.bfloat16

D_MODEL = 1024
DEPTH = 4
N_HEADS = 4
HEAD_DIM = 64
GLA_DK = 32
GLA_KW = N_HEADS * GLA_DK
GLA_RANK = 16
GLA_TAU = 16.0
MOBA_BLOCK = 256
MOBA_TOPK = 3
BRANCH_WIDTH = 256
N_EXPERTS = 32
TOP_K = 4
SWIGLU_LIMIT = 7.0
SWIGLU_ALPHA = 1.702
ALPHA = (2 * DEPTH) ** 0.25
LN_EPS = 1e-5
RMS_EPS = 1e-6
LOG2E = 1.4426950408889634

LANES = 128
ROW_TILE = 8
NEG = -1e30
UNSELECTED = -32768.0
VMEM_LIMIT = 48 * 1024 * 1024

P_GATES = 0
P_FOX = 3 * D_MODEL
P_MOBA = P_FOX + 3 * BRANCH_WIDTH
P_GLA = P_MOBA + 3 * BRANCH_WIDTH
P_WIDTH = P_GLA + 2 * GLA_KW + 2 * BRANCH_WIDTH
TAIL_FF = 0
TAIL_GA = 4

EXPERT_BLOCK = 512


def _params(sem):
    return pltpu.CompilerParams(dimension_semantics=sem, vmem_limit_bytes=VMEM_LIMIT)


def _split3(x):
    hi = x.astype(BF16)
    r1 = x - hi.astype(F32)
    mid = r1.astype(BF16)
    lo = (r1 - mid.astype(F32)).astype(BF16)
    return hi, mid, lo


def _split2(x):
    hi = x.astype(BF16)
    lo = (x - hi.astype(F32)).astype(BF16)
    return hi, lo


def _dot(a, b):
    return jnp.dot(a, b, preferred_element_type=F32)


def _dot_nt(a, b):
    return lax.dot_general(a, b, (((1,), (1,)), ((), ())), preferred_element_type=F32)


def _dot_tn(a, b):
    return lax.dot_general(a, b, (((0,), (0,)), ((), ())), preferred_element_type=F32)


def _log_sigmoid(t):
    return jnp.minimum(t, 0.0) - jnp.log1p(jnp.exp(-jnp.abs(t)))


def _sigmoid(t):
    return 0.5 * jnp.tanh(0.5 * t) + 0.5


def _iota(shape, axis):
    return lax.broadcasted_iota(jnp.int32, shape, axis)


def _transpose_bf16(x):
    return x.astype(F32).T.astype(BF16)


BF16_ROWS = 16
V_ROWS = HEAD_DIM + BF16_ROWS


def _values_t(v):
    ones_row = jnp.where(_iota((BF16_ROWS, v.shape[0]), 0) == 0, 1.0, 0.0).astype(BF16)
    return jnp.concatenate([_transpose_bf16(v), ones_row], axis=0)


def _aug_specs(s, t):
    aug = jax.ShapeDtypeStruct((N_HEADS, s, LANES), BF16)
    aug_spec = pl.BlockSpec((N_HEADS, t, LANES), lambda i: (0, i, 0))
    aug_t = jax.ShapeDtypeStruct((N_HEADS, LANES, s), BF16)
    aug_t_spec = pl.BlockSpec((N_HEADS, LANES, t), lambda i: (0, 0, i))
    aug_v = jax.ShapeDtypeStruct((N_HEADS, V_ROWS, s), BF16)
    aug_v_spec = pl.BlockSpec((N_HEADS, V_ROWS, t), lambda i: (0, 0, i))
    return aug, aug_spec, aug_t, aug_t_spec, aug_v, aug_v_spec


def _inproj_kernel(x_ref, w_ref, wt_ref, sc_ref, p_ref, t_ref, xb_ref):
    @pl.when(pl.program_id(1) == 0)
    def _():
        xb_ref[...] = x_ref[...].astype(BF16)
        t_ref[...] = _dot(xb_ref[...], wt_ref[...])

    p_ref[...] = (_dot(xb_ref[...], w_ref[...]) * sc_ref[...]).astype(BF16)


def _inproj(x, w, wt, tm=1024, tn=1792):
    s = x.shape[0]
    col = jnp.arange(P_WIDTH)
    is_q = ((col >= P_FOX) & (col < P_FOX + BRANCH_WIDTH)) | ((col >= P_MOBA) & (col < P_MOBA + BRANCH_WIDTH))
    scale = jnp.where(is_q, HEAD_DIM ** -0.5 * LOG2E, 1.0).astype(F32)[None, :]
    return pl.pallas_call(
        _inproj_kernel,
        grid=(s // tm, P_WIDTH // tn),
        in_specs=[pl.BlockSpec((tm, D_MODEL), lambda i, j: (i, 0)),
                  pl.BlockSpec((D_MODEL, tn), lambda i, j: (0, j)),
                  pl.BlockSpec((D_MODEL, LANES), lambda i, j: (0, 0)),
                  pl.BlockSpec((1, tn), lambda i, j: (0, j))],
        out_specs=[pl.BlockSpec((tm, tn), lambda i, j: (i, j)),
                   pl.BlockSpec((tm, LANES), lambda i, j: (i, 0))],
        out_shape=[jax.ShapeDtypeStruct((s, P_WIDTH), BF16),
                   jax.ShapeDtypeStruct((s, LANES), F32)],
        scratch_shapes=[pltpu.VMEM((tm, D_MODEL), BF16)],
        compiler_params=_params(("parallel", "arbitrary")),
    )(x, w, wt, scale)


def _fox_prep_kernel(q_ref, k_ref, v_ref, t_ref, fb_ref, qa_ref, ka_ref, va_ref, st_ref, carry_ref):
    t = q_ref.shape[0]

    @pl.when(pl.program_id(0) == 0)
    def _():
        carry_ref[...] = jnp.zeros_like(carry_ref)

    ls = _log_sigmoid(t_ref[...] + fb_ref[...])
    tri = (_iota((t, t), 0) >= _iota((t, t), 1)).astype(BF16)
    hi, mid, lo = _split3(ls)
    c = _dot(tri, hi) + _dot(tri, mid) + _dot(tri, lo) + carry_ref[...]
    carry_ref[...] = c[t - 1:t, :]

    c2 = LOG2E * c

    lane = _iota((t, HEAD_DIM), 1)
    q_aug = jnp.where(_iota((HEAD_DIM, t), 0) < 3, 1.0, 0.0).astype(BF16)
    q = q_ref[...]
    k = k_ref[...]
    v = v_ref[...]
    srow = _iota((ROW_TILE, LANES), 0)
    slane = _iota((ROW_TILE, LANES), 1)
    stat = jnp.where(srow == 2, c2[0:1, :], jnp.where(srow == 3, c2[t - 1:t, :], 0.0))
    for h in range(N_HEADS):
        sl = slice(h * HEAD_DIM, (h + 1) * HEAD_DIM)
        nhi, nmid, nlo = (piece.astype(F32) for piece in _split3(-c2[:, h:h + 1]))
        k_aug = jnp.where(lane == 0, nhi, jnp.where(lane == 1, nmid, jnp.where(lane == 2, nlo, 0.0)))
        qa_ref[h] = jnp.concatenate([_transpose_bf16(q[:, sl]), q_aug], axis=0)
        ka_ref[h] = jnp.concatenate([k[:, sl], k_aug.astype(BF16)], axis=1)
        va_ref[h] = _values_t(v[:, sl])
        for r, x in ((0, q), (1, k)):
            xf = x[:, sl].astype(F32)
            norm = jnp.sqrt(jnp.max(jnp.sum(xf * xf, axis=1, keepdims=True), axis=0, keepdims=True))
            stat = jnp.where(jnp.logical_and(srow == r, slane == h), norm, stat)
    st_ref[0] = stat


def _fox_prep(p, tail, fb_row, t=512):
    s = p.shape[0]
    cb = P_FOX // BRANCH_WIDTH
    aug, aug_spec, aug_t, aug_t_spec, aug_v, aug_v_spec = _aug_specs(s, t)
    return pl.pallas_call(
        _fox_prep_kernel,
        grid=(s // t,),
        in_specs=[pl.BlockSpec((t, BRANCH_WIDTH), lambda i: (i, cb)),
                  pl.BlockSpec((t, BRANCH_WIDTH), lambda i: (i, cb + 1)),
                  pl.BlockSpec((t, BRANCH_WIDTH), lambda i: (i, cb + 2)),
                  pl.BlockSpec((t, LANES), lambda i: (i, 0)),
                  pl.BlockSpec((1, LANES), lambda i: (0, 0))],
        out_specs=[aug_t_spec, aug_spec, aug_v_spec,
                   pl.BlockSpec((1, ROW_TILE, LANES), lambda i: (i, 0, 0))],
        out_shape=[aug_t, aug, aug_v, jax.ShapeDtypeStruct((s // t, ROW_TILE, LANES), F32)],
        scratch_shapes=[pltpu.VMEM((1, LANES), F32)],
        compiler_params=_params(("arbitrary",)),
    )(p, p, p, tail, fb_row)


SKIP_LOG2 = 48.0


def _fox_first_tile(stats, s, tq, tk):
    tp = s // stats.shape[0]
    qn, kn, c_first, c_last = (stats[:, r, :N_HEADS] for r in range(4))
    nq, nk = s // tq, s // tk
    qn = jnp.max(qn.reshape(nq, tq // tp, N_HEADS), axis=1)
    c_q = c_first.reshape(nq, tq // tp, N_HEADS)[:, 0, :]
    c_k = c_last.reshape(nk, tk // tp, N_HEADS)[:, -1, :]
    bound = (2.0 * qn * jnp.max(kn, axis=0))[:, None, :] + c_q[:, None, :] - c_k[None, :, :]
    j = jnp.arange(nk, dtype=jnp.int32)[None, :, None]
    first = jnp.min(jnp.where(bound >= -SKIP_LOG2, j, nk), axis=1)
    return first.T.reshape(-1).astype(jnp.int32)


def _moba_prep_kernel(q_ref, k_ref, v_ref, qa_ref, ka_ref, va_ref, kmean_ref):
    i = pl.program_id(0)
    t = q_ref.shape[0]
    nbl = kmean_ref.shape[0]

    @pl.when(i == 0)
    def _():
        kmean_ref[...] = jnp.zeros_like(kmean_ref)

    q = q_ref[...]
    k = k_ref[...]
    v = v_ref[...]
    km = kmean_ref[...]
    k_aug = jnp.where(_iota((t, HEAD_DIM), 1) == i, 1.0, 0.0).astype(BF16)
    blk = _iota((HEAD_DIM, t), 0)
    for h in range(N_HEADS):
        sl = slice(h * HEAD_DIM, (h + 1) * HEAD_DIM)
        qt = _transpose_bf16(q[:, sl])
        kh_hi, kh_lo = _split2(km[:HEAD_DIM, sl])
        gate = _dot(kh_hi, qt) + _dot(kh_lo, qt)
        g = jnp.where(blk < i, gate, NEG)
        bias = jnp.where(blk == i, 0.0, UNSELECTED)
        for _ in range(MOBA_TOPK):
            mx = jnp.max(g, axis=0, keepdims=True)
            idx = jnp.min(jnp.where(g == mx, blk, HEAD_DIM), axis=0, keepdims=True)
            pick = jnp.logical_and(blk == idx, mx > 0.5 * NEG)
            bias = jnp.where(pick, 0.0, bias)
            g = jnp.where(blk == idx, NEG, g)
        qa_ref[h] = jnp.concatenate([qt, bias.astype(BF16)], axis=0)
        ka_ref[h] = jnp.concatenate([k[:, sl], k_aug], axis=1)
        va_ref[h] = _values_t(v[:, sl])
    kmean = jnp.mean(k.astype(F32), axis=0, keepdims=True)
    kmean_ref[...] = jnp.where(_iota((nbl, BRANCH_WIDTH), 0) == i, kmean, km)


def _moba_prep(p):
    s = p.shape[0]
    t = MOBA_BLOCK
    assert s % t == 0 and s // t <= HEAD_DIM, "block one-hot must fit the 64 augmentation lanes"
    cb = P_MOBA // BRANCH_WIDTH
    aug, aug_spec, aug_t, aug_t_spec, aug_v, aug_v_spec = _aug_specs(s, t)
    return pl.pallas_call(
        _moba_prep_kernel,
        grid=(s // t,),
        in_specs=[pl.BlockSpec((t, BRANCH_WIDTH), lambda i: (i, cb)),
                  pl.BlockSpec((t, BRANCH_WIDTH), lambda i: (i, cb + 1)),
                  pl.BlockSpec((t, BRANCH_WIDTH), lambda i: (i, cb + 2))],
        out_specs=[aug_t_spec, aug_spec, aug_v_spec],
        out_shape=[aug_t, aug, aug_v],
        scratch_shapes=[pltpu.VMEM((HEAD_DIM, BRANCH_WIDTH), F32)],
        compiler_params=_params(("arbitrary",)),
    )(p, p, p)


def _flash_kernel(first_ref, qt_ref, k_ref, vt_ref, o_ref, s0_ref, s1_ref, *, tq, tk):
    h = pl.program_id(0)
    i = pl.program_id(1)
    qt = qt_ref[0]
    n_diag = tq // tk
    n_full = i * n_diag

    def scores(j):
        return _dot(k_ref[0, pl.ds(pl.multiple_of(j * tk, tk), tk), :], qt)

    def absorb(j, s, m, acc, masked):
        if masked:
            key = _iota((tk, tq), 0) + j * tk
            qry = _iota((tk, tq), 1) + i * tq
            s = jnp.where(key <= qry, s, NEG)
        m_new = jnp.maximum(m, jnp.max(s, axis=0, keepdims=True))
        p = jnp.exp2(s - m_new).astype(BF16)
        vt = vt_ref[0, :, pl.ds(pl.multiple_of(j * tk, tk), tk)]
        return m_new, jnp.exp2(m - m_new) * acc + _dot(vt, p)

    j0 = jnp.minimum(first_ref[h * pl.num_programs(1) + i], n_full)
    m = jnp.full((1, tq), NEG, F32)
    acc = jnp.zeros((V_ROWS, tq), F32)

    odd = (n_full - j0) % 2

    def plain(j, carry):
        return absorb(j, scores(j), *carry, False)

    m, acc = lax.fori_loop(j0, j0 + odd, plain, (m, acc))
    j0 = j0 + odd

    s0_ref[...] = scores(j0)

    def pair(g, carry):
        j = j0 + 2 * g
        s1_ref[...] = scores(j + 1)
        carry = absorb(j, s0_ref[...], *carry, False)
        s0_ref[...] = scores(j + 2)
        return absorb(j + 1, s1_ref[...], *carry, False)

    m, acc = lax.fori_loop(0, (n_full - j0) // 2, pair, (m, acc))
    s = s0_ref[...]
    for d in range(n_diag):
        s_next = scores(n_full + d + 1) if d + 1 < n_diag else None
        m, acc = absorb(n_full + d, s, m, acc, True)
        s = s_next
    o_ref[0] = (acc[:HEAD_DIM, :] / acc[HEAD_DIM:HEAD_DIM + 1, :]).T.astype(o_ref.dtype)


def _flash(first_tile, qt, ka, vt, tq, tk):
    nh, s, _ = ka.shape
    return pl.pallas_call(
        functools.partial(_flash_kernel, tq=tq, tk=tk),
        grid_spec=pltpu.PrefetchScalarGridSpec(
            num_scalar_prefetch=1,
            grid=(nh, s // tq),
            in_specs=[pl.BlockSpec((1, LANES, tq), lambda h, i, f: (h, 0, i)),
                      pl.BlockSpec((1, s, LANES), lambda h, i, f: (h, 0, 0)),
                      pl.BlockSpec((1, V_ROWS, s), lambda h, i, f: (h, 0, 0))],
            out_specs=pl.BlockSpec((1, tq, HEAD_DIM), lambda h, i, f: (h, i, 0)),
            scratch_shapes=[pltpu.VMEM((tk, tq), F32), pltpu.VMEM((tk, tq), F32)]),
        out_shape=jax.ShapeDtypeStruct((nh, s, HEAD_DIM), BF16),
        compiler_params=_params(("parallel", "parallel")),
    )(first_tile, qt, ka, vt)


def _flash_tiles(s):
    tq = min(1024, s)
    return tq, min(512, tq)


GLA_SUB = 16
GLA_UNROLL = 2


def _gla_kernel(q_ref, k_ref, v_ref, r_ref, t_ref, aup_ref, ab_ref, g_ref, y_ref,
                st_ref, b_ref, o_ref):
    t = q_ref.shape[0]
    c = GLA_SUB

    @pl.when(pl.program_id(0) == 0)
    def _():
        st_ref[...] = jnp.zeros_like(st_ref)

    t_hi, t_lo = _split2(t_ref[...])
    a_hi, a_lo = _split2(aup_ref[...])
    z = _dot(t_hi, a_hi) + _dot(t_lo, a_hi) + _dot(t_hi, a_lo) + ab_ref[...]
    log_a = _log_sigmoid(z) * (1.0 / GLA_TAU)
    row = _iota((t, t), 0)
    col = _iota((t, t), 1)
    tri = jnp.logical_and(row // c == col // c, row >= col).astype(BF16)
    hi, mid, lo = _split3(log_a)
    b_ref[...] = _dot(tri, hi) + _dot(tri, mid) + _dot(tri, lo)

    expand = (_iota((GLA_KW, BRANCH_WIDTH), 0) // GLA_DK
              == _iota((GLA_KW, BRANCH_WIDTH), 1) // HEAD_DIM).astype(BF16)
    st_mask = (_iota((BRANCH_WIDTH, GLA_KW), 0) // HEAD_DIM
               == _iota((BRANCH_WIDTH, GLA_KW), 1) // GLA_DK).astype(F32)
    srow = _iota((c, GLA_KW), 0)
    orow = _iota((c, BRANCH_WIDTH), 0)
    scale = GLA_DK ** -0.5

    def step(r0, st):
        qs = q_ref[pl.ds(r0, c), :].astype(F32) * scale
        ks = k_ref[pl.ds(r0, c), :].astype(F32)
        vb = v_ref[pl.ds(r0, c), :]
        vf = vb.astype(F32)
        bs = b_ref[pl.ds(r0, c), :]
        b_last = bs[c - 1:c, :]

        o_inter = _dot_nt((qs * jnp.exp(bs)).astype(BF16), st.astype(BF16))

        pieces = []
        for tt in range(c):
            e = jnp.exp(jnp.where(srow <= tt, bs[tt:tt + 1, :] - bs, NEG))
            pieces.append(e * qs[tt:tt + 1, :] * ks)
        pm = jnp.concatenate(pieces, axis=0).astype(BF16)
        a = _dot(pm, expand)
        o_intra = jnp.zeros((c, BRANCH_WIDTH), F32)
        for tt in range(c):
            o_t = jnp.sum(a[tt * c:(tt + 1) * c, :] * vf, axis=0, keepdims=True)
            o_intra = jnp.where(orow == tt, o_t, o_intra)
        o_ref[pl.ds(r0, c), :] = o_inter + o_intra

        ke = (ks * jnp.exp(b_last - bs)).astype(BF16)
        return st * jnp.exp(b_last) + _dot_tn(vb, ke) * st_mask

    def steps(g, _):
        st = st_ref[...]
        for u in range(GLA_UNROLL):
            st = step(pl.multiple_of((g * GLA_UNROLL + u) * c, c), st)
        st_ref[...] = st
        return 0

    lax.fori_loop(0, t // (c * GLA_UNROLL), steps, 0)

    o = o_ref[...]
    ones_bd = (_iota((BRANCH_WIDTH, BRANCH_WIDTH), 0) // HEAD_DIM
               == _iota((BRANCH_WIDTH, BRANCH_WIDTH), 1) // HEAD_DIM).astype(BF16)
    sq_hi, sq_mid, sq_lo = _split3(o * o)
    ms = (_dot(sq_hi, ones_bd) + _dot(sq_mid, ones_bd) + _dot(sq_lo, ones_bd)) * (1.0 / HEAD_DIM)
    gr = r_ref[...].astype(F32)
    y = o * lax.rsqrt(ms + RMS_EPS) * g_ref[...] * (gr * _sigmoid(gr))
    y_ref[...] = y.astype(y_ref.dtype)


def _gla(p, tail, aup_pad, ab_row, g_row, t=512):
    s = p.shape[0]
    cq = P_GLA // GLA_KW
    cv = (P_GLA + 2 * GLA_KW) // BRANCH_WIDTH
    return pl.pallas_call(
        _gla_kernel,
        grid=(s // t,),
        in_specs=[pl.BlockSpec((t, GLA_KW), lambda i: (i, cq)),
                  pl.BlockSpec((t, GLA_KW), lambda i: (i, cq + 1)),
                  pl.BlockSpec((t, BRANCH_WIDTH), lambda i: (i, cv)),
                  pl.BlockSpec((t, BRANCH_WIDTH), lambda i: (i, cv + 1)),
                  pl.BlockSpec((t, LANES), lambda i: (i, 0)),
                  pl.BlockSpec((LANES, GLA_KW), lambda i: (0, 0)),
                  pl.BlockSpec((1, GLA_KW), lambda i: (0, 0)),
                  pl.BlockSpec((1, BRANCH_WIDTH), lambda i: (0, 0))],
        out_specs=pl.BlockSpec((t, BRANCH_WIDTH), lambda i: (i, 0)),
        out_shape=jax.ShapeDtypeStruct((s, BRANCH_WIDTH), BF16),
        scratch_shapes=[pltpu.VMEM((BRANCH_WIDTH, GLA_KW), F32),
                        pltpu.VMEM((t, GLA_KW), F32),
                        pltpu.VMEM((t, BRANCH_WIDTH), F32)],
        compiler_params=_params(("arbitrary",)),
    )(p, p, p, p, tail, aup_pad, ab_row, g_row)


def _layer_norm(z, g, b):
    mu = jnp.mean(z, axis=1, keepdims=True)
    zc = z - mu
    var = jnp.mean(zc * zc, axis=1, keepdims=True)
    return zc * lax.rsqrt(var + LN_EPS) * g + b


def _store_row_tiles(ref, val):
    n = val.shape[0]
    for a in range(ROW_TILE):
        ref[pl.ds(a, n, stride=ROW_TILE), :] = val[:, a * LANES:(a + 1) * LANES]


def _load_row_tiles(ref, n, base=0):
    return jnp.concatenate(
        [ref[pl.ds(base + a, n, stride=ROW_TILE), :] for a in range(ROW_TILE)], axis=1)


def _merge_kernel(fox_ref, gla_ref, moba_ref, g0_ref, g1_ref, g2_ref, x_ref, wb_ref, wo_ref,
                  lg_ref, lb_ref, wrh_ref, wrl_ref, br_ref,
                  x1_ref, sel_ref, idx_ref, gw_ref):
    tm = x_ref.shape[0]

    def heads(ref):
        return jnp.concatenate([ref[h] for h in range(N_HEADS)], axis=1)

    merged = _sigmoid(g0_ref[...].astype(F32)) * _dot(heads(fox_ref), wb_ref[0])
    merged += _sigmoid(g1_ref[...].astype(F32)) * _dot(gla_ref[...], wb_ref[1])
    merged += _sigmoid(g2_ref[...].astype(F32)) * _dot(heads(moba_ref), wb_ref[2])
    z = ALPHA * x_ref[...] + _dot(merged.astype(BF16), wo_ref[...])
    x1 = _layer_norm(z, lg_ref[...], lb_ref[...])
    _store_row_tiles(x1_ref, x1)

    x_hi, x_lo = _split2(x1)
    both = _dot(x_hi, wrl_ref[...])
    logits = both[:, :LANES] + both[:, LANES:] + _dot(x_lo, wrh_ref[...]) + br_ref[...]
    lane = _iota((tm, LANES), 1)
    lg = jnp.where(lane < N_EXPERTS, logits, NEG)
    sel = jnp.zeros((tm, LANES), F32)
    idxf = jnp.zeros((tm, LANES), F32)
    ew = jnp.zeros((tm, LANES), F32)
    top = None
    for r in range(TOP_K):
        mx = jnp.max(lg, axis=1, keepdims=True)
        idx = jnp.min(jnp.where(lg == mx, lane, LANES), axis=1, keepdims=True)
        hit = lane == idx
        top = mx if top is None else top
        sel = jnp.where(hit, 1.0, sel)
        idxf = jnp.where(lane == r, idx.astype(F32), idxf)
        ew = jnp.where(lane == r, jnp.exp(mx - top), ew)
        lg = jnp.where(hit, NEG, lg)
    sel_ref[...] = sel
    idx_ref[...] = idxf
    gw_ref[...] = ew / jnp.sum(ew, axis=1, keepdims=True)


def _merge(fox_o, y_gla, moba_o, p, x, wb, wo, lg, lb, wrh, wrl, br, tm=256):
    s = x.shape[0]
    head_spec = pl.BlockSpec((N_HEADS, tm, HEAD_DIM), lambda i: (0, i, 0))
    row = lambda n: pl.BlockSpec((1, n), lambda i: (0, 0))
    small = jax.ShapeDtypeStruct((s, LANES), F32)
    small_spec = pl.BlockSpec((tm, LANES), lambda i: (i, 0))
    return pl.pallas_call(
        _merge_kernel,
        grid=(s // tm,),
        in_specs=[head_spec,
                  pl.BlockSpec((tm, BRANCH_WIDTH), lambda i: (i, 0)),
                  head_spec,
                  pl.BlockSpec((tm, D_MODEL), lambda i: (i, 0)),
                  pl.BlockSpec((tm, D_MODEL), lambda i: (i, 1)),
                  pl.BlockSpec((tm, D_MODEL), lambda i: (i, 2)),
                  pl.BlockSpec((tm, D_MODEL), lambda i: (i, 0)),
                  pl.BlockSpec((3, BRANCH_WIDTH, D_MODEL), lambda i: (0, 0, 0)),
                  pl.BlockSpec((D_MODEL, D_MODEL), lambda i: (0, 0)),
                  row(D_MODEL), row(D_MODEL),
                  pl.BlockSpec((D_MODEL, LANES), lambda i: (0, 0)),
                  pl.BlockSpec((D_MODEL, 2 * LANES), lambda i: (0, 0)),
                  row(LANES)],
        out_specs=[pl.BlockSpec((tm * ROW_TILE, LANES), lambda i: (i, 0)),
                   small_spec, small_spec, small_spec],
        out_shape=[jax.ShapeDtypeStruct((s * ROW_TILE, LANES), F32), small, small, small],
        compiler_params=_params(("parallel",)),
    )(fox_o, y_gla, moba_o, p, p, p, x, wb, wo, lg, lb, wrh, wrl, br)


def _rank_kernel(sel_ref, idx_ref, dest_ref, cnt_ref, carry_ref, start_ref):
    phase = pl.program_id(0)
    i = pl.program_id(1)
    t = sel_ref.shape[0]
    sel = sel_ref[...]
    chosen = jnp.sum(sel, axis=0, keepdims=True)

    @pl.when(jnp.logical_and(phase == 0, i == 0))
    def _():
        carry_ref[...] = jnp.zeros_like(carry_ref)

    @pl.when(jnp.logical_and(phase == 1, i == 0))
    def _():
        counts = carry_ref[...]
        cnt_ref[...] = counts
        blocks = jnp.floor((counts + (EXPERT_BLOCK - 1)) * (1.0 / EXPERT_BLOCK))
        before = (_iota((LANES, LANES), 0) < _iota((LANES, LANES), 1)).astype(BF16)
        first = _dot(jnp.broadcast_to(blocks, (ROW_TILE, LANES)).astype(BF16), before)
        start_ref[...] = first[0:1, :] * float(EXPERT_BLOCK)
        carry_ref[...] = jnp.zeros_like(carry_ref)

    @pl.when(phase == 1)
    def _():
        stri = (_iota((t, t), 0) > _iota((t, t), 1)).astype(BF16)
        pos = _dot(stri, sel.astype(BF16)) + carry_ref[...] + start_ref[...]
        lane = _iota((t, LANES), 1).astype(F32)
        idxf = idx_ref[...]
        dest = jnp.zeros((t, LANES), F32)
        for r in range(TOP_K):
            mine = jnp.sum(jnp.where(lane == idxf[:, r:r + 1], pos, 0.0), axis=1, keepdims=True)
            dest = jnp.where(lane == float(r), mine, dest)
        dest_ref[...] = dest.T[:ROW_TILE, :].astype(jnp.int32)

    carry_ref[...] += chosen


def _rank(sel, idxf, t=512):
    s = sel.shape[0]
    spec = pl.BlockSpec((t, LANES), lambda ph, i: (i, 0))
    return pl.pallas_call(
        _rank_kernel,
        grid=(2, s // t),
        in_specs=[spec, spec],
        out_specs=[pl.BlockSpec((ROW_TILE, t), lambda ph, i: (0, i * ph)),
                   pl.BlockSpec((1, LANES), lambda ph, i: (0, 0))],
        out_shape=[jax.ShapeDtypeStruct((ROW_TILE, s), jnp.int32),
                   jax.ShapeDtypeStruct((1, LANES), F32)],
        scratch_shapes=[pltpu.VMEM((1, LANES), F32), pltpu.VMEM((1, LANES), F32)],
        compiler_params=_params(("arbitrary", "arbitrary")),
    )(sel, idxf)


DMA_UNROLL = 4


def _token_rows(ref, token):
    return ref.at[pl.ds(pl.multiple_of(token * ROW_TILE, ROW_TILE), ROW_TILE), :]


def _dispatch_kernel(dest_ref, x_ref, xin_ref, sem):
    n = dest_ref.shape[1]

    def issue(g, _):
        for u in range(DMA_UNROLL):
            t = g * DMA_UNROLL + u
            for k in range(TOP_K):
                pltpu.make_async_copy(_token_rows(x_ref, t), _token_rows(xin_ref, dest_ref[k, t]),
                                      sem).start()
        return 0

    lax.fori_loop(0, n // DMA_UNROLL, issue, 0)
    for k in range(TOP_K):
        pltpu.make_async_copy(x_ref, xin_ref.at[pl.ds(0, n * ROW_TILE), :], sem).wait()


def _dispatch(dest, x1_2d, n_rows, td=512):
    s = x1_2d.shape[0] // ROW_TILE
    return pl.pallas_call(
        _dispatch_kernel,
        grid=(s // td,),
        in_specs=[pl.BlockSpec((ROW_TILE, td), lambda i: (0, i), memory_space=pltpu.SMEM),
                  pl.BlockSpec((td * ROW_TILE, LANES), lambda i: (i, 0))],
        out_specs=pl.BlockSpec(memory_space=pl.ANY),
        out_shape=jax.ShapeDtypeStruct((n_rows * ROW_TILE, LANES), F32),
        scratch_shapes=[pltpu.SemaphoreType.DMA(())],
        compiler_params=pltpu.CompilerParams(dimension_semantics=("arbitrary",),
                                             has_side_effects=True),
    )(dest, x1_2d)


HALF = LANES // 2
W_CAST_ROWS = 64


def _expert_kernel(be_ref, nv_ref, xin_ref, wgu_ref, bgu_ref, wdn_ref, bdn_ref, y_ref,
                   wgu_s, wdn_s, perm_s):
    b = pl.program_id(0)
    blk = y_ref.shape[0] // ROW_TILE
    nvalid = nv_ref[b]
    new_expert = jnp.logical_or(b == 0, be_ref[b] != be_ref[jnp.maximum(b - 1, 0)])

    @pl.when(jnp.logical_and(new_expert, nvalid > 0))
    def _():
        def cast(r, _):
            r0 = pl.multiple_of(r * W_CAST_ROWS, W_CAST_ROWS)
            wgu_s[pl.ds(r0, W_CAST_ROWS), :] = wgu_ref[0, pl.ds(r0, W_CAST_ROWS), :].astype(BF16)
            return 0

        lax.fori_loop(0, D_MODEL // W_CAST_ROWS, cast, 0)
        for c in range(D_MODEL // LANES):
            cols = slice(c * LANES, (c + 1) * LANES)
            for g in range(D_MODEL // LANES):
                lo = g * LANES
                perm_s[c, pl.ds(lo, HALF, stride=2), :] = wdn_ref[0, lo:lo + HALF, cols]
                perm_s[c, pl.ds(lo + 1, HALF, stride=2), :] = wdn_ref[0, lo + HALF:lo + LANES, cols]
            wdn_s[:, cols] = perm_s[c].astype(BF16)

    @pl.when(nvalid > 0)
    def _():
        x = _load_row_tiles(xin_ref, blk)
        x = jnp.where(_iota((blk, D_MODEL), 0) < nvalid, x, 0.0).astype(BF16)
        even = _iota((blk, LANES), 1) % 2 == 0
        acts = []
        for g in range(D_MODEL // LANES):
            lo = g * 2 * LANES
            h = _dot(x, wgu_s[:, lo:lo + 2 * LANES]) + bgu_ref[0, :, lo:lo + 2 * LANES]
            h_a = h[:, :LANES]
            h_b = h[:, LANES:]
            gate = jnp.where(even, h_a, pltpu.roll(h_b, 1, 1))
            up = jnp.where(even, pltpu.roll(h_a, LANES - 1, 1), h_b)
            gate = jnp.minimum(gate, SWIGLU_LIMIT)
            up = jnp.clip(up, -SWIGLU_LIMIT, SWIGLU_LIMIT)
            acts.append(((up + 1.0) * (gate * _sigmoid(SWIGLU_ALPHA * gate))).astype(BF16))
        act = jnp.concatenate(acts, axis=1)
        _store_row_tiles(y_ref, _dot(act, wdn_s[...]) + bdn_ref[0])

    @pl.when(nvalid <= 0)
    def _():
        y_ref[...] = jnp.zeros_like(y_ref)


def _experts(block_expert, block_valid, xin2d, w_gu, b_gu, w_dn, b_dn, layer):
    blk = EXPERT_BLOCK
    nb = block_expert.shape[0]
    x_spec = pl.BlockSpec((blk * ROW_TILE, LANES), lambda b, be, nv: (b, 0))
    which = lambda b, be, nv: (layer * N_EXPERTS + be[b], 0, 0)
    return pl.pallas_call(
        _expert_kernel,
        grid_spec=pltpu.PrefetchScalarGridSpec(
            num_scalar_prefetch=2,
            grid=(nb,),
            in_specs=[x_spec,
                      pl.BlockSpec((1, D_MODEL, 2 * D_MODEL), which),
                      pl.BlockSpec((1, 1, 2 * D_MODEL), which),
                      pl.BlockSpec((1, D_MODEL, D_MODEL), which),
                      pl.BlockSpec((1, 1, D_MODEL), which)],
            out_specs=x_spec,
            scratch_shapes=[pltpu.VMEM((D_MODEL, 2 * D_MODEL), BF16),
                            pltpu.VMEM((D_MODEL, D_MODEL), BF16),
                            pltpu.VMEM((D_MODEL // LANES, D_MODEL, LANES), F32)]),
        out_shape=jax.ShapeDtypeStruct((nb * blk * ROW_TILE, LANES), F32),
        compiler_params=pltpu.CompilerParams(dimension_semantics=("arbitrary",),
                                             vmem_limit_bytes=56 * 1024 * 1024),
    )(block_expert, block_valid, xin2d, w_gu, b_gu, w_dn, b_dn)


def _combine_kernel(dest_ref, next_ref, yb_ref, gw_ref, x1_ref, lg_ref, lb_ref, out_ref,
                    buf_ref, sem):
    i = pl.program_id(0)
    tc = out_ref.shape[0]
    slot = i % 2

    def gather(rows_ref, into):
        def body(g, _):
            for u in range(DMA_UNROLL):
                t = g * DMA_UNROLL + u
                for k in range(TOP_K):
                    pltpu.make_async_copy(_token_rows(yb_ref, rows_ref[k, t]),
                                          _token_rows(buf_ref.at[into, k], t), sem.at[into]).start()
            return 0

        lax.fori_loop(0, tc // DMA_UNROLL, body, 0)

    @pl.when(i == 0)
    def _():
        gather(dest_ref, 0)

    @pl.when(i + 1 < pl.num_programs(0))
    def _():
        gather(next_ref, 1 - slot)

    for k in range(TOP_K):
        pltpu.make_async_copy(yb_ref.at[pl.ds(0, tc * ROW_TILE), :], buf_ref.at[slot, k],
                              sem.at[slot]).wait()

    gw = gw_ref[...]
    f = jnp.zeros((tc, D_MODEL), F32)
    for k in range(TOP_K):
        f += gw[:, k:k + 1] * _load_row_tiles(buf_ref.at[slot, k], tc)
    z = ALPHA * _load_row_tiles(x1_ref, tc) + f
    out_ref[...] = _layer_norm(z, lg_ref[...], lb_ref[...])


def _combine(dest, yb_rows, gw, x1_2d, lg, lb, tc=256):
    s = gw.shape[0]
    n = s // tc
    row = pl.BlockSpec((1, D_MODEL), lambda i: (0, 0))
    return pl.pallas_call(
        _combine_kernel,
        grid=(n,),
        in_specs=[pl.BlockSpec((ROW_TILE, tc), lambda i: (0, i), memory_space=pltpu.SMEM),
                  pl.BlockSpec((ROW_TILE, tc), lambda i: (0, jnp.minimum(i + 1, n - 1)),
                               memory_space=pltpu.SMEM),
                  pl.BlockSpec(memory_space=pl.ANY),
                  pl.BlockSpec((tc, LANES), lambda i: (i, 0)),
                  pl.BlockSpec((tc * ROW_TILE, LANES), lambda i: (i, 0)),
                  row, row],
        out_specs=pl.BlockSpec((tc, D_MODEL), lambda i: (i, 0)),
        out_shape=jax.ShapeDtypeStruct((s, D_MODEL), F32),
        scratch_shapes=[pltpu.VMEM((2, TOP_K, tc * ROW_TILE, LANES), F32),
                        pltpu.SemaphoreType.DMA((2,))],
        compiler_params=_params(("arbitrary",)),
    )(dest, dest, yb_rows, gw, x1_2d, lg, lb)


W_IN_SIZES = (256, 256, 256, 4, 128, 128, 256, 16, 256, 256, 256, 256, 3 * D_MODEL)
W_IN_WIDTH = sum(W_IN_SIZES)


def _w_in_kernel(w_ref, main_ref, tail_ref):
    w = w_ref[0]
    offs = [0]
    for n in W_IN_SIZES:
        offs.append(offs[-1] + n)
    fq, fk, fv, ff, gq, gk, gv, ga, gr, mq, mk, mv, gates = (
        w[:, offs[i]:offs[i + 1]] for i in range(len(W_IN_SIZES)))
    main_ref[...] = jnp.concatenate(
        [gates, fq, fk, fv, mq, mk, mv, gq, gk, gv, gr], axis=1).astype(BF16)
    pad = jnp.zeros((w.shape[0], LANES - 4 - GLA_RANK), F32)
    tail_ref[...] = jnp.concatenate([ff, ga, pad], axis=1).astype(BF16)


def _permute_w_in(w_in, layer, tr=128):
    return pl.pallas_call(
        _w_in_kernel,
        grid=(D_MODEL // tr,),
        in_specs=[pl.BlockSpec((1, tr, W_IN_WIDTH), lambda i: (layer, i, 0))],
        out_specs=[pl.BlockSpec((tr, P_WIDTH), lambda i: (i, 0)),
                   pl.BlockSpec((tr, LANES), lambda i: (i, 0))],
        out_shape=[jax.ShapeDtypeStruct((D_MODEL, P_WIDTH), BF16),
                   jax.ShapeDtypeStruct((D_MODEL, LANES), BF16)],
        compiler_params=_params(("parallel",)),
    )(w_in)


def _pad_row(v, n=LANES, fill=0.0):
    return jnp.pad(v.astype(F32), (0, n - v.shape[0]), constant_values=fill)[None, :]


def _mixer_layer(x, w_in, layer, fox_fb, gla_a_up, gla_a_b, gla_norm_g, w_branch, w_out, ln_g, ln_b,
                 w_router, b_router):
    w_main, w_tail = _permute_w_in(w_in, layer)
    p, tail = _inproj(x, w_main, w_tail)

    s = x.shape[0]
    tq, tk = _flash_tiles(s)
    fox_qt, fox_k, fox_vt, fox_stats = _fox_prep(p, tail, _pad_row(fox_fb))
    fox_o = _flash(_fox_first_tile(fox_stats, s, tq, tk), fox_qt, fox_k, fox_vt, tq, tk)
    moba_o = _flash(jnp.zeros((N_HEADS * (s // tq),), jnp.int32), *_moba_prep(p), tq, tk)
    aup_pad = jnp.zeros((LANES, GLA_KW), F32).at[TAIL_GA:TAIL_GA + GLA_RANK].set(gla_a_up)
    y_gla = _gla(p, tail, aup_pad, gla_a_b[None, :], gla_norm_g[None, :])

    wr = jnp.pad(w_router, ((0, 0), (0, LANES - N_EXPERTS)))
    wr_hi = wr.astype(BF16)
    wr_lo = jnp.concatenate([wr_hi, (wr - wr_hi.astype(F32)).astype(BF16)], axis=1)
    return _merge(fox_o, y_gla, moba_o, p, x, w_branch.astype(BF16), w_out.astype(BF16),
                  ln_g[None, :], ln_b[None, :], wr_hi, wr_lo, _pad_row(b_router))


def _moe_layer(x1_2d, sel, idxf, gw, w_gu, b_gu, w_dn, b_dn, ln_g, ln_b, layer):
    s = sel.shape[0]
    blk = EXPERT_BLOCK
    dest, counts = _rank(sel, idxf)

    counts = counts[0, :N_EXPERTS].astype(jnp.int32)
    padded = (counts + blk - 1) // blk * blk
    pad_end = jnp.cumsum(padded)
    pad_start = pad_end - padded
    n_blocks = (s * TOP_K + N_EXPERTS * (blk - 1)) // blk + 1
    block_start = jnp.arange(n_blocks, dtype=jnp.int32) * blk
    active = block_start < pad_end[-1]
    block_expert = jnp.sum((block_start[:, None] >= pad_end[None, :]).astype(jnp.int32), axis=1)
    last_expert = jnp.max(jnp.where(counts > 0, jnp.arange(N_EXPERTS, dtype=jnp.int32), 0))
    block_expert = jnp.where(active, block_expert, last_expert)
    block_valid = jnp.clip(pad_start[block_expert] + counts[block_expert] - block_start, 0, blk)
    block_valid = jnp.where(active, block_valid, 0).astype(jnp.int32)

    xin = _dispatch(dest, x1_2d, n_blocks * blk)
    yb = _experts(block_expert, block_valid, xin, w_gu, b_gu, w_dn, b_dn, layer)
    return _combine(dest, yb, gw, x1_2d, ln_g[None, :], ln_b[None, :])


def kernel(x, w_in, fox_fb, gla_a_up, gla_a_b, gla_norm_g, w_branch, w_out, ln1_g, ln1_b,
           w_router, b_router, w_gu, b_gu, w_dn, b_dn, ln2_g, ln2_b):
    b, s, d = x.shape
    assert b == 1 and d == D_MODEL
    h = x.reshape(s, d)
    n_le = DEPTH * N_EXPERTS
    w_gu = w_gu.reshape(n_le, D_MODEL, 2 * D_MODEL)
    b_gu = b_gu.reshape(n_le, 1, 2 * D_MODEL)
    w_dn = w_dn.reshape(n_le, D_MODEL, D_MODEL)
    b_dn = b_dn.reshape(n_le, 1, D_MODEL)
    for l in range(DEPTH):
        x1_2d, sel, idxf, gw = _mixer_layer(
            h, w_in, l, fox_fb[l], gla_a_up[l], gla_a_b[l], gla_norm_g[l], w_branch[l], w_out[l],
            ln1_g[l], ln1_b[l], w_router[l], b_router[l])
        h = _moe_layer(x1_2d, sel, idxf, gw, w_gu, b_gu, w_dn, b_dn, ln2_g[l], ln2_b[l], l)
    return h.reshape(b, s, d)
```

```python
import functools

import jax
import jax.numpy as jnp
from jax import lax
from jax.experimental import pallas as pl
from jax.experimental.pallas import tpu as pltpu

F32 = jnp.float32
BF16 = jnp.bfloat16

D_MODEL = 1024
DEPTH = 4
N_HEADS = 4
HEAD_DIM = 64
GLA_DK = 32
GLA_KW = N_HEADS * GLA_DK
GLA_RANK = 16
GLA_TAU = 16.0
MOBA_BLOCK = 256
MOBA_TOPK = 3
BRANCH_WIDTH = 256
N_EXPERTS = 32
TOP_K = 4
SWIGLU_LIMIT = 7.0
SWIGLU_ALPHA = 1.702
ALPHA = (2 * DEPTH) ** 0.25
LN_EPS = 1e-5
RMS_EPS = 1e-6
LOG2E = 1.4426950408889634

LANES = 128
ROW_TILE = 8
NEG = -1e30
UNSELECTED = -32768.0
VMEM_LIMIT = 48 * 1024 * 1024

P_GATES = 0
P_FOX = 3 * D_MODEL
P_MOBA = P_FOX + 3 * BRANCH_WIDTH
P_GLA = P_MOBA + 3 * BRANCH_WIDTH
P_WIDTH = P_GLA + 2 * GLA_KW + 2 * BRANCH_WIDTH
TAIL_FF = 0
TAIL_GA = 4

EXPERT_BLOCK = 512


def _params(sem):
    return pltpu.CompilerParams(dimension_semantics=sem, vmem_limit_bytes=VMEM_LIMIT)


def _split3(x):
    hi = x.astype(BF16)
    r1 = x - hi.astype(F32)
    mid = r1.astype(BF16)
    lo = (r1 - mid.astype(F32)).astype(BF16)
    return hi, mid, lo


def _split2(x):
    hi = x.astype(BF16)
    lo = (x - hi.astype(F32)).astype(BF16)
    return hi, lo


def _dot(a, b):
    return jnp.dot(a, b, preferred_element_type=F32)


def _dot_nt(a, b):
    return lax.dot_general(a, b, (((1,), (1,)), ((), ())), preferred_element_type=F32)


def _dot_tn(a, b):
    return lax.dot_general(a, b, (((0,), (0,)), ((), ())), preferred_element_type=F32)


def _log_sigmoid(t):
    return jnp.minimum(t, 0.0) - jnp.log1p(jnp.exp(-jnp.abs(t)))


def _sigmoid(t):
    return 0.5 * jnp.tanh(0.5 * t) + 0.5


def _iota(shape, axis):
    return lax.broadcasted_iota(jnp.int32, shape, axis)


def _transpose_bf16(x):
    return x.astype(F32).T.astype(BF16)


BF16_ROWS = 16
V_ROWS = HEAD_DIM + BF16_ROWS


def _values_t(v):
    ones_row = jnp.where(_iota((BF16_ROWS, v.shape[0]), 0) == 0, 1.0, 0.0).astype(BF16)
    return jnp.concatenate([_transpose_bf16(v), ones_row], axis=0)


def _aug_specs(s, t):
    aug = jax.ShapeDtypeStruct((N_HEADS, s, LANES), BF16)
    aug_spec = pl.BlockSpec((N_HEADS, t, LANES), lambda i: (0, i, 0))
    aug_t = jax.ShapeDtypeStruct((N_HEADS, LANES, s), BF16)
    aug_t_spec = pl.BlockSpec((N_HEADS, LANES, t), lambda i: (0, 0, i))
    aug_v = jax.ShapeDtypeStruct((N_HEADS, V_ROWS, s), BF16)
    aug_v_spec = pl.BlockSpec((N_HEADS, V_ROWS, t), lambda i: (0, 0, i))
    return aug, aug_spec, aug_t, aug_t_spec, aug_v, aug_v_spec


def _inproj_kernel(x_ref, w_ref, wt_ref, sc_ref, p_ref, t_ref, xb_ref):
    @pl.when(pl.program_id(1) == 0)
    def _():
        xb_ref[...] = x_ref[...].astype(BF16)
        t_ref[...] = _dot(xb_ref[...], wt_ref[...])

    p_ref[...] = (_dot(xb_ref[...], w_ref[...]) * sc_ref[...]).astype(BF16)


def _inproj(x, w, wt, tm=1024, tn=1792):
    s = x.shape[0]
    col = jnp.arange(P_WIDTH)
    is_q = ((col >= P_FOX) & (col < P_FOX + BRANCH_WIDTH)) | ((col >= P_MOBA) & (col < P_MOBA + BRANCH_WIDTH))
    scale = jnp.where(is_q, HEAD_DIM ** -0.5 * LOG2E, 1.0).astype(F32)[None, :]
    return pl.pallas_call(
        _inproj_kernel,
        grid=(s // tm, P_WIDTH // tn),
        in_specs=[pl.BlockSpec((tm, D_MODEL), lambda i, j: (i, 0)),
                  pl.BlockSpec((D_MODEL, tn), lambda i, j: (0, j)),
                  pl.BlockSpec((D_MODEL, LANES), lambda i, j: (0, 0)),
                  pl.BlockSpec((1, tn), lambda i, j: (0, j))],
        out_specs=[pl.BlockSpec((tm, tn), lambda i, j: (i, j)),
                   pl.BlockSpec((tm, LANES), lambda i, j: (i, 0))],
        out_shape=[jax.ShapeDtypeStruct((s, P_WIDTH), BF16),
                   jax.ShapeDtypeStruct((s, LANES), F32)],
        scratch_shapes=[pltpu.VMEM((tm, D_MODEL), BF16)],
        compiler_params=_params(("parallel", "arbitrary")),
    )(x, w, wt, scale)


def _fox_prep_kernel(q_ref, k_ref, v_ref, t_ref, fb_ref, qa_ref, ka_ref, va_ref, st_ref, carry_ref):
    t = q_ref.shape[0]

    @pl.when(pl.program_id(0) == 0)
    def _():
        carry_ref[...] = jnp.zeros_like(carry_ref)

    ls = _log_sigmoid(t_ref[...] + fb_ref[...])
    tri = (_iota((t, t), 0) >= _iota((t, t), 1)).astype(BF16)
    hi, mid, lo = _split3(ls)
    c = _dot(tri, hi) + _dot(tri, mid) + _dot(tri, lo) + carry_ref[...]
    carry_ref[...] = c[t - 1:t, :]

    c2 = LOG2E * c

    lane = _iota((t, HEAD_DIM), 1)
    q_aug = jnp.where(_iota((HEAD_DIM, t), 0) < 3, 1.0, 0.0).astype(BF16)
    q = q_ref[...]
    k = k_ref[...]
    v = v_ref[...]
    srow = _iota((ROW_TILE, LANES), 0)
    slane = _iota((ROW_TILE, LANES), 1)
    stat = jnp.where(srow == 2, c2[0:1, :], jnp.where(srow == 3, c2[t - 1:t, :], 0.0))
    for h in range(N_HEADS):
        sl = slice(h * HEAD_DIM, (h + 1) * HEAD_DIM)
        nhi, nmid, nlo = (piece.astype(F32) for piece in _split3(-c2[:, h:h + 1]))
        k_aug = jnp.where(lane == 0, nhi, jnp.where(lane == 1, nmid, jnp.where(lane == 2, nlo, 0.0)))
        qa_ref[h] = jnp.concatenate([_transpose_bf16(q[:, sl]), q_aug], axis=0)
        ka_ref[h] = jnp.concatenate([k[:, sl], k_aug.astype(BF16)], axis=1)
        va_ref[h] = _values_t(v[:, sl])
        for r, x in ((0, q), (1, k)):
            xf = x[:, sl].astype(F32)
            norm = jnp.sqrt(jnp.max(jnp.sum(xf * xf, axis=1, keepdims=True), axis=0, keepdims=True))
            stat = jnp.where(jnp.logical_and(srow == r, slane == h), norm, stat)
    st_ref[0] = stat


def _fox_prep(p, tail, fb_row, t=512):
    s = p.shape[0]
    cb = P_FOX // BRANCH_WIDTH
    aug, aug_spec, aug_t, aug_t_spec, aug_v, aug_v_spec = _aug_specs(s, t)
    return pl.pallas_call(
        _fox_prep_kernel,
        grid=(s // t,),
        in_specs=[pl.BlockSpec((t, BRANCH_WIDTH), lambda i: (i, cb)),
                  pl.BlockSpec((t, BRANCH_WIDTH), lambda i: (i, cb + 1)),
                  pl.BlockSpec((t, BRANCH_WIDTH), lambda i: (i, cb + 2)),
                  pl.BlockSpec((t, LANES), lambda i: (i, 0)),
                  pl.BlockSpec((1, LANES), lambda i: (0, 0))],
        out_specs=[aug_t_spec, aug_spec, aug_v_spec,
                   pl.BlockSpec((1, ROW_TILE, LANES), lambda i: (i, 0, 0))],
        out_shape=[aug_t, aug, aug_v, jax.ShapeDtypeStruct((s // t, ROW_TILE, LANES), F32)],
        scratch_shapes=[pltpu.VMEM((1, LANES), F32)],
        compiler_params=_params(("arbitrary",)),
    )(p, p, p, tail, fb_row)


SKIP_LOG2 = 48.0


def _fox_first_tile(stats, s, tq, tk):
    tp = s // stats.shape[0]
    qn, kn, c_first, c_last = (stats[:, r, :N_HEADS] for r in range(4))
    nq, nk = s // tq, s // tk
    qn = jnp.max(qn.reshape(nq, tq // tp, N_HEADS), axis=1)
    c_q = c_first.reshape(nq, tq // tp, N_HEADS)[:, 0, :]
    c_k = c_last.reshape(nk, tk // tp, N_HEADS)[:, -1, :]
    bound = (2.0 * qn * jnp.max(kn, axis=0))[:, None, :] + c_q[:, None, :] - c_k[None, :, :]
    j = jnp.arange(nk, dtype=jnp.int32)[None, :, None]
    first = jnp.min(jnp.where(bound >= -SKIP_LOG2, j, nk), axis=1)
    return first.T.reshape(-1).astype(jnp.int32)


def _moba_prep_kernel(q_ref, k_ref, v_ref, qa_ref, ka_ref, va_ref, kmean_ref):
    i = pl.program_id(0)
    t = q_ref.shape[0]
    nbl = kmean_ref.shape[0]

    @pl.when(i == 0)
    def _():
        kmean_ref[...] = jnp.zeros_like(kmean_ref)

    q = q_ref[...]
    k = k_ref[...]
    v = v_ref[...]
    km = kmean_ref[...]
    k_aug = jnp.where(_iota((t, HEAD_DIM), 1) == i, 1.0, 0.0).astype(BF16)
    blk = _iota((HEAD_DIM, t), 0)
    for h in range(N_HEADS):
        sl = slice(h * HEAD_DIM, (h + 1) * HEAD_DIM)
        qt = _transpose_bf16(q[:, sl])
        kh_hi, kh_lo = _split2(km[:HEAD_DIM, sl])
        gate = _dot(kh_hi, qt) + _dot(kh_lo, qt)
        g = jnp.where(blk < i, gate, NEG)
        bias = jnp.where(blk == i, 0.0, UNSELECTED)
        for _ in range(MOBA_TOPK):
            mx = jnp.max(g, axis=0, keepdims=True)
            idx = jnp.min(jnp.where(g == mx, blk, HEAD_DIM), axis=0, keepdims=True)
            pick = jnp.logical_and(blk == idx, mx > 0.5 * NEG)
            bias = jnp.where(pick, 0.0, bias)
            g = jnp.where(blk == idx, NEG, g)
        qa_ref[h] = jnp.concatenate([qt, bias.astype(BF16)], axis=0)
        ka_ref[h] = jnp.concatenate([k[:, sl], k_aug], axis=1)
        va_ref[h] = _values_t(v[:, sl])
    kmean = jnp.mean(k.astype(F32), axis=0, keepdims=True)
    kmean_ref[...] = jnp.where(_iota((nbl, BRANCH_WIDTH), 0) == i, kmean, km)


def _moba_prep(p):
    s = p.shape[0]
    t = MOBA_BLOCK
    assert s % t == 0 and s // t <= HEAD_DIM, "block one-hot must fit the 64 augmentation lanes"
    cb = P_MOBA // BRANCH_WIDTH
    aug, aug_spec, aug_t, aug_t_spec, aug_v, aug_v_spec = _aug_specs(s, t)
    return pl.pallas_call(
        _moba_prep_kernel,
        grid=(s // t,),
        in_specs=[pl.BlockSpec((t, BRANCH_WIDTH), lambda i: (i, cb)),
                  pl.BlockSpec((t, BRANCH_WIDTH), lambda i: (i, cb + 1)),
                  pl.BlockSpec((t, BRANCH_WIDTH), lambda i: (i, cb + 2))],
        out_specs=[aug_t_spec, aug_spec, aug_v_spec],
        out_shape=[aug_t, aug, aug_v],
        scratch_shapes=[pltpu.VMEM((HEAD_DIM, BRANCH_WIDTH), F32)],
        compiler_params=_params(("arbitrary",)),
    )(p, p, p)


def _flash_kernel(first_ref, qt_ref, k_ref, vt_ref, o_ref, s0_ref, s1_ref, *, tq, tk):
    h = pl.program_id(0)
    i = pl.program_id(1)
    qt = qt_ref[0]
    n_diag = tq // tk
    n_full = i * n_diag

    def scores(j):
        return _dot(k_ref[0, pl.ds(pl.multiple_of(j * tk, tk), tk), :], qt)

    def absorb(j, s, m, acc, masked):
        if masked:
            key = _iota((tk, tq), 0) + j * tk
            qry = _iota((tk, tq), 1) + i * tq
            s = jnp.where(key <= qry, s, NEG)
        m_new = jnp.maximum(m, jnp.max(s, axis=0, keepdims=True))
        p = jnp.exp2(s - m_new).astype(BF16)
        vt = vt_ref[0, :, pl.ds(pl.multiple_of(j * tk, tk), tk)]
        return m_new, jnp.exp2(m - m_new) * acc + _dot(vt, p)

    j0 = jnp.minimum(first_ref[h * pl.num_programs(1) + i], n_full)
    m = jnp.full((1, tq), NEG, F32)
    acc = jnp.zeros((V_ROWS, tq), F32)

    odd = (n_full - j0) % 2

    def plain(j, carry):
        return absorb(j, scores(j), *carry, False)

    m, acc = lax.fori_loop(j0, j0 + odd, plain, (m, acc))
    j0 = j0 + odd

    s0_ref[...] = scores(j0)

    def pair(g, carry):
        j = j0 + 2 * g
        s1_ref[...] = scores(j + 1)
        carry = absorb(j, s0_ref[...], *carry, False)
        s0_ref[...] = scores(j + 2)
        return absorb(j + 1, s1_ref[...], *carry, False)

    m, acc = lax.fori_loop(0, (n_full - j0) // 2, pair, (m, acc))
    s = s0_ref[...]
    for d in range(n_diag):
        s_next = scores(n_full + d + 1) if d + 1 < n_diag else None
        m, acc = absorb(n_full + d, s, m, acc, True)
        s = s_next
    o_ref[0] = (acc[:HEAD_DIM, :] / acc[HEAD_DIM:HEAD_DIM + 1, :]).T.astype(o_ref.dtype)


def _flash(first_tile, qt, ka, vt, tq, tk):
    nh, s, _ = ka.shape
    return pl.pallas_call(
        functools.partial(_flash_kernel, tq=tq, tk=tk),
        grid_spec=pltpu.PrefetchScalarGridSpec(
            num_scalar_prefetch=1,
            grid=(nh, s // tq),
            in_specs=[pl.BlockSpec((1, LANES, tq), lambda h, i, f: (h, 0, i)),
                      pl.BlockSpec((1, s, LANES), lambda h, i, f: (h, 0, 0)),
                      pl.BlockSpec((1, V_ROWS, s), lambda h, i, f: (h, 0, 0))],
            out_specs=pl.BlockSpec((1, tq, HEAD_DIM), lambda h, i, f: (h, i, 0)),
            scratch_shapes=[pltpu.VMEM((tk, tq), F32), pltpu.VMEM((tk, tq), F32)]),
        out_shape=jax.ShapeDtypeStruct((nh, s, HEAD_DIM), BF16),
        compiler_params=_params(("parallel", "parallel")),
    )(first_tile, qt, ka, vt)


def _flash_tiles(s):
    tq = min(1024, s)
    return tq, min(512, tq)


GLA_SUB = 16
GLA_UNROLL = 2


def _gla_kernel(q_ref, k_ref, v_ref, r_ref, t_ref, aup_ref, ab_ref, g_ref, y_ref,
                st_ref, b_ref, o_ref):
    t = q_ref.shape[0]
    c = GLA_SUB

    @pl.when(pl.program_id(0) == 0)
    def _():
        st_ref[...] = jnp.zeros_like(st_ref)

    t_hi, t_lo = _split2(t_ref[...])
    a_hi, a_lo = _split2(aup_ref[...])
    z = _dot(t_hi, a_hi) + _dot(t_lo, a_hi) + _dot(t_hi, a_lo) + ab_ref[...]
    log_a = _log_sigmoid(z) * (1.0 / GLA_TAU)
    row = _iota((t, t), 0)
    col = _iota((t, t), 1)
    tri = jnp.logical_and(row // c == col // c, row >= col).astype(BF16)
    hi, mid, lo = _split3(log_a)
    b_ref[...] = _dot(tri, hi) + _dot(tri, mid) + _dot(tri, lo)

    expand = (_iota((GLA_KW, BRANCH_WIDTH), 0) // GLA_DK
              == _iota((GLA_KW, BRANCH_WIDTH), 1) // HEAD_DIM).astype(BF16)
    st_mask = (_iota((BRANCH_WIDTH, GLA_KW), 0) // HEAD_DIM
               == _iota((BRANCH_WIDTH, GLA_KW), 1) // GLA_DK).astype(F32)
    srow = _iota((c, GLA_KW), 0)
    orow = _iota((c, BRANCH_WIDTH), 0)
    scale = GLA_DK ** -0.5

    def step(r0, st):
        qs = q_ref[pl.ds(r0, c), :].astype(F32) * scale
        ks = k_ref[pl.ds(r0, c), :].astype(F32)
        vb = v_ref[pl.ds(r0, c), :]
        vf = vb.astype(F32)
        bs = b_ref[pl.ds(r0, c), :]
        b_last = bs[c - 1:c, :]

        o_inter = _dot_nt((qs * jnp.exp(bs)).astype(BF16), st.astype(BF16))

        pieces = []
        for tt in range(c):
            e = jnp.exp(jnp.where(srow <= tt, bs[tt:tt + 1, :] - bs, NEG))
            pieces.append(e * qs[tt:tt + 1, :] * ks)
        pm = jnp.concatenate(pieces, axis=0).astype(BF16)
        a = _dot(pm, expand)
        o_intra = jnp.zeros((c, BRANCH_WIDTH), F32)
        for tt in range(c):
            o_t = jnp.sum(a[tt * c:(tt + 1) * c, :] * vf, axis=0, keepdims=True)
            o_intra = jnp.where(orow == tt, o_t, o_intra)
        o_ref[pl.ds(r0, c), :] = o_inter + o_intra

        ke = (ks * jnp.exp(b_last - bs)).astype(BF16)
        return st * jnp.exp(b_last) + _dot_tn(vb, ke) * st_mask

    def steps(g, _):
        st = st_ref[...]
        for u in range(GLA_UNROLL):
            st = step(pl.multiple_of((g * GLA_UNROLL + u) * c, c), st)
        st_ref[...] = st
        return 0

    lax.fori_loop(0, t // (c * GLA_UNROLL), steps, 0)

    o = o_ref[...]
    ones_bd = (_iota((BRANCH_WIDTH, BRANCH_WIDTH), 0) // HEAD_DIM
               == _iota((BRANCH_WIDTH, BRANCH_WIDTH), 1) // HEAD_DIM).astype(BF16)
    sq_hi, sq_mid, sq_lo = _split3(o * o)
    ms = (_dot(sq_hi, ones_bd) + _dot(sq_mid, ones_bd) + _dot(sq_lo, ones_bd)) * (1.0 / HEAD_DIM)
    gr = r_ref[...].astype(F32)
    y = o * lax.rsqrt(ms + RMS_EPS) * g_ref[...] * (gr * _sigmoid(gr))
    y_ref[...] = y.astype(y_ref.dtype)


def _gla(p, tail, aup_pad, ab_row, g_row, t=512):
    s = p.shape[0]
    cq = P_GLA // GLA_KW
    cv = (P_GLA + 2 * GLA_KW) // BRANCH_WIDTH
    return pl.pallas_call(
        _gla_kernel,
        grid=(s // t,),
        in_specs=[pl.BlockSpec((t, GLA_KW), lambda i: (i, cq)),
                  pl.BlockSpec((t, GLA_KW), lambda i: (i, cq + 1)),
                  pl.BlockSpec((t, BRANCH_WIDTH), lambda i: (i, cv)),
                  pl.BlockSpec((t, BRANCH_WIDTH), lambda i: (i, cv + 1)),
                  pl.BlockSpec((t, LANES), lambda i: (i, 0)),
                  pl.BlockSpec((LANES, GLA_KW), lambda i: (0, 0)),
                  pl.BlockSpec((1, GLA_KW), lambda i: (0, 0)),
                  pl.BlockSpec((1, BRANCH_WIDTH), lambda i: (0, 0))],
        out_specs=pl.BlockSpec((t, BRANCH_WIDTH), lambda i: (i, 0)),
        out_shape=jax.ShapeDtypeStruct((s, BRANCH_WIDTH), BF16),
        scratch_shapes=[pltpu.VMEM((BRANCH_WIDTH, GLA_KW), F32),
                        pltpu.VMEM((t, GLA_KW), F32),
                        pltpu.VMEM((t, BRANCH_WIDTH), F32)],
        compiler_params=_params(("arbitrary",)),
    )(p, p, p, p, tail, aup_pad, ab_row, g_row)


def _layer_norm(z, g, b):
    mu = jnp.mean(z, axis=1, keepdims=True)
    zc = z - mu
    var = jnp.mean(zc * zc, axis=1, keepdims=True)
    return zc * lax.rsqrt(var + LN_EPS) * g + b


def _store_row_tiles(ref, val):
    n = val.shape[0]
    for a in range(ROW_TILE):
        ref[pl.ds(a, n, stride=ROW_TILE), :] = val[:, a * LANES:(a + 1) * LANES]


def _load_row_tiles(ref, n, base=0):
    return jnp.concatenate(
        [ref[pl.ds(base + a, n, stride=ROW_TILE), :] for a in range(ROW_TILE)], axis=1)


def _merge_kernel(fox_ref, gla_ref, moba_ref, g0_ref, g1_ref, g2_ref, x_ref, wb_ref, wo_ref,
                  lg_ref, lb_ref, wrh_ref, wrl_ref, br_ref,
                  x1_ref, sel_ref, idx_ref, gw_ref):
    tm = x_ref.shape[0]

    def heads(ref):
        return jnp.concatenate([ref[h] for h in range(N_HEADS)], axis=1)

    merged = _sigmoid(g0_ref[...]) * _dot(heads(fox_ref), wb_ref[0]).astype(BF16)
    merged += _sigmoid(g1_ref[...]) * _dot(gla_ref[...], wb_ref[1]).astype(BF16)
    merged += _sigmoid(g2_ref[...]) * _dot(heads(moba_ref), wb_ref[2]).astype(BF16)
    z = ALPHA * x_ref[...] + _dot(merged, wo_ref[...])
    x1 = _layer_norm(z, lg_ref[...], lb_ref[...])
    _store_row_tiles(x1_ref, x1)

    x_hi, x_lo = _split2(x1)
    both = _dot(x_hi, wrl_ref[...])
    logits = both[:, :LANES] + both[:, LANES:] + _dot(x_lo, wrh_ref[...]) + br_ref[...]
    lane = _iota((tm, LANES), 1)
    lg = jnp.where(lane < N_EXPERTS, logits, NEG)
    sel = jnp.zeros((tm, LANES), F32)
    idxf = jnp.zeros((tm, LANES), F32)
    ew = jnp.zeros((tm, LANES), F32)
    top = None
    for r in range(TOP_K):
        mx = jnp.max(lg, axis=1, keepdims=True)
        idx = jnp.min(jnp.where(lg == mx, lane, LANES), axis=1, keepdims=True)
        hit = lane == idx
        top = mx if top is None else top
        sel = jnp.where(hit, 1.0, sel)
        idxf = jnp.where(lane == r, idx.astype(F32), idxf)
        ew = jnp.where(lane == r, jnp.exp(mx - top), ew)
        lg = jnp.where(hit, NEG, lg)
    sel_ref[...] = sel
    idx_ref[...] = idxf
    gw_ref[...] = ew / jnp.sum(ew, axis=1, keepdims=True)


def _merge(fox_o, y_gla, moba_o, p, x, wb, wo, lg, lb, wrh, wrl, br, tm=256):
    s = x.shape[0]
    head_spec = pl.BlockSpec((N_HEADS, tm, HEAD_DIM), lambda i: (0, i, 0))
    row = lambda n: pl.BlockSpec((1, n), lambda i: (0, 0))
    small = jax.ShapeDtypeStruct((s, LANES), F32)
    small_spec = pl.BlockSpec((tm, LANES), lambda i: (i, 0))
    return pl.pallas_call(
        _merge_kernel,
        grid=(s // tm,),
        in_specs=[head_spec,
                  pl.BlockSpec((tm, BRANCH_WIDTH), lambda i: (i, 0)),
                  head_spec,
                  pl.BlockSpec((tm, D_MODEL), lambda i: (i, 0)),
                  pl.BlockSpec((tm, D_MODEL), lambda i: (i, 1)),
                  pl.BlockSpec((tm, D_MODEL), lambda i: (i, 2)),
                  pl.BlockSpec((tm, D_MODEL), lambda i: (i, 0)),
                  pl.BlockSpec((3, BRANCH_WIDTH, D_MODEL), lambda i: (0, 0, 0)),
                  pl.BlockSpec((D_MODEL, D_MODEL), lambda i: (0, 0)),
                  row(D_MODEL), row(D_MODEL),
                  pl.BlockSpec((D_MODEL, LANES), lambda i: (0, 0)),
                  pl.BlockSpec((D_MODEL, 2 * LANES), lambda i: (0, 0)),
                  row(LANES)],
        out_specs=[pl.BlockSpec((tm * ROW_TILE, LANES), lambda i: (i, 0)),
                   small_spec, small_spec, small_spec],
        out_shape=[jax.ShapeDtypeStruct((s * ROW_TILE, LANES), F32), small, small, small],
        compiler_params=_params(("parallel",)),
    )(fox_o, y_gla, moba_o, p, p, p, x, wb, wo, lg, lb, wrh, wrl, br)


def _rank_kernel(sel_ref, idx_ref, dest_ref, cnt_ref, carry_ref, start_ref):
    phase = pl.program_id(0)
    i = pl.program_id(1)
    t = sel_ref.shape[0]
    sel = sel_ref[...]
    chosen = jnp.sum(sel, axis=0, keepdims=True)

    @pl.when(jnp.logical_and(phase == 0, i == 0))
    def _():
        carry_ref[...] = jnp.zeros_like(carry_ref)

    @pl.when(jnp.logical_and(phase == 1, i == 0))
    def _():
        counts = carry_ref[...]
        cnt_ref[...] = counts
        blocks = jnp.floor((counts + (EXPERT_BLOCK - 1)) * (1.0 / EXPERT_BLOCK))
        before = (_iota((LANES, LANES), 0) < _iota((LANES, LANES), 1)).astype(BF16)
        first = _dot(jnp.broadcast_to(blocks, (ROW_TILE, LANES)).astype(BF16), before)
        start_ref[...] = first[0:1, :] * float(EXPERT_BLOCK)
        carry_ref[...] = jnp.zeros_like(carry_ref)

    @pl.when(phase == 1)
    def _():
        stri = (_iota((t, t), 0) > _iota((t, t), 1)).astype(BF16)
        pos = _dot(stri, sel.astype(BF16)) + carry_ref[...] + start_ref[...]
        lane = _iota((t, LANES), 1).astype(F32)
        idxf = idx_ref[...]
        dest = jnp.zeros((t, LANES), F32)
        for r in range(TOP_K):
            mine = jnp.sum(jnp.where(lane == idxf[:, r:r + 1], pos, 0.0), axis=1, keepdims=True)
            dest = jnp.where(lane == float(r), mine, dest)
        dest_ref[...] = dest.T[:ROW_TILE, :].astype(jnp.int32)

    carry_ref[...] += chosen


def _rank(sel, idxf, t=512):
    s = sel.shape[0]
    spec = pl.BlockSpec((t, LANES), lambda ph, i: (i, 0))
    return pl.pallas_call(
        _rank_kernel,
        grid=(2, s // t),
        in_specs=[spec, spec],
        out_specs=[pl.BlockSpec((ROW_TILE, t), lambda ph, i: (0, i * ph)),
                   pl.BlockSpec((1, LANES), lambda ph, i: (0, 0))],
        out_shape=[jax.ShapeDtypeStruct((ROW_TILE, s), jnp.int32),
                   jax.ShapeDtypeStruct((1, LANES), F32)],
        scratch_shapes=[pltpu.VMEM((1, LANES), F32), pltpu.VMEM((1, LANES), F32)],
        compiler_params=_params(("arbitrary", "arbitrary")),
    )(sel, idxf)


DMA_UNROLL = 4


def _token_rows(ref, token):
    return ref.at[pl.ds(pl.multiple_of(token * ROW_TILE, ROW_TILE), ROW_TILE), :]


def _dispatch_kernel(dest_ref, x_ref, xin_ref, sem):
    n = dest_ref.shape[1]

    def issue(g, _):
        for u in range(DMA_UNROLL):
            t = g * DMA_UNROLL + u
            for k in range(TOP_K):
                pltpu.make_async_copy(_token_rows(x_ref, t), _token_rows(xin_ref, dest_ref[k, t]),
                                      sem).start(priority=k % 2)
        return 0

    lax.fori_loop(0, n // DMA_UNROLL, issue, 0)
    for k in range(TOP_K):
        pltpu.make_async_copy(x_ref, xin_ref.at[pl.ds(0, n * ROW_TILE), :], sem).wait()


def _dispatch(dest, x1_2d, n_rows, td=512):
    s = x1_2d.shape[0] // ROW_TILE
    return pl.pallas_call(
        _dispatch_kernel,
        grid=(s // td,),
        in_specs=[pl.BlockSpec((ROW_TILE, td), lambda i: (0, i), memory_space=pltpu.SMEM),
                  pl.BlockSpec((td * ROW_TILE, LANES), lambda i: (i, 0))],
        out_specs=pl.BlockSpec(memory_space=pl.ANY),
        out_shape=jax.ShapeDtypeStruct((n_rows * ROW_TILE, LANES), F32),
        scratch_shapes=[pltpu.SemaphoreType.DMA(())],
        compiler_params=pltpu.CompilerParams(dimension_semantics=("arbitrary",),
                                             has_side_effects=True),
    )(dest, x1_2d)


HALF = LANES // 2
W_CAST_ROWS = 64


def _expert_kernel(be_ref, nv_ref, xin_ref, wgu_ref, bgu_ref, wdn_ref, bdn_ref, y_ref,
                   wgu_s, wdn_s, perm_s):
    b = pl.program_id(0)
    blk = y_ref.shape[0] // ROW_TILE
    nvalid = nv_ref[b]
    new_expert = jnp.logical_or(b == 0, be_ref[b] != be_ref[jnp.maximum(b - 1, 0)])

    @pl.when(jnp.logical_and(new_expert, nvalid > 0))
    def _():
        def cast(r, _):
            r0 = pl.multiple_of(r * W_CAST_ROWS, W_CAST_ROWS)
            wgu_s[pl.ds(r0, W_CAST_ROWS), :] = wgu_ref[0, pl.ds(r0, W_CAST_ROWS), :].astype(BF16)
            return 0

        lax.fori_loop(0, D_MODEL // W_CAST_ROWS, cast, 0)
        for c in range(D_MODEL // LANES):
            cols = slice(c * LANES, (c + 1) * LANES)
            for g in range(D_MODEL // LANES):
                lo = g * LANES
                perm_s[c, pl.ds(lo, HALF, stride=2), :] = wdn_ref[0, lo:lo + HALF, cols]
                perm_s[c, pl.ds(lo + 1, HALF, stride=2), :] = wdn_ref[0, lo + HALF:lo + LANES, cols]
            wdn_s[:, cols] = perm_s[c].astype(BF16)

    @pl.when(nvalid > 0)
    def _():
        x = _load_row_tiles(xin_ref, blk)
        x = jnp.where(_iota((blk, D_MODEL), 0) < nvalid, x, 0.0).astype(BF16)
        even = _iota((blk, LANES), 1) % 2 == 0
        acts = []
        for g in range(D_MODEL // LANES):
            lo = g * 2 * LANES
            h = _dot(x, wgu_s[:, lo:lo + 2 * LANES]) + bgu_ref[0, :, lo:lo + 2 * LANES]
            h_a = h[:, :LANES]
            h_b = h[:, LANES:]
            gate = jnp.where(even, h_a, pltpu.roll(h_b, 1, 1))
            up = jnp.where(even, pltpu.roll(h_a, LANES - 1, 1), h_b)
            gate = jnp.minimum(gate, SWIGLU_LIMIT)
            up = jnp.clip(up, -SWIGLU_LIMIT, SWIGLU_LIMIT)
            acts.append(((up + 1.0) * (gate * _sigmoid(SWIGLU_ALPHA * gate))).astype(BF16))
        act = jnp.concatenate(acts, axis=1)
        _store_row_tiles(y_ref, _dot(act, wdn_s[...]) + bdn_ref[0])

    @pl.when(nvalid <= 0)
    def _():
        y_ref[...] = jnp.zeros_like(y_ref)


def _experts(block_expert, block_valid, xin2d, w_gu, b_gu, w_dn, b_dn, layer):
    blk = EXPERT_BLOCK
    nb = block_expert.shape[0]
    x_spec = pl.BlockSpec((blk * ROW_TILE, LANES), lambda b, be, nv: (b, 0))
    which = lambda b, be, nv: (layer * N_EXPERTS + be[b], 0, 0)
    return pl.pallas_call(
        _expert_kernel,
        grid_spec=pltpu.PrefetchScalarGridSpec(
            num_scalar_prefetch=2,
            grid=(nb,),
            in_specs=[x_spec,
                      pl.BlockSpec((1, D_MODEL, 2 * D_MODEL), which),
                      pl.BlockSpec((1, 1, 2 * D_MODEL), which),
                      pl.BlockSpec((1, D_MODEL, D_MODEL), which),
                      pl.BlockSpec((1, 1, D_MODEL), which)],
            out_specs=x_spec,
            scratch_shapes=[pltpu.VMEM((D_MODEL, 2 * D_MODEL), BF16),
                            pltpu.VMEM((D_MODEL, D_MODEL), BF16),
                            pltpu.VMEM((D_MODEL // LANES, D_MODEL, LANES), F32)]),
        out_shape=jax.ShapeDtypeStruct((nb * blk * ROW_TILE, LANES), F32),
        compiler_params=pltpu.CompilerParams(dimension_semantics=("arbitrary",),
                                             vmem_limit_bytes=56 * 1024 * 1024),
    )(block_expert, block_valid, xin2d, w_gu, b_gu, w_dn, b_dn)


def _combine_kernel(dest_ref, next_ref, yb_ref, gw_ref, x1_ref, lg_ref, lb_ref, out_ref,
                    buf_ref, sem):
    i = pl.program_id(0)
    tc = out_ref.shape[0]
    slot = i % 2

    def gather(rows_ref, into):
        def body(g, _):
            for u in range(DMA_UNROLL):
                t = g * DMA_UNROLL + u
                for k in range(TOP_K):
                    pltpu.make_async_copy(_token_rows(yb_ref, rows_ref[k, t]),
                                          _token_rows(buf_ref.at[into, k], t),
                                          sem.at[into]).start(priority=k % 2)
            return 0

        lax.fori_loop(0, tc // DMA_UNROLL, body, 0)

    @pl.when(i == 0)
    def _():
        gather(dest_ref, 0)

    @pl.when(i + 1 < pl.num_programs(0))
    def _():
        gather(next_ref, 1 - slot)

    for k in range(TOP_K):
        pltpu.make_async_copy(yb_ref.at[pl.ds(0, tc * ROW_TILE), :], buf_ref.at[slot, k],
                              sem.at[slot]).wait()

    gw = gw_ref[...]
    f = jnp.zeros((tc, D_MODEL), F32)
    for k in range(TOP_K):
        f += gw[:, k:k + 1] * _load_row_tiles(buf_ref.at[slot, k], tc)
    z = ALPHA * _load_row_tiles(x1_ref, tc) + f
    out_ref[...] = _layer_norm(z, lg_ref[...], lb_ref[...])


def _combine(dest, yb_rows, gw, x1_2d, lg, lb, tc=256):
    s = gw.shape[0]
    n = s // tc
    row = pl.BlockSpec((1, D_MODEL), lambda i: (0, 0))
    return pl.pallas_call(
        _combine_kernel,
        grid=(n,),
        in_specs=[pl.BlockSpec((ROW_TILE, tc), lambda i: (0, i), memory_space=pltpu.SMEM),
                  pl.BlockSpec((ROW_TILE, tc), lambda i: (0, jnp.minimum(i + 1, n - 1)),
                               memory_space=pltpu.SMEM),
                  pl.BlockSpec(memory_space=pl.ANY),
                  pl.BlockSpec((tc, LANES), lambda i: (i, 0)),
                  pl.BlockSpec((tc * ROW_TILE, LANES), lambda i: (i, 0)),
                  row, row],
        out_specs=pl.BlockSpec((tc, D_MODEL), lambda i: (i, 0)),
        out_shape=jax.ShapeDtypeStruct((s, D_MODEL), F32),
        scratch_shapes=[pltpu.VMEM((2, TOP_K, tc * ROW_TILE, LANES), F32),
                        pltpu.SemaphoreType.DMA((2,))],
        compiler_params=_params(("arbitrary",)),
    )(dest, dest, yb_rows, gw, x1_2d, lg, lb)


W_IN_SIZES = (256, 256, 256, 4, 128, 128, 256, 16, 256, 256, 256, 256, 3 * D_MODEL)
W_IN_WIDTH = sum(W_IN_SIZES)


def _w_in_kernel(w_ref, main_ref, tail_ref):
    w = w_ref[0]
    offs = [0]
    for n in W_IN_SIZES:
        offs.append(offs[-1] + n)
    fq, fk, fv, ff, gq, gk, gv, ga, gr, mq, mk, mv, gates = (
        w[:, offs[i]:offs[i + 1]] for i in range(len(W_IN_SIZES)))
    main_ref[...] = jnp.concatenate(
        [gates, fq, fk, fv, mq, mk, mv, gq, gk, gv, gr], axis=1).astype(BF16)
    pad = jnp.zeros((w.shape[0], LANES - 4 - GLA_RANK), F32)
    tail_ref[...] = jnp.concatenate([ff, ga, pad], axis=1).astype(BF16)


def _permute_w_in(w_in, layer, tr=128):
    return pl.pallas_call(
        _w_in_kernel,
        grid=(D_MODEL // tr,),
        in_specs=[pl.BlockSpec((1, tr, W_IN_WIDTH), lambda i: (layer, i, 0))],
        out_specs=[pl.BlockSpec((tr, P_WIDTH), lambda i: (i, 0)),
                   pl.BlockSpec((tr, LANES), lambda i: (i, 0))],
        out_shape=[jax.ShapeDtypeStruct((D_MODEL, P_WIDTH), BF16),
                   jax.ShapeDtypeStruct((D_MODEL, LANES), BF16)],
        compiler_params=_params(("parallel",)),
    )(w_in)


def _pad_row(v, n=LANES, fill=0.0):
    return jnp.pad(v.astype(F32), (0, n - v.shape[0]), constant_values=fill)[None, :]


def _mixer_layer(x, w_in, layer, fox_fb, gla_a_up, gla_a_b, gla_norm_g, w_branch, w_out, ln_g, ln_b,
                 w_router, b_router):
    w_main, w_tail = _permute_w_in(w_in, layer)
    p, tail = _inproj(x, w_main, w_tail)

    s = x.shape[0]
    tq, tk = _flash_tiles(s)
    fox_qt, fox_k, fox_vt, fox_stats = _fox_prep(p, tail, _pad_row(fox_fb))
    fox_o = _flash(_fox_first_tile(fox_stats, s, tq, tk), fox_qt, fox_k, fox_vt, tq, tk)
    moba_o = _flash(jnp.zeros((N_HEADS * (s // tq),), jnp.int32), *_moba_prep(p), tq, tk)
    aup_pad = jnp.zeros((LANES, GLA_KW), F32).at[TAIL_GA:TAIL_GA + GLA_RANK].set(gla_a_up)
    y_gla = _gla(p, tail, aup_pad, gla_a_b[None, :], gla_norm_g[None, :])

    wr = jnp.pad(w_router, ((0, 0), (0, LANES - N_EXPERTS)))
    wr_hi = wr.astype(BF16)
    wr_lo = jnp.concatenate([wr_hi, (wr - wr_hi.astype(F32)).astype(BF16)], axis=1)
    return _merge(fox_o, y_gla, moba_o, p, x, w_branch.astype(BF16), w_out.astype(BF16),
                  ln_g[None, :], ln_b[None, :], wr_hi, wr_lo, _pad_row(b_router))


def _moe_layer(x1_2d, sel, idxf, gw, w_gu, b_gu, w_dn, b_dn, ln_g, ln_b, layer):
    s = sel.shape[0]
    blk = EXPERT_BLOCK
    dest, counts = _rank(sel, idxf)

    counts = counts[0, :N_EXPERTS].astype(jnp.int32)
    padded = (counts + blk - 1) // blk * blk
    pad_end = jnp.cumsum(padded)
    pad_start = pad_end - padded
    n_blocks = (s * TOP_K + N_EXPERTS * (blk - 1)) // blk + 1
    block_start = jnp.arange(n_blocks, dtype=jnp.int32) * blk
    active = block_start < pad_end[-1]
    block_expert = jnp.sum((block_start[:, None] >= pad_end[None, :]).astype(jnp.int32), axis=1)
    last_expert = jnp.max(jnp.where(counts > 0, jnp.arange(N_EXPERTS, dtype=jnp.int32), 0))
    block_expert = jnp.where(active, block_expert, last_expert)
    block_valid = jnp.clip(pad_start[block_expert] + counts[block_expert] - block_start, 0, blk)
    block_valid = jnp.where(active, block_valid, 0).astype(jnp.int32)

    xin = _dispatch(dest, x1_2d, n_blocks * blk)
    yb = _experts(block_expert, block_valid, xin, w_gu, b_gu, w_dn, b_dn, layer)
    return _combine(dest, yb, gw, x1_2d, ln_g[None, :], ln_b[None, :])


def kernel(x, w_in, fox_fb, gla_a_up, gla_a_b, gla_norm_g, w_branch, w_out, ln1_g, ln1_b,
           w_router, b_router, w_gu, b_gu, w_dn, b_dn, ln2_g, ln2_b):
    b, s, d = x.shape
    assert b == 1 and d == D_MODEL
    h = x.reshape(s, d)
    n_le = DEPTH * N_EXPERTS
    w_gu = w_gu.reshape(n_le, D_MODEL, 2 * D_MODEL)
    b_gu = b_gu.reshape(n_le, 1, 2 * D_MODEL)
    w_dn = w_dn.reshape(n_le, D_MODEL, D_MODEL)
    b_dn = b_dn.reshape(n_le, 1, D_MODEL)
    for l in range(DEPTH):
        x1_2d, sel, idxf, gw = _mixer_layer(
            h, w_in, l, fox_fb[l], gla_a_up[l], gla_a_b[l], gla_norm_g[l], w_branch[l], w_out[l],
            ln1_g[l], ln1_b[l], w_router[l], b_router[l])
        h = _moe_layer(x1_2d, sel, idxf, gw, w_gu, b_gu, w_dn, b_dn, ln2_g[l], ln2_b[l], l)
    return h.reshape(b, s, d)
```

```python
import functools

import jax
import jax.numpy as jnp
from jax import lax
from jax.experimental import pallas as pl
from jax.experimental.pallas import tpu as pltpu

F32 = jnp.float32
BF16 = jnp.bfloat16

D_MODEL = 1024
DEPTH = 4
N_HEADS = 4
HEAD_DIM = 64
GLA_DK = 32
GLA_KW = N_HEADS * GLA_DK
GLA_RANK = 16
GLA_TAU = 16.0
MOBA_BLOCK = 256
MOBA_TOPK = 3
BRANCH_WIDTH = 256
N_EXPERTS = 32
TOP_K = 4
SWIGLU_LIMIT = 7.0
SWIGLU_ALPHA = 1.702
ALPHA = (2 * DEPTH) ** 0.25
LN_EPS = 1e-5
RMS_EPS = 1e-6
LOG2E = 1.4426950408889634

LANES = 128
ROW_TILE = 8
NEG = -1e30
UNSELECTED = -32768.0
VMEM_LIMIT = 48 * 1024 * 1024

P_GATES = 0
P_FOX = 3 * D_MODEL
P_MOBA = P_FOX + 3 * BRANCH_WIDTH
P_GLA = P_MOBA + 3 * BRANCH_WIDTH
P_WIDTH = P_GLA + 2 * GLA_KW + 2 * BRANCH_WIDTH
TAIL_FF = 0
TAIL_GA = 4

EXPERT_BLOCK = 512


def _params(sem):
    return pltpu.CompilerParams(dimension_semantics=sem, vmem_limit_bytes=VMEM_LIMIT)


def _split3(x):
    hi = x.astype(BF16)
    r1 = x - hi.astype(F32)
    mid = r1.astype(BF16)
    lo = (r1 - mid.astype(F32)).astype(BF16)
    return hi, mid, lo


def _split2(x):
    hi = x.astype(BF16)
    lo = (x - hi.astype(F32)).astype(BF16)
    return hi, lo


def _dot(a, b):
    return jnp.dot(a, b, preferred_element_type=F32)


def _dot_nt(a, b):
    return lax.dot_general(a, b, (((1,), (1,)), ((), ())), preferred_element_type=F32)


def _dot_tn(a, b):
    return lax.dot_general(a, b, (((0,), (0,)), ((), ())), preferred_element_type=F32)


def _log_sigmoid(t):
    return jnp.minimum(t, 0.0) - jnp.log1p(jnp.exp(-jnp.abs(t)))


def _sigmoid(t):
    return 0.5 * jnp.tanh(0.5 * t) + 0.5


def _iota(shape, axis):
    return lax.broadcasted_iota(jnp.int32, shape, axis)


def _transpose_bf16(x):
    return x.astype(F32).T.astype(BF16)


BF16_ROWS = 16
V_ROWS = HEAD_DIM + BF16_ROWS


def _values_t(v):
    ones_row = jnp.where(_iota((BF16_ROWS, v.shape[0]), 0) == 0, 1.0, 0.0).astype(BF16)
    return jnp.concatenate([_transpose_bf16(v), ones_row], axis=0)


def _aug_specs(s, t):
    aug = jax.ShapeDtypeStruct((N_HEADS, s, LANES), BF16)
    aug_spec = pl.BlockSpec((N_HEADS, t, LANES), lambda i: (0, i, 0))
    aug_t = jax.ShapeDtypeStruct((N_HEADS, LANES, s), BF16)
    aug_t_spec = pl.BlockSpec((N_HEADS, LANES, t), lambda i: (0, 0, i))
    aug_v = jax.ShapeDtypeStruct((N_HEADS, V_ROWS, s), BF16)
    aug_v_spec = pl.BlockSpec((N_HEADS, V_ROWS, t), lambda i: (0, 0, i))
    return aug, aug_spec, aug_t, aug_t_spec, aug_v, aug_v_spec


def _inproj_kernel(x_ref, w_ref, wt_ref, sc_ref, p_ref, t_ref, xb_ref):
    @pl.when(pl.program_id(1) == 0)
    def _():
        xb_ref[...] = x_ref[...].astype(BF16)
        t_ref[...] = _dot(xb_ref[...], wt_ref[...])

    p_ref[...] = (_dot(xb_ref[...], w_ref[...]) * sc_ref[...]).astype(BF16)


def _inproj(x, w, wt, tm=1024, tn=1792):
    s = x.shape[0]
    col = jnp.arange(P_WIDTH)
    is_q = ((col >= P_FOX) & (col < P_FOX + BRANCH_WIDTH)) | ((col >= P_MOBA) & (col < P_MOBA + BRANCH_WIDTH))
    scale = jnp.where(is_q, HEAD_DIM ** -0.5 * LOG2E, 1.0).astype(F32)[None, :]
    return pl.pallas_call(
        _inproj_kernel,
        grid=(s // tm, P_WIDTH // tn),
        in_specs=[pl.BlockSpec((tm, D_MODEL), lambda i, j: (i, 0)),
                  pl.BlockSpec((D_MODEL, tn), lambda i, j: (0, j)),
                  pl.BlockSpec((D_MODEL, LANES), lambda i, j: (0, 0)),
                  pl.BlockSpec((1, tn), lambda i, j: (0, j))],
        out_specs=[pl.BlockSpec((tm, tn), lambda i, j: (i, j)),
                   pl.BlockSpec((tm, LANES), lambda i, j: (i, 0))],
        out_shape=[jax.ShapeDtypeStruct((s, P_WIDTH), BF16),
                   jax.ShapeDtypeStruct((s, LANES), F32)],
        scratch_shapes=[pltpu.VMEM((tm, D_MODEL), BF16)],
        compiler_params=_params(("parallel", "arbitrary")),
    )(x, w, wt, scale)


def _fox_prep_kernel(q_ref, k_ref, v_ref, t_ref, fb_ref, qa_ref, ka_ref, va_ref, st_ref, carry_ref):
    t = q_ref.shape[0]

    @pl.when(pl.program_id(0) == 0)
    def _():
        carry_ref[...] = jnp.zeros_like(carry_ref)

    ls = _log_sigmoid(t_ref[...] + fb_ref[...])
    tri = (_iota((t, t), 0) >= _iota((t, t), 1)).astype(BF16)
    hi, mid, lo = _split3(ls)
    c = _dot(tri, hi) + _dot(tri, mid) + _dot(tri, lo) + carry_ref[...]
    carry_ref[...] = c[t - 1:t, :]

    c2 = LOG2E * c

    lane = _iota((t, HEAD_DIM), 1)
    q_aug = jnp.where(_iota((HEAD_DIM, t), 0) < 3, 1.0, 0.0).astype(BF16)
    q = q_ref[...]
    k = k_ref[...]
    v = v_ref[...]
    srow = _iota((ROW_TILE, LANES), 0)
    slane = _iota((ROW_TILE, LANES), 1)
    stat = jnp.where(srow == 2, c2[0:1, :], jnp.where(srow == 3, c2[t - 1:t, :], 0.0))
    for h in range(N_HEADS):
        sl = slice(h * HEAD_DIM, (h + 1) * HEAD_DIM)
        nhi, nmid, nlo = (piece.astype(F32) for piece in _split3(-c2[:, h:h + 1]))
        k_aug = jnp.where(lane == 0, nhi, jnp.where(lane == 1, nmid, jnp.where(lane == 2, nlo, 0.0)))
        qa_ref[h] = jnp.concatenate([_transpose_bf16(q[:, sl]), q_aug], axis=0)
        ka_ref[h] = jnp.concatenate([k[:, sl], k_aug.astype(BF16)], axis=1)
        va_ref[h] = _values_t(v[:, sl])
        for r, x in ((0, q), (1, k)):
            xf = x[:, sl].astype(F32)
            norm = jnp.sqrt(jnp.max(jnp.sum(xf * xf, axis=1, keepdims=True), axis=0, keepdims=True))
            stat = jnp.where(jnp.logical_and(srow == r, slane == h), norm, stat)
    st_ref[0] = stat


def _fox_prep(p, tail, fb_row, t=512):
    s = p.shape[0]
    cb = P_FOX // BRANCH_WIDTH
    aug, aug_spec, aug_t, aug_t_spec, aug_v, aug_v_spec = _aug_specs(s, t)
    return pl.pallas_call(
        _fox_prep_kernel,
        grid=(s // t,),
        in_specs=[pl.BlockSpec((t, BRANCH_WIDTH), lambda i: (i, cb)),
                  pl.BlockSpec((t, BRANCH_WIDTH), lambda i: (i, cb + 1)),
                  pl.BlockSpec((t, BRANCH_WIDTH), lambda i: (i, cb + 2)),
                  pl.BlockSpec((t, LANES), lambda i: (i, 0)),
                  pl.BlockSpec((1, LANES), lambda i: (0, 0))],
        out_specs=[aug_t_spec, aug_spec, aug_v_spec,
                   pl.BlockSpec((1, ROW_TILE, LANES), lambda i: (i, 0, 0))],
        out_shape=[aug_t, aug, aug_v, jax.ShapeDtypeStruct((s // t, ROW_TILE, LANES), F32)],
        scratch_shapes=[pltpu.VMEM((1, LANES), F32)],
        compiler_params=_params(("arbitrary",)),
    )(p, p, p, tail, fb_row)


SKIP_LOG2 = 48.0


def _fox_first_tile(stats, s, tq, tk):
    tp = s // stats.shape[0]
    qn, kn, c_first, c_last = (stats[:, r, :N_HEADS] for r in range(4))
    nq, nk = s // tq, s // tk
    qn = jnp.max(qn.reshape(nq, tq // tp, N_HEADS), axis=1)
    c_q = c_first.reshape(nq, tq // tp, N_HEADS)[:, 0, :]
    c_k = c_last.reshape(nk, tk // tp, N_HEADS)[:, -1, :]
    bound = (2.0 * qn * jnp.max(kn, axis=0))[:, None, :] + c_q[:, None, :] - c_k[None, :, :]
    j = jnp.arange(nk, dtype=jnp.int32)[None, :, None]
    first = jnp.min(jnp.where(bound >= -SKIP_LOG2, j, nk), axis=1)
    return first.T.reshape(-1).astype(jnp.int32)


def _moba_prep_kernel(q_ref, k_ref, v_ref, qa_ref, ka_ref, va_ref, kmean_ref):
    i = pl.program_id(0)
    t = q_ref.shape[0]
    nbl = kmean_ref.shape[0]

    @pl.when(i == 0)
    def _():
        kmean_ref[...] = jnp.zeros_like(kmean_ref)

    q = q_ref[...]
    k = k_ref[...]
    v = v_ref[...]
    km = kmean_ref[...]
    k_aug = jnp.where(_iota((t, HEAD_DIM), 1) == i, 1.0, 0.0).astype(BF16)
    blk = _iota((HEAD_DIM, t), 0)
    for h in range(N_HEADS):
        sl = slice(h * HEAD_DIM, (h + 1) * HEAD_DIM)
        qt = _transpose_bf16(q[:, sl])
        kh_hi, kh_lo = _split2(km[:HEAD_DIM, sl])
        gate = _dot(kh_hi, qt) + _dot(kh_lo, qt)
        g = jnp.where(blk < i, gate, NEG)
        bias = jnp.where(blk == i, 0.0, UNSELECTED)
        for _ in range(MOBA_TOPK):
            mx = jnp.max(g, axis=0, keepdims=True)
            idx = jnp.min(jnp.where(g == mx, blk, HEAD_DIM), axis=0, keepdims=True)
            pick = jnp.logical_and(blk == idx, mx > 0.5 * NEG)
            bias = jnp.where(pick, 0.0, bias)
            g = jnp.where(blk == idx, NEG, g)
        qa_ref[h] = jnp.concatenate([qt, bias.astype(BF16)], axis=0)
        ka_ref[h] = jnp.concatenate([k[:, sl], k_aug], axis=1)
        va_ref[h] = _values_t(v[:, sl])
    kmean = jnp.mean(k.astype(F32), axis=0, keepdims=True)
    kmean_ref[...] = jnp.where(_iota((nbl, BRANCH_WIDTH), 0) == i, kmean, km)


def _moba_prep(p):
    s = p.shape[0]
    t = MOBA_BLOCK
    assert s % t == 0 and s // t <= HEAD_DIM, "block one-hot must fit the 64 augmentation lanes"
    cb = P_MOBA // BRANCH_WIDTH
    aug, aug_spec, aug_t, aug_t_spec, aug_v, aug_v_spec = _aug_specs(s, t)
    return pl.pallas_call(
        _moba_prep_kernel,
        grid=(s // t,),
        in_specs=[pl.BlockSpec((t, BRANCH_WIDTH), lambda i: (i, cb)),
                  pl.BlockSpec((t, BRANCH_WIDTH), lambda i: (i, cb + 1)),
                  pl.BlockSpec((t, BRANCH_WIDTH), lambda i: (i, cb + 2))],
        out_specs=[aug_t_spec, aug_spec, aug_v_spec],
        out_shape=[aug_t, aug, aug_v],
        scratch_shapes=[pltpu.VMEM((HEAD_DIM, BRANCH_WIDTH), F32)],
        compiler_params=_params(("arbitrary",)),
    )(p, p, p)


def _flash_kernel(first_ref, qt_ref, k_ref, vt_ref, o_ref, s0_ref, s1_ref, s2_ref, *, tq, tk):
    h = pl.program_id(0)
    i = pl.program_id(1)
    qt = qt_ref[0]
    n_diag = tq // tk
    n_full = i * n_diag

    def scores(j):
        return _dot(k_ref[0, pl.ds(pl.multiple_of(j * tk, tk), tk), :], qt)

    def absorb(j, s, m, acc, masked):
        if masked:
            key = _iota((tk, tq), 0) + j * tk
            qry = _iota((tk, tq), 1) + i * tq
            s = jnp.where(key <= qry, s, NEG)
        m_new = jnp.maximum(m, jnp.max(s, axis=0, keepdims=True))
        p = jnp.exp2(s - m_new).astype(BF16)
        vt = vt_ref[0, :, pl.ds(pl.multiple_of(j * tk, tk), tk)]
        return m_new, jnp.exp2(m - m_new) * acc + _dot(vt, p)

    j0 = jnp.minimum(first_ref[h * pl.num_programs(1) + i], n_full)
    m = jnp.full((1, tq), NEG, F32)
    acc = jnp.zeros((V_ROWS, tq), F32)

    extra = (n_full - j0) % 3

    def plain(j, carry):
        return absorb(j, scores(j), *carry, False)

    m, acc = lax.fori_loop(j0, j0 + extra, plain, (m, acc))
    j0 = j0 + extra

    s0_ref[...] = scores(j0)

    def triple(g, carry):
        j = j0 + 3 * g
        s1_ref[...] = scores(j + 1)
        carry = absorb(j, s0_ref[...], *carry, False)
        s2_ref[...] = scores(j + 2)
        carry = absorb(j + 1, s1_ref[...], *carry, False)
        s0_ref[...] = scores(j + 3)
        return absorb(j + 2, s2_ref[...], *carry, False)

    m, acc = lax.fori_loop(0, (n_full - j0) // 3, triple, (m, acc))
    s = s0_ref[...]
    for d in range(n_diag):
        s_next = scores(n_full + d + 1) if d + 1 < n_diag else None
        m, acc = absorb(n_full + d, s, m, acc, True)
        s = s_next
    o_ref[0] = (acc[:HEAD_DIM, :] / acc[HEAD_DIM:HEAD_DIM + 1, :]).T.astype(o_ref.dtype)


def _flash(first_tile, qt, ka, vt, tq, tk):
    nh, s, _ = ka.shape
    return pl.pallas_call(
        functools.partial(_flash_kernel, tq=tq, tk=tk),
        grid_spec=pltpu.PrefetchScalarGridSpec(
            num_scalar_prefetch=1,
            grid=(nh, s // tq),
            in_specs=[pl.BlockSpec((1, LANES, tq), lambda h, i, f: (h, 0, i)),
                      pl.BlockSpec((1, s, LANES), lambda h, i, f: (h, 0, 0)),
                      pl.BlockSpec((1, V_ROWS, s), lambda h, i, f: (h, 0, 0))],
            out_specs=pl.BlockSpec((1, tq, HEAD_DIM), lambda h, i, f: (h, i, 0)),
            scratch_shapes=[pltpu.VMEM((tk, tq), F32)] * 3),
        out_shape=jax.ShapeDtypeStruct((nh, s, HEAD_DIM), BF16),
        compiler_params=_params(("parallel", "parallel")),
    )(first_tile, qt, ka, vt)


def _flash_tiles(s):
    tq = min(1024, s)
    return tq, min(512, tq)


GLA_SUB = 16
GLA_UNROLL = 4


def _gla_kernel(q_ref, k_ref, v_ref, r_ref, t_ref, aup_ref, ab_ref, g_ref, y_ref,
                st_ref, b_ref, o_ref):
    t = q_ref.shape[0]
    c = GLA_SUB

    @pl.when(pl.program_id(0) == 0)
    def _():
        st_ref[...] = jnp.zeros_like(st_ref)

    t_hi, t_lo = _split2(t_ref[...])
    a_hi, a_lo = _split2(aup_ref[...])
    z = _dot(t_hi, a_hi) + _dot(t_lo, a_hi) + _dot(t_hi, a_lo) + ab_ref[...]
    log_a = _log_sigmoid(z) * (1.0 / GLA_TAU)
    row = _iota((t, t), 0)
    col = _iota((t, t), 1)
    tri = jnp.logical_and(row // c == col // c, row >= col).astype(BF16)
    hi, mid, lo = _split3(log_a)
    b_ref[...] = _dot(tri, hi) + _dot(tri, mid) + _dot(tri, lo)

    expand = (_iota((GLA_KW, BRANCH_WIDTH), 0) // GLA_DK
              == _iota((GLA_KW, BRANCH_WIDTH), 1) // HEAD_DIM).astype(BF16)
    st_mask = (_iota((BRANCH_WIDTH, GLA_KW), 0) // HEAD_DIM
               == _iota((BRANCH_WIDTH, GLA_KW), 1) // GLA_DK).astype(F32)
    srow = _iota((c, GLA_KW), 0)
    scale = GLA_DK ** -0.5

    def step(r0, st):
        qs = q_ref[pl.ds(r0, c), :].astype(F32) * scale
        ks = k_ref[pl.ds(r0, c), :].astype(F32)
        vb = v_ref[pl.ds(r0, c), :]
        vf = vb.astype(F32)
        bs = b_ref[pl.ds(r0, c), :]
        b_last = bs[c - 1:c, :]

        o_inter = _dot_nt((qs * jnp.exp(bs)).astype(BF16), st.astype(BF16))

        pieces = []
        for ss in range(c):
            e = jnp.exp(jnp.where(srow >= ss, bs - bs[ss:ss + 1, :], NEG))
            pieces.append(e * qs * ks[ss:ss + 1, :])
        pm = jnp.concatenate(pieces, axis=0).astype(BF16)
        a = _dot(pm, expand)
        o_intra = a[:c, :] * vf[0:1, :]
        for ss in range(1, c):
            o_intra += a[ss * c:(ss + 1) * c, :] * vf[ss:ss + 1, :]
        o_ref[pl.ds(r0, c), :] = o_inter + o_intra

        ke = (ks * jnp.exp(b_last - bs)).astype(BF16)
        return st * jnp.exp(b_last) + _dot_tn(vb, ke) * st_mask

    def steps(g, _):
        st = st_ref[...]
        for u in range(GLA_UNROLL):
            st = step(pl.multiple_of((g * GLA_UNROLL + u) * c, c), st)
        st_ref[...] = st
        return 0

    lax.fori_loop(0, t // (c * GLA_UNROLL), steps, 0)

    o = o_ref[...]
    ones_bd = (_iota((BRANCH_WIDTH, BRANCH_WIDTH), 0) // HEAD_DIM
               == _iota((BRANCH_WIDTH, BRANCH_WIDTH), 1) // HEAD_DIM).astype(BF16)
    sq_hi, sq_mid, sq_lo = _split3(o * o)
    ms = (_dot(sq_hi, ones_bd) + _dot(sq_mid, ones_bd) + _dot(sq_lo, ones_bd)) * (1.0 / HEAD_DIM)
    gr = r_ref[...].astype(F32)
    y = o * lax.rsqrt(ms + RMS_EPS) * g_ref[...] * (gr * _sigmoid(gr))
    y_ref[...] = y.astype(y_ref.dtype)


def _gla(p, tail, aup_pad, ab_row, g_row, t=512):
    s = p.shape[0]
    cq = P_GLA // GLA_KW
    cv = (P_GLA + 2 * GLA_KW) // BRANCH_WIDTH
    return pl.pallas_call(
        _gla_kernel,
        grid=(s // t,),
        in_specs=[pl.BlockSpec((t, GLA_KW), lambda i: (i, cq)),
                  pl.BlockSpec((t, GLA_KW), lambda i: (i, cq + 1)),
                  pl.BlockSpec((t, BRANCH_WIDTH), lambda i: (i, cv)),
                  pl.BlockSpec((t, BRANCH_WIDTH), lambda i: (i, cv + 1)),
                  pl.BlockSpec((t, LANES), lambda i: (i, 0)),
                  pl.BlockSpec((LANES, GLA_KW), lambda i: (0, 0)),
                  pl.BlockSpec((1, GLA_KW), lambda i: (0, 0)),
                  pl.BlockSpec((1, BRANCH_WIDTH), lambda i: (0, 0))],
        out_specs=pl.BlockSpec((t, BRANCH_WIDTH), lambda i: (i, 0)),
        out_shape=jax.ShapeDtypeStruct((s, BRANCH_WIDTH), BF16),
        scratch_shapes=[pltpu.VMEM((BRANCH_WIDTH, GLA_KW), F32),
                        pltpu.VMEM((t, GLA_KW), F32),
                        pltpu.VMEM((t, BRANCH_WIDTH), F32)],
        compiler_params=_params(("arbitrary",)),
    )(p, p, p, p, tail, aup_pad, ab_row, g_row)


def _layer_norm(z, g, b):
    mu = jnp.mean(z, axis=1, keepdims=True)
    zc = z - mu
    var = jnp.mean(zc * zc, axis=1, keepdims=True)
    return zc * lax.rsqrt(var + LN_EPS) * g + b


def _store_row_tiles(ref, val):
    n = val.shape[0]
    for a in range(ROW_TILE):
        ref[pl.ds(a, n, stride=ROW_TILE), :] = val[:, a * LANES:(a + 1) * LANES]


def _load_row_tiles(ref, n, base=0):
    return jnp.concatenate(
        [ref[pl.ds(base + a, n, stride=ROW_TILE), :] for a in range(ROW_TILE)], axis=1)


def _merge_kernel(fox_ref, gla_ref, moba_ref, g0_ref, g1_ref, g2_ref, x_ref, wb_ref, wo_ref,
                  lg_ref, lb_ref, wrh_ref, wrl_ref, br_ref,
                  x1_ref, sel_ref, idx_ref, gw_ref):
    tm = x_ref.shape[0]

    def heads(ref):
        return jnp.concatenate([ref[h] for h in range(N_HEADS)], axis=1)

    merged = _sigmoid(g0_ref[...]) * _dot(heads(fox_ref), wb_ref[0]).astype(BF16)
    merged += _sigmoid(g1_ref[...]) * _dot(gla_ref[...], wb_ref[1]).astype(BF16)
    merged += _sigmoid(g2_ref[...]) * _dot(heads(moba_ref), wb_ref[2]).astype(BF16)
    z = ALPHA * x_ref[...] + _dot(merged, wo_ref[...])
    x1 = _layer_norm(z, lg_ref[...], lb_ref[...])
    _store_row_tiles(x1_ref, x1)

    x_hi, x_lo = _split2(x1)
    both = _dot(x_hi, wrl_ref[...])
    logits = both[:, :LANES] + both[:, LANES:] + _dot(x_lo, wrh_ref[...]) + br_ref[...]
    lane = _iota((tm, LANES), 1)
    lg = jnp.where(lane < N_EXPERTS, logits, NEG)
    sel = jnp.zeros((tm, LANES), F32)
    idxf = jnp.zeros((tm, LANES), F32)
    ew = jnp.zeros((tm, LANES), F32)
    top = None
    for r in range(TOP_K):
        mx = jnp.max(lg, axis=1, keepdims=True)
        idx = jnp.min(jnp.where(lg == mx, lane, LANES), axis=1, keepdims=True)
        hit = lane == idx
        top = mx if top is None else top
        sel = jnp.where(hit, 1.0, sel)
        idxf = jnp.where(lane == r, idx.astype(F32), idxf)
        ew = jnp.where(lane == r, jnp.exp(mx - top), ew)
        lg = jnp.where(hit, NEG, lg)
    sel_ref[...] = sel
    idx_ref[...] = idxf
    gw_ref[...] = ew / jnp.sum(ew, axis=1, keepdims=True)


def _merge(fox_o, y_gla, moba_o, p, x, wb, wo, lg, lb, wrh, wrl, br, tm=256):
    s = x.shape[0]
    head_spec = pl.BlockSpec((N_HEADS, tm, HEAD_DIM), lambda i: (0, i, 0))
    row = lambda n: pl.BlockSpec((1, n), lambda i: (0, 0))
    small = jax.ShapeDtypeStruct((s, LANES), F32)
    small_spec = pl.BlockSpec((tm, LANES), lambda i: (i, 0))
    return pl.pallas_call(
        _merge_kernel,
        grid=(s // tm,),
        in_specs=[head_spec,
                  pl.BlockSpec((tm, BRANCH_WIDTH), lambda i: (i, 0)),
                  head_spec,
                  pl.BlockSpec((tm, D_MODEL), lambda i: (i, 0)),
                  pl.BlockSpec((tm, D_MODEL), lambda i: (i, 1)),
                  pl.BlockSpec((tm, D_MODEL), lambda i: (i, 2)),
                  pl.BlockSpec((tm, D_MODEL), lambda i: (i, 0)),
                  pl.BlockSpec((3, BRANCH_WIDTH, D_MODEL), lambda i: (0, 0, 0)),
                  pl.BlockSpec((D_MODEL, D_MODEL), lambda i: (0, 0)),
                  row(D_MODEL), row(D_MODEL),
                  pl.BlockSpec((D_MODEL, LANES), lambda i: (0, 0)),
                  pl.BlockSpec((D_MODEL, 2 * LANES), lambda i: (0, 0)),
                  row(LANES)],
        out_specs=[pl.BlockSpec((tm * ROW_TILE, LANES), lambda i: (i, 0)),
                   small_spec, small_spec, small_spec],
        out_shape=[jax.ShapeDtypeStruct((s * ROW_TILE, LANES), F32), small, small, small],
        compiler_params=_params(("parallel",)),
    )(fox_o, y_gla, moba_o, p, p, p, x, wb, wo, lg, lb, wrh, wrl, br)


def _rank_kernel(sel_ref, idx_ref, dest_ref, cnt_ref, carry_ref, start_ref):
    phase = pl.program_id(0)
    i = pl.program_id(1)
    t = sel_ref.shape[0]
    sel = sel_ref[...]
    chosen = jnp.sum(sel, axis=0, keepdims=True)

    @pl.when(jnp.logical_and(phase == 0, i == 0))
    def _():
        carry_ref[...] = jnp.zeros_like(carry_ref)

    @pl.when(jnp.logical_and(phase == 1, i == 0))
    def _():
        counts = carry_ref[...]
        cnt_ref[...] = counts
        blocks = jnp.floor((counts + (EXPERT_BLOCK - 1)) * (1.0 / EXPERT_BLOCK))
        before = (_iota((LANES, LANES), 0) < _iota((LANES, LANES), 1)).astype(BF16)
        first = _dot(jnp.broadcast_to(blocks, (ROW_TILE, LANES)).astype(BF16), before)
        start_ref[...] = first[0:1, :] * float(EXPERT_BLOCK)
        carry_ref[...] = jnp.zeros_like(carry_ref)

    @pl.when(phase == 1)
    def _():
        stri = (_iota((t, t), 0) > _iota((t, t), 1)).astype(BF16)
        pos = _dot(stri, sel.astype(BF16)) + carry_ref[...] + start_ref[...]
        lane = _iota((t, LANES), 1).astype(F32)
        idxf = idx_ref[...]
        dest = jnp.zeros((t, LANES), F32)
        for r in range(TOP_K):
            mine = jnp.sum(jnp.where(lane == idxf[:, r:r + 1], pos, 0.0), axis=1, keepdims=True)
            dest = jnp.where(lane == float(r), mine, dest)
        dest_ref[...] = dest.T[:ROW_TILE, :].astype(jnp.int32)

    carry_ref[...] += chosen


def _rank(sel, idxf, t=512):
    s = sel.shape[0]
    spec = pl.BlockSpec((t, LANES), lambda ph, i: (i, 0))
    return pl.pallas_call(
        _rank_kernel,
        grid=(2, s // t),
        in_specs=[spec, spec],
        out_specs=[pl.BlockSpec((ROW_TILE, t), lambda ph, i: (0, i * ph)),
                   pl.BlockSpec((1, LANES), lambda ph, i: (0, 0))],
        out_shape=[jax.ShapeDtypeStruct((ROW_TILE, s), jnp.int32),
                   jax.ShapeDtypeStruct((1, LANES), F32)],
        scratch_shapes=[pltpu.VMEM((1, LANES), F32), pltpu.VMEM((1, LANES), F32)],
        compiler_params=_params(("arbitrary", "arbitrary")),
    )(sel, idxf)


DMA_UNROLL = 4


def _token_rows(ref, token):
    return ref.at[pl.ds(pl.multiple_of(token * ROW_TILE, ROW_TILE), ROW_TILE), :]


def _dispatch_kernel(dest_ref, x_ref, xin_ref, sem):
    n = dest_ref.shape[1]

    def issue(g, _):
        for u in range(DMA_UNROLL):
            t = g * DMA_UNROLL + u
            for k in range(TOP_K):
                pltpu.make_async_copy(_token_rows(x_ref, t), _token_rows(xin_ref, dest_ref[k, t]),
                                      sem).start(priority=k % 2)
        return 0

    lax.fori_loop(0, n // DMA_UNROLL, issue, 0)
    for k in range(TOP_K):
        pltpu.make_async_copy(x_ref, xin_ref.at[pl.ds(0, n * ROW_TILE), :], sem).wait()


def _dispatch(dest, x1_2d, n_rows, td=512):
    s = x1_2d.shape[0] // ROW_TILE
    return pl.pallas_call(
        _dispatch_kernel,
        grid=(s // td,),
        in_specs=[pl.BlockSpec((ROW_TILE, td), lambda i: (0, i), memory_space=pltpu.SMEM),
                  pl.BlockSpec((td * ROW_TILE, LANES), lambda i: (i, 0))],
        out_specs=pl.BlockSpec(memory_space=pl.ANY),
        out_shape=jax.ShapeDtypeStruct((n_rows * ROW_TILE, LANES), F32),
        scratch_shapes=[pltpu.SemaphoreType.DMA(())],
        compiler_params=pltpu.CompilerParams(dimension_semantics=("arbitrary",),
                                             has_side_effects=True),
    )(dest, x1_2d)


HALF = LANES // 2
W_CAST_ROWS = 64


def _expert_kernel(be_ref, nv_ref, xin_ref, wgu_ref, bgu_ref, wdn_ref, bdn_ref, y_ref,
                   wgu_s, wdn_s, perm_s):
    b = pl.program_id(0)
    blk = y_ref.shape[0] // ROW_TILE
    nvalid = nv_ref[b]
    new_expert = jnp.logical_or(b == 0, be_ref[b] != be_ref[jnp.maximum(b - 1, 0)])

    @pl.when(jnp.logical_and(new_expert, nvalid > 0))
    def _():
        def cast(r, _):
            r0 = pl.multiple_of(r * W_CAST_ROWS, W_CAST_ROWS)
            wgu_s[pl.ds(r0, W_CAST_ROWS), :] = wgu_ref[0, pl.ds(r0, W_CAST_ROWS), :].astype(BF16)
            return 0

        lax.fori_loop(0, D_MODEL // W_CAST_ROWS, cast, 0)
        for c in range(D_MODEL // LANES):
            cols = slice(c * LANES, (c + 1) * LANES)
            for g in range(D_MODEL // LANES):
                lo = g * LANES
                perm_s[c, pl.ds(lo, HALF, stride=2), :] = wdn_ref[0, lo:lo + HALF, cols]
                perm_s[c, pl.ds(lo + 1, HALF, stride=2), :] = wdn_ref[0, lo + HALF:lo + LANES, cols]
            wdn_s[:, cols] = perm_s[c].astype(BF16)

    @pl.when(nvalid > 0)
    def _():
        x = _load_row_tiles(xin_ref, blk)
        x = jnp.where(_iota((blk, D_MODEL), 0) < nvalid, x, 0.0).astype(BF16)
        even = _iota((blk, LANES), 1) % 2 == 0
        acts = []
        for g in range(D_MODEL // LANES):
            lo = g * 2 * LANES
            h = _dot(x, wgu_s[:, lo:lo + 2 * LANES]) + bgu_ref[0, :, lo:lo + 2 * LANES]
            h_a = h[:, :LANES]
            h_b = h[:, LANES:]
            gate = jnp.where(even, h_a, pltpu.roll(h_b, 1, 1))
            up = jnp.where(even, pltpu.roll(h_a, LANES - 1, 1), h_b)
            gate = jnp.minimum(gate, SWIGLU_LIMIT)
            up = jnp.clip(up, -SWIGLU_LIMIT, SWIGLU_LIMIT)
            acts.append(((up + 1.0) * (gate * _sigmoid(SWIGLU_ALPHA * gate))).astype(BF16))
        act = jnp.concatenate(acts, axis=1)
        _store_row_tiles(y_ref, _dot(act, wdn_s[...]) + bdn_ref[0])

    @pl.when(nvalid <= 0)
    def _():
        y_ref[...] = jnp.zeros_like(y_ref)


def _experts(block_expert, block_valid, xin2d, w_gu, b_gu, w_dn, b_dn, layer):
    blk = EXPERT_BLOCK
    nb = block_expert.shape[0]
    x_spec = pl.BlockSpec((blk * ROW_TILE, LANES), lambda b, be, nv: (b, 0))
    which = lambda b, be, nv: (layer * N_EXPERTS + be[b], 0, 0)
    return pl.pallas_call(
        _expert_kernel,
        grid_spec=pltpu.PrefetchScalarGridSpec(
            num_scalar_prefetch=2,
            grid=(nb,),
            in_specs=[x_spec,
                      pl.BlockSpec((1, D_MODEL, 2 * D_MODEL), which),
                      pl.BlockSpec((1, 1, 2 * D_MODEL), which),
                      pl.BlockSpec((1, D_MODEL, D_MODEL), which),
                      pl.BlockSpec((1, 1, D_MODEL), which)],
            out_specs=x_spec,
            scratch_shapes=[pltpu.VMEM((D_MODEL, 2 * D_MODEL), BF16),
                            pltpu.VMEM((D_MODEL, D_MODEL), BF16),
                            pltpu.VMEM((D_MODEL // LANES, D_MODEL, LANES), F32)]),
        out_shape=jax.ShapeDtypeStruct((nb * blk * ROW_TILE, LANES), F32),
        compiler_params=pltpu.CompilerParams(dimension_semantics=("arbitrary",),
                                             vmem_limit_bytes=56 * 1024 * 1024),
    )(block_expert, block_valid, xin2d, w_gu, b_gu, w_dn, b_dn)


def _combine_kernel(dest_ref, next_ref, yb_ref, gw_ref, x1_ref, lg_ref, lb_ref, out_ref,
                    buf_ref, sem):
    i = pl.program_id(0)
    tc = out_ref.shape[0]
    slot = i % 2

    def gather(rows_ref, into):
        def body(g, _):
            for u in range(DMA_UNROLL):
                t = g * DMA_UNROLL + u
                for k in range(TOP_K):
                    pltpu.make_async_copy(_token_rows(yb_ref, rows_ref[k, t]),
                                          _token_rows(buf_ref.at[into, k], t),
                                          sem.at[into]).start(priority=k % 2)
            return 0

        lax.fori_loop(0, tc // DMA_UNROLL, body, 0)

    @pl.when(i == 0)
    def _():
        gather(dest_ref, 0)

    @pl.when(i + 1 < pl.num_programs(0))
    def _():
        gather(next_ref, 1 - slot)

    for k in range(TOP_K):
        pltpu.make_async_copy(yb_ref.at[pl.ds(0, tc * ROW_TILE), :], buf_ref.at[slot, k],
                              sem.at[slot]).wait()

    gw = gw_ref[...]
    f = jnp.zeros((tc, D_MODEL), F32)
    for k in range(TOP_K):
        f += gw[:, k:k + 1] * _load_row_tiles(buf_ref.at[slot, k], tc)
    z = ALPHA * _load_row_tiles(x1_ref, tc) + f
    out_ref[...] = _layer_norm(z, lg_ref[...], lb_ref[...])


def _combine(dest, yb_rows, gw, x1_2d, lg, lb, tc=256):
    s = gw.shape[0]
    n = s // tc
    row = pl.BlockSpec((1, D_MODEL), lambda i: (0, 0))
    return pl.pallas_call(
        _combine_kernel,
        grid=(n,),
        in_specs=[pl.BlockSpec((ROW_TILE, tc), lambda i: (0, i), memory_space=pltpu.SMEM),
                  pl.BlockSpec((ROW_TILE, tc), lambda i: (0, jnp.minimum(i + 1, n - 1)),
                               memory_space=pltpu.SMEM),
                  pl.BlockSpec(memory_space=pl.ANY),
                  pl.BlockSpec((tc, LANES), lambda i: (i, 0)),
                  pl.BlockSpec((tc * ROW_TILE, LANES), lambda i: (i, 0)),
                  row, row],
        out_specs=pl.BlockSpec((tc, D_MODEL), lambda i: (i, 0)),
        out_shape=jax.ShapeDtypeStruct((s, D_MODEL), F32),
        scratch_shapes=[pltpu.VMEM((2, TOP_K, tc * ROW_TILE, LANES), F32),
                        pltpu.SemaphoreType.DMA((2,))],
        compiler_params=_params(("arbitrary",)),
    )(dest, dest, yb_rows, gw, x1_2d, lg, lb)


W_IN_SIZES = (256, 256, 256, 4, 128, 128, 256, 16, 256, 256, 256, 256, 3 * D_MODEL)
W_IN_WIDTH = sum(W_IN_SIZES)


def _w_in_kernel(w_ref, main_ref, tail_ref):
    w = w_ref[0]
    offs = [0]
    for n in W_IN_SIZES:
        offs.append(offs[-1] + n)
    fq, fk, fv, ff, gq, gk, gv, ga, gr, mq, mk, mv, gates = (
        w[:, offs[i]:offs[i + 1]] for i in range(len(W_IN_SIZES)))
    main_ref[...] = jnp.concatenate(
        [gates, fq, fk, fv, mq, mk, mv, gq, gk, gv, gr], axis=1).astype(BF16)
    pad = jnp.zeros((w.shape[0], LANES - 4 - GLA_RANK), F32)
    tail_ref[...] = jnp.concatenate([ff, ga, pad], axis=1).astype(BF16)


def _permute_w_in(w_in, layer, tr=128):
    return pl.pallas_call(
        _w_in_kernel,
        grid=(D_MODEL // tr,),
        in_specs=[pl.BlockSpec((1, tr, W_IN_WIDTH), lambda i: (layer, i, 0))],
        out_specs=[pl.BlockSpec((tr, P_WIDTH), lambda i: (i, 0)),
                   pl.BlockSpec((tr, LANES), lambda i: (i, 0))],
        out_shape=[jax.ShapeDtypeStruct((D_MODEL, P_WIDTH), BF16),
                   jax.ShapeDtypeStruct((D_MODEL, LANES), BF16)],
        compiler_params=_params(("parallel",)),
    )(w_in)


def _pad_row(v, n=LANES, fill=0.0):
    return jnp.pad(v.astype(F32), (0, n - v.shape[0]), constant_values=fill)[None, :]


def _mixer_layer(x, w_in, layer, fox_fb, gla_a_up, gla_a_b, gla_norm_g, w_branch, w_out, ln_g, ln_b,
                 w_router, b_router):
    w_main, w_tail = _permute_w_in(w_in, layer)
    p, tail = _inproj(x, w_main, w_tail)

    s = x.shape[0]
    tq, tk = _flash_tiles(s)
    fox_qt, fox_k, fox_vt, fox_stats = _fox_prep(p, tail, _pad_row(fox_fb))
    fox_o = _flash(_fox_first_tile(fox_stats, s, tq, tk), fox_qt, fox_k, fox_vt, tq, tk)
    moba_o = _flash(jnp.zeros((N_HEADS * (s // tq),), jnp.int32), *_moba_prep(p), tq, tk)
    aup_pad = jnp.zeros((LANES, GLA_KW), F32).at[TAIL_GA:TAIL_GA + GLA_RANK].set(gla_a_up)
    y_gla = _gla(p, tail, aup_pad, gla_a_b[None, :], gla_norm_g[None, :])

    wr = jnp.pad(w_router, ((0, 0), (0, LANES - N_EXPERTS)))
    wr_hi = wr.astype(BF16)
    wr_lo = jnp.concatenate([wr_hi, (wr - wr_hi.astype(F32)).astype(BF16)], axis=1)
    return _merge(fox_o, y_gla, moba_o, p, x, w_branch.astype(BF16), w_out.astype(BF16),
                  ln_g[None, :], ln_b[None, :], wr_hi, wr_lo, _pad_row(b_router))


def _moe_layer(x1_2d, sel, idxf, gw, w_gu, b_gu, w_dn, b_dn, ln_g, ln_b, layer):
    s = sel.shape[0]
    blk = EXPERT_BLOCK
    dest, counts = _rank(sel, idxf)

    counts = counts[0, :N_EXPERTS].astype(jnp.int32)
    padded = (counts + blk - 1) // blk * blk
    pad_end = jnp.cumsum(padded)
    pad_start = pad_end - padded
    n_blocks = (s * TOP_K + N_EXPERTS * (blk - 1)) // blk + 1
    block_start = jnp.arange(n_blocks, dtype=jnp.int32) * blk
    active = block_start < pad_end[-1]
    block_expert = jnp.sum((block_start[:, None] >= pad_end[None, :]).astype(jnp.int32), axis=1)
    last_expert = jnp.max(jnp.where(counts > 0, jnp.arange(N_EXPERTS, dtype=jnp.int32), 0))
    block_expert = jnp.where(active, block_expert, last_expert)
    block_valid = jnp.clip(pad_start[block_expert] + counts[block_expert] - block_start, 0, blk)
    block_valid = jnp.where(active, block_valid, 0).astype(jnp.int32)

    xin = _dispatch(dest, x1_2d, n_blocks * blk)
    yb = _experts(block_expert, block_valid, xin, w_gu, b_gu, w_dn, b_dn, layer)
    return _combine(dest, yb, gw, x1_2d, ln_g[None, :], ln_b[None, :])


def kernel(x, w_in, fox_fb, gla_a_up, gla_a_b, gla_norm_g, w_branch, w_out, ln1_g, ln1_b,
           w_router, b_router, w_gu, b_gu, w_dn, b_dn, ln2_g, ln2_b):
    b, s, d = x.shape
    assert b == 1 and d == D_MODEL
    h = x.reshape(s, d)
    n_le = DEPTH * N_EXPERTS
    w_gu = w_gu.reshape(n_le, D_MODEL, 2 * D_MODEL)
    b_gu = b_gu.reshape(n_le, 1, 2 * D_MODEL)
    w_dn = w_dn.reshape(n_le, D_MODEL, D_MODEL)
    b_dn = b_dn.reshape(n_le, 1, D_MODEL)
    for l in range(DEPTH):
        x1_2d, sel, idxf, gw = _mixer_layer(
            h, w_in, l, fox_fb[l], gla_a_up[l], gla_a_b[l], gla_norm_g[l], w_branch[l], w_out[l],
            ln1_g[l], ln1_b[l], w_router[l], b_router[l])
        h = _moe_layer(x1_2d, sel, idxf, gw, w_gu, b_gu, w_dn, b_dn, ln2_g[l], ln2_b[l], l)
    return h.reshape(b, s, d)
```

```python
import functools

import jax
import jax.numpy as jnp
from jax import lax
from jax.experimental import pallas as pl
from jax.experimental.pallas import tpu as pltpu

F32 = jnp.float32
BF16 = jnp.bfloat16

D_MODEL = 1024
DEPTH = 4
N_HEADS = 4
HEAD_DIM = 64
GLA_DK = 32
GLA_KW = N_HEADS * GLA_DK
GLA_RANK = 16
GLA_TAU = 16.0
MOBA_BLOCK = 256
MOBA_TOPK = 3
BRANCH_WIDTH = 256
N_EXPERTS = 32
TOP_K = 4
SWIGLU_LIMIT = 7.0
SWIGLU_ALPHA = 1.702
ALPHA = (2 * DEPTH) ** 0.25
LN_EPS = 1e-5
RMS_EPS = 1e-6
LOG2E = 1.4426950408889634

LANES = 128
ROW_TILE = 8
NEG = -1e30
UNSELECTED = -32768.0
VMEM_LIMIT = 48 * 1024 * 1024

P_GATES = 0
P_FOX = 3 * D_MODEL
P_MOBA = P_FOX + 3 * BRANCH_WIDTH
P_GLA = P_MOBA + 3 * BRANCH_WIDTH
P_WIDTH = P_GLA + 2 * GLA_KW + 2 * BRANCH_WIDTH
TAIL_FF = 0
TAIL_GA = 4

EXPERT_BLOCK = 512


def _params(sem):
    return pltpu.CompilerParams(dimension_semantics=sem, vmem_limit_bytes=VMEM_LIMIT)


def _split3(x):
    hi = x.astype(BF16)
    r1 = x - hi.astype(F32)
    mid = r1.astype(BF16)
    lo = (r1 - mid.astype(F32)).astype(BF16)
    return hi, mid, lo


def _split2(x):
    hi = x.astype(BF16)
    lo = (x - hi.astype(F32)).astype(BF16)
    return hi, lo


def _dot(a, b):
    return jnp.dot(a, b, preferred_element_type=F32)


def _dot_nt(a, b):
    return lax.dot_general(a, b, (((1,), (1,)), ((), ())), preferred_element_type=F32)


def _dot_tn(a, b):
    return lax.dot_general(a, b, (((0,), (0,)), ((), ())), preferred_element_type=F32)


def _log_sigmoid(t):
    return jnp.minimum(t, 0.0) - jnp.log1p(jnp.exp(-jnp.abs(t)))


def _sigmoid(t):
    return 0.5 * jnp.tanh(0.5 * t) + 0.5


def _iota(shape, axis):
    return lax.broadcasted_iota(jnp.int32, shape, axis)


def _transpose_bf16(x):
    return x.astype(F32).T.astype(BF16)


BF16_ROWS = 16
V_ROWS = HEAD_DIM + BF16_ROWS


def _values_t(v):
    ones_row = jnp.where(_iota((BF16_ROWS, v.shape[0]), 0) == 0, 1.0, 0.0).astype(BF16)
    return jnp.concatenate([_transpose_bf16(v), ones_row], axis=0)


def _aug_specs(s, t):
    aug = jax.ShapeDtypeStruct((N_HEADS, s, LANES), BF16)
    aug_spec = pl.BlockSpec((N_HEADS, t, LANES), lambda i: (0, i, 0))
    aug_t = jax.ShapeDtypeStruct((N_HEADS, LANES, s), BF16)
    aug_t_spec = pl.BlockSpec((N_HEADS, LANES, t), lambda i: (0, 0, i))
    aug_v = jax.ShapeDtypeStruct((N_HEADS, V_ROWS, s), BF16)
    aug_v_spec = pl.BlockSpec((N_HEADS, V_ROWS, t), lambda i: (0, 0, i))
    return aug, aug_spec, aug_t, aug_t_spec, aug_v, aug_v_spec


def _inproj_kernel(x_ref, w_ref, wt_ref, sc_ref, p_ref, t_ref, xb_ref):
    @pl.when(pl.program_id(1) == 0)
    def _():
        xb_ref[...] = x_ref[...].astype(BF16)
        t_ref[...] = _dot(xb_ref[...], wt_ref[...])

    p_ref[...] = (_dot(xb_ref[...], w_ref[...]) * sc_ref[...]).astype(BF16)


def _inproj(x, w, wt, tm=1024, tn=1792):
    s = x.shape[0]
    col = jnp.arange(P_WIDTH)
    is_q = ((col >= P_FOX) & (col < P_FOX + BRANCH_WIDTH)) | ((col >= P_MOBA) & (col < P_MOBA + BRANCH_WIDTH))
    scale = jnp.where(is_q, HEAD_DIM ** -0.5 * LOG2E, 1.0).astype(F32)[None, :]
    return pl.pallas_call(
        _inproj_kernel,
        grid=(s // tm, P_WIDTH // tn),
        in_specs=[pl.BlockSpec((tm, D_MODEL), lambda i, j: (i, 0)),
                  pl.BlockSpec((D_MODEL, tn), lambda i, j: (0, j)),
                  pl.BlockSpec((D_MODEL, LANES), lambda i, j: (0, 0)),
                  pl.BlockSpec((1, tn), lambda i, j: (0, j))],
        out_specs=[pl.BlockSpec((tm, tn), lambda i, j: (i, j)),
                   pl.BlockSpec((tm, LANES), lambda i, j: (i, 0))],
        out_shape=[jax.ShapeDtypeStruct((s, P_WIDTH), BF16),
                   jax.ShapeDtypeStruct((s, LANES), F32)],
        scratch_shapes=[pltpu.VMEM((tm, D_MODEL), BF16)],
        compiler_params=_params(("parallel", "arbitrary")),
    )(x, w, wt, scale)


def _fox_prep_kernel(q_ref, k_ref, v_ref, t_ref, fb_ref, qa_ref, ka_ref, va_ref, st_ref, carry_ref):
    t = q_ref.shape[0]

    @pl.when(pl.program_id(0) == 0)
    def _():
        carry_ref[...] = jnp.zeros_like(carry_ref)

    ls = _log_sigmoid(t_ref[...] + fb_ref[...])
    tri = (_iota((t, t), 0) >= _iota((t, t), 1)).astype(BF16)
    hi, mid, lo = _split3(ls)
    c = _dot(tri, hi) + _dot(tri, mid) + _dot(tri, lo) + carry_ref[...]
    carry_ref[...] = c[t - 1:t, :]

    c2 = LOG2E * c

    lane = _iota((t, HEAD_DIM), 1)
    q_aug = jnp.where(_iota((HEAD_DIM, t), 0) < 3, 1.0, 0.0).astype(BF16)
    q = q_ref[...]
    k = k_ref[...]
    v = v_ref[...]
    srow = _iota((ROW_TILE, LANES), 0)
    slane = _iota((ROW_TILE, LANES), 1)
    stat = jnp.where(srow == 2, c2[0:1, :], jnp.where(srow == 3, c2[t - 1:t, :], 0.0))
    for h in range(N_HEADS):
        sl = slice(h * HEAD_DIM, (h + 1) * HEAD_DIM)
        nhi, nmid, nlo = (piece.astype(F32) for piece in _split3(-c2[:, h:h + 1]))
        k_aug = jnp.where(lane == 0, nhi, jnp.where(lane == 1, nmid, jnp.where(lane == 2, nlo, 0.0)))
        qa_ref[h] = jnp.concatenate([_transpose_bf16(q[:, sl]), q_aug], axis=0)
        ka_ref[h] = jnp.concatenate([k[:, sl], k_aug.astype(BF16)], axis=1)
        va_ref[h] = _values_t(v[:, sl])
        for r, x in ((0, q), (1, k)):
            xf = x[:, sl].astype(F32)
            norm = jnp.sqrt(jnp.max(jnp.sum(xf * xf, axis=1, keepdims=True), axis=0, keepdims=True))
            stat = jnp.where(jnp.logical_and(srow == r, slane == h), norm, stat)
    st_ref[0] = stat


def _fox_prep(p, tail, fb_row, t=512):
    s = p.shape[0]
    cb = P_FOX // BRANCH_WIDTH
    aug, aug_spec, aug_t, aug_t_spec, aug_v, aug_v_spec = _aug_specs(s, t)
    return pl.pallas_call(
        _fox_prep_kernel,
        grid=(s // t,),
        in_specs=[pl.BlockSpec((t, BRANCH_WIDTH), lambda i: (i, cb)),
                  pl.BlockSpec((t, BRANCH_WIDTH), lambda i: (i, cb + 1)),
                  pl.BlockSpec((t, BRANCH_WIDTH), lambda i: (i, cb + 2)),
                  pl.BlockSpec((t, LANES), lambda i: (i, 0)),
                  pl.BlockSpec((1, LANES), lambda i: (0, 0))],
        out_specs=[aug_t_spec, aug_spec, aug_v_spec,
                   pl.BlockSpec((1, ROW_TILE, LANES), lambda i: (i, 0, 0))],
        out_shape=[aug_t, aug, aug_v, jax.ShapeDtypeStruct((s // t, ROW_TILE, LANES), F32)],
        scratch_shapes=[pltpu.VMEM((1, LANES), F32)],
        compiler_params=_params(("arbitrary",)),
    )(p, p, p, tail, fb_row)


SKIP_LOG2 = 48.0


def _fox_first_tile(stats, s, tq, tk):
    tp = s // stats.shape[0]
    qn, kn, c_first, c_last = (stats[:, r, :N_HEADS] for r in range(4))
    nq, nk = s // tq, s // tk
    qn = jnp.max(qn.reshape(nq, tq // tp, N_HEADS), axis=1)
    c_q = c_first.reshape(nq, tq // tp, N_HEADS)[:, 0, :]
    c_k = c_last.reshape(nk, tk // tp, N_HEADS)[:, -1, :]
    bound = (2.0 * qn * jnp.max(kn, axis=0))[:, None, :] + c_q[:, None, :] - c_k[None, :, :]
    j = jnp.arange(nk, dtype=jnp.int32)[None, :, None]
    first = jnp.min(jnp.where(bound >= -SKIP_LOG2, j, nk), axis=1)
    return first.T.reshape(-1).astype(jnp.int32)


def _moba_prep_kernel(q_ref, k_ref, v_ref, qa_ref, ka_ref, va_ref, kmean_ref):
    i = pl.program_id(0)
    t = q_ref.shape[0]
    nbl = kmean_ref.shape[0]

    @pl.when(i == 0)
    def _():
        kmean_ref[...] = jnp.zeros_like(kmean_ref)

    q = q_ref[...]
    k = k_ref[...]
    v = v_ref[...]
    km = kmean_ref[...]
    k_aug = jnp.where(_iota((t, HEAD_DIM), 1) == i, 1.0, 0.0).astype(BF16)
    blk = _iota((HEAD_DIM, t), 0)
    for h in range(N_HEADS):
        sl = slice(h * HEAD_DIM, (h + 1) * HEAD_DIM)
        qt = _transpose_bf16(q[:, sl])
        kh_hi, kh_lo = _split2(km[:HEAD_DIM, sl])
        gate = _dot(kh_hi, qt) + _dot(kh_lo, qt)
        g = jnp.where(blk < i, gate, NEG)
        bias = jnp.where(blk == i, 0.0, UNSELECTED)
        for _ in range(MOBA_TOPK):
            mx = jnp.max(g, axis=0, keepdims=True)
            idx = jnp.min(jnp.where(g == mx, blk, HEAD_DIM), axis=0, keepdims=True)
            pick = jnp.logical_and(blk == idx, mx > 0.5 * NEG)
            bias = jnp.where(pick, 0.0, bias)
            g = jnp.where(blk == idx, NEG, g)
        qa_ref[h] = jnp.concatenate([qt, bias.astype(BF16)], axis=0)
        ka_ref[h] = jnp.concatenate([k[:, sl], k_aug], axis=1)
        va_ref[h] = _values_t(v[:, sl])
    kmean = jnp.mean(k.astype(F32), axis=0, keepdims=True)
    kmean_ref[...] = jnp.where(_iota((nbl, BRANCH_WIDTH), 0) == i, kmean, km)


def _moba_prep(p):
    s = p.shape[0]
    t = MOBA_BLOCK
    assert s % t == 0 and s // t <= HEAD_DIM, "block one-hot must fit the 64 augmentation lanes"
    cb = P_MOBA // BRANCH_WIDTH
    aug, aug_spec, aug_t, aug_t_spec, aug_v, aug_v_spec = _aug_specs(s, t)
    return pl.pallas_call(
        _moba_prep_kernel,
        grid=(s // t,),
        in_specs=[pl.BlockSpec((t, BRANCH_WIDTH), lambda i: (i, cb)),
                  pl.BlockSpec((t, BRANCH_WIDTH), lambda i: (i, cb + 1)),
                  pl.BlockSpec((t, BRANCH_WIDTH), lambda i: (i, cb + 2))],
        out_specs=[aug_t_spec, aug_spec, aug_v_spec],
        out_shape=[aug_t, aug, aug_v],
        scratch_shapes=[pltpu.VMEM((HEAD_DIM, BRANCH_WIDTH), F32)],
        compiler_params=_params(("arbitrary",)),
    )(p, p, p)


def _flash_kernel(first_ref, qt_ref, k_ref, vt_ref, o_ref, s0_ref, s1_ref, s2_ref, *, tq, tk):
    h = pl.program_id(0)
    i = pl.program_id(1)
    qt = qt_ref[0]
    n_diag = tq // tk
    n_full = i * n_diag

    def scores(j, first_query=0):
        return _dot(k_ref[0, pl.ds(pl.multiple_of(j * tk, tk), tk), :], qt[:, first_query:])

    def absorb(j, s, m, acc):
        m_new = jnp.maximum(m, jnp.max(s, axis=0, keepdims=True))
        p = jnp.exp2(s - m_new).astype(BF16)
        vt = vt_ref[0, :, pl.ds(pl.multiple_of(j * tk, tk), tk)]
        return m_new, jnp.exp2(m - m_new) * acc + _dot(vt, p)

    j0 = jnp.minimum(first_ref[h * pl.num_programs(1) + i], n_full)
    m = jnp.full((1, tq), NEG, F32)
    acc = jnp.zeros((V_ROWS, tq), F32)

    extra = (n_full - j0) % 3

    def plain(j, carry):
        return absorb(j, scores(j), *carry)

    m, acc = lax.fori_loop(j0, j0 + extra, plain, (m, acc))
    j0 = j0 + extra

    s0_ref[...] = scores(j0)

    def triple(g, carry):
        j = j0 + 3 * g
        s1_ref[...] = scores(j + 1)
        carry = absorb(j, s0_ref[...], *carry)
        s2_ref[...] = scores(j + 2)
        carry = absorb(j + 1, s1_ref[...], *carry)
        s0_ref[...] = scores(j + 3)
        return absorb(j + 2, s2_ref[...], *carry)

    m, acc = lax.fori_loop(0, (n_full - j0) // 3, triple, (m, acc))

    causal = _iota((tk, tk), 0) <= _iota((tk, tk), 1)
    s = s0_ref[...]
    for d in range(n_diag):
        lo = d * tk
        s_next = scores(n_full + d + 1, lo + tk) if d + 1 < n_diag else None
        tri = jnp.where(causal, s[:, :tk], NEG)
        s = tri if lo + tk == tq else jnp.concatenate([tri, s[:, tk:]], axis=1)
        m_d, acc_d = absorb(n_full + d, s, m[:, lo:], acc[:, lo:])
        m = m_d if lo == 0 else jnp.concatenate([m[:, :lo], m_d], axis=1)
        acc = acc_d if lo == 0 else jnp.concatenate([acc[:, :lo], acc_d], axis=1)
        s = s_next
    o_ref[0] = (acc[:HEAD_DIM, :] / acc[HEAD_DIM:HEAD_DIM + 1, :]).T.astype(o_ref.dtype)


def _flash(first_tile, qt, ka, vt, tq, tk):
    nh, s, _ = ka.shape
    return pl.pallas_call(
        functools.partial(_flash_kernel, tq=tq, tk=tk),
        grid_spec=pltpu.PrefetchScalarGridSpec(
            num_scalar_prefetch=1,
            grid=(nh, s // tq),
            in_specs=[pl.BlockSpec((1, LANES, tq), lambda h, i, f: (h, 0, i)),
                      pl.BlockSpec((1, s, LANES), lambda h, i, f: (h, 0, 0)),
                      pl.BlockSpec((1, V_ROWS, s), lambda h, i, f: (h, 0, 0))],
            out_specs=pl.BlockSpec((1, tq, HEAD_DIM), lambda h, i, f: (h, i, 0)),
            scratch_shapes=[pltpu.VMEM((tk, tq), F32)] * 3),
        out_shape=jax.ShapeDtypeStruct((nh, s, HEAD_DIM), BF16),
        compiler_params=_params(("parallel", "parallel")),
    )(first_tile, qt, ka, vt)


def _flash_tiles(s):
    tq = min(1024, s)
    return tq, min(512, tq)


GLA_SUB = 16
GLA_UNROLL = 4


def _gla_kernel(q_ref, k_ref, v_ref, r_ref, t_ref, aup_ref, ab_ref, g_ref, y_ref,
                st_ref, b_ref, o_ref):
    t = q_ref.shape[0]
    c = GLA_SUB

    @pl.when(pl.program_id(0) == 0)
    def _():
        st_ref[...] = jnp.zeros_like(st_ref)

    t_hi, t_lo = _split2(t_ref[...])
    a_hi, a_lo = _split2(aup_ref[...])
    z = _dot(t_hi, a_hi) + _dot(t_lo, a_hi) + _dot(t_hi, a_lo) + ab_ref[...]
    log_a = _log_sigmoid(z) * (1.0 / GLA_TAU)
    row = _iota((t, t), 0)
    col = _iota((t, t), 1)
    tri = jnp.logical_and(row // c == col // c, row >= col).astype(BF16)
    hi, mid, lo = _split3(log_a)
    b_ref[...] = _dot(tri, hi) + _dot(tri, mid) + _dot(tri, lo)

    expand = (_iota((GLA_KW, BRANCH_WIDTH), 0) // GLA_DK
              == _iota((GLA_KW, BRANCH_WIDTH), 1) // HEAD_DIM).astype(BF16)
    st_mask = (_iota((BRANCH_WIDTH, GLA_KW), 0) // HEAD_DIM
               == _iota((BRANCH_WIDTH, GLA_KW), 1) // GLA_DK).astype(F32)
    srow = _iota((c, GLA_KW), 0)
    scale = GLA_DK ** -0.5

    def step(r0, st):
        qs = q_ref[pl.ds(r0, c), :].astype(F32) * scale
        ks = k_ref[pl.ds(r0, c), :].astype(F32)
        vb = v_ref[pl.ds(r0, c), :]
        vf = vb.astype(F32)
        bs = b_ref[pl.ds(r0, c), :]
        b_last = bs[c - 1:c, :]

        o_inter = _dot_nt((qs * jnp.exp(bs)).astype(BF16), st.astype(BF16))

        pieces = []
        for ss in range(c):
            e = jnp.exp(jnp.where(srow >= ss, bs - bs[ss:ss + 1, :], NEG))
            pieces.append(e * qs * ks[ss:ss + 1, :])
        pm = jnp.concatenate(pieces, axis=0).astype(BF16)
        a = _dot(pm, expand)
        o_intra = a[:c, :] * vf[0:1, :]
        for ss in range(1, c):
            o_intra += a[ss * c:(ss + 1) * c, :] * vf[ss:ss + 1, :]
        o_ref[pl.ds(r0, c), :] = o_inter + o_intra

        ke = (ks * jnp.exp(b_last - bs)).astype(BF16)
        return st * jnp.exp(b_last) + _dot_tn(vb, ke) * st_mask

    def steps(g, _):
        st = st_ref[...]
        for u in range(GLA_UNROLL):
            st = step(pl.multiple_of((g * GLA_UNROLL + u) * c, c), st)
        st_ref[...] = st
        return 0

    lax.fori_loop(0, t // (c * GLA_UNROLL), steps, 0)

    o = o_ref[...]
    ones_bd = (_iota((BRANCH_WIDTH, BRANCH_WIDTH), 0) // HEAD_DIM
               == _iota((BRANCH_WIDTH, BRANCH_WIDTH), 1) // HEAD_DIM).astype(BF16)
    sq_hi, sq_mid, sq_lo = _split3(o * o)
    ms = (_dot(sq_hi, ones_bd) + _dot(sq_mid, ones_bd) + _dot(sq_lo, ones_bd)) * (1.0 / HEAD_DIM)
    gr = r_ref[...].astype(F32)
    y = o * lax.rsqrt(ms + RMS_EPS) * g_ref[...] * (gr * _sigmoid(gr))
    y_ref[...] = y.astype(y_ref.dtype)


def _gla(p, tail, aup_pad, ab_row, g_row, t=512):
    s = p.shape[0]
    cq = P_GLA // GLA_KW
    cv = (P_GLA + 2 * GLA_KW) // BRANCH_WIDTH
    return pl.pallas_call(
        _gla_kernel,
        grid=(s // t,),
        in_specs=[pl.BlockSpec((t, GLA_KW), lambda i: (i, cq)),
                  pl.BlockSpec((t, GLA_KW), lambda i: (i, cq + 1)),
                  pl.BlockSpec((t, BRANCH_WIDTH), lambda i: (i, cv)),
                  pl.BlockSpec((t, BRANCH_WIDTH), lambda i: (i, cv + 1)),
                  pl.BlockSpec((t, LANES), lambda i: (i, 0)),
                  pl.BlockSpec((LANES, GLA_KW), lambda i: (0, 0)),
                  pl.BlockSpec((1, GLA_KW), lambda i: (0, 0)),
                  pl.BlockSpec((1, BRANCH_WIDTH), lambda i: (0, 0))],
        out_specs=pl.BlockSpec((t, BRANCH_WIDTH), lambda i: (i, 0)),
        out_shape=jax.ShapeDtypeStruct((s, BRANCH_WIDTH), BF16),
        scratch_shapes=[pltpu.VMEM((BRANCH_WIDTH, GLA_KW), F32),
                        pltpu.VMEM((t, GLA_KW), F32),
                        pltpu.VMEM((t, BRANCH_WIDTH), F32)],
        compiler_params=_params(("arbitrary",)),
    )(p, p, p, p, tail, aup_pad, ab_row, g_row)


def _layer_norm(z, g, b):
    mu = jnp.mean(z, axis=1, keepdims=True)
    zc = z - mu
    var = jnp.mean(zc * zc, axis=1, keepdims=True)
    return zc * lax.rsqrt(var + LN_EPS) * g + b


def _store_row_tiles(ref, val):
    n = val.shape[0]
    for a in range(ROW_TILE):
        ref[pl.ds(a, n, stride=ROW_TILE), :] = val[:, a * LANES:(a + 1) * LANES]


def _load_row_tiles(ref, n, base=0):
    return jnp.concatenate(
        [ref[pl.ds(base + a, n, stride=ROW_TILE), :] for a in range(ROW_TILE)], axis=1)


def _merge_kernel(fox_ref, gla_ref, moba_ref, g0_ref, g1_ref, g2_ref, x_ref, wb_ref, wo_ref,
                  lg_ref, lb_ref, wrh_ref, wrl_ref, br_ref,
                  x1_ref, sel_ref, idx_ref, gw_ref):
    tm = x_ref.shape[0]

    def heads(ref):
        return jnp.concatenate([ref[h] for h in range(N_HEADS)], axis=1)

    merged = _sigmoid(g0_ref[...]) * _dot(heads(fox_ref), wb_ref[0]).astype(BF16)
    merged += _sigmoid(g1_ref[...]) * _dot(gla_ref[...], wb_ref[1]).astype(BF16)
    merged += _sigmoid(g2_ref[...]) * _dot(heads(moba_ref), wb_ref[2]).astype(BF16)
    z = ALPHA * x_ref[...] + _dot(merged, wo_ref[...])
    x1 = _layer_norm(z, lg_ref[...], lb_ref[...])
    _store_row_tiles(x1_ref, x1)

    x_hi, x_lo = _split2(x1)
    both = _dot(x_hi, wrl_ref[...])
    logits = both[:, :LANES] + both[:, LANES:] + _dot(x_lo, wrh_ref[...]) + br_ref[...]
    lane = _iota((tm, LANES), 1)
    lg = jnp.where(lane < N_EXPERTS, logits, NEG)
    sel = jnp.zeros((tm, LANES), F32)
    idxf = jnp.zeros((tm, LANES), F32)
    ew = jnp.zeros((tm, LANES), F32)
    top = None
    for r in range(TOP_K):
        mx = jnp.max(lg, axis=1, keepdims=True)
        idx = jnp.min(jnp.where(lg == mx, lane, LANES), axis=1, keepdims=True)
        hit = lane == idx
        top = mx if top is None else top
        sel = jnp.where(hit, 1.0, sel)
        idxf = jnp.where(lane == r, idx.astype(F32), idxf)
        ew = jnp.where(lane == r, jnp.exp(mx - top), ew)
        lg = jnp.where(hit, NEG, lg)
    sel_ref[...] = sel
    idx_ref[...] = idxf
    gw_ref[...] = ew / jnp.sum(ew, axis=1, keepdims=True)


def _merge(fox_o, y_gla, moba_o, p, x, wb, wo, lg, lb, wrh, wrl, br, tm=256):
    s = x.shape[0]
    head_spec = pl.BlockSpec((N_HEADS, tm, HEAD_DIM), lambda i: (0, i, 0))
    row = lambda n: pl.BlockSpec((1, n), lambda i: (0, 0))
    small = jax.ShapeDtypeStruct((s, LANES), F32)
    small_spec = pl.BlockSpec((tm, LANES), lambda i: (i, 0))
    return pl.pallas_call(
        _merge_kernel,
        grid=(s // tm,),
        in_specs=[head_spec,
                  pl.BlockSpec((tm, BRANCH_WIDTH), lambda i: (i, 0)),
                  head_spec,
                  pl.BlockSpec((tm, D_MODEL), lambda i: (i, 0)),
                  pl.BlockSpec((tm, D_MODEL), lambda i: (i, 1)),
                  pl.BlockSpec((tm, D_MODEL), lambda i: (i, 2)),
                  pl.BlockSpec((tm, D_MODEL), lambda i: (i, 0)),
                  pl.BlockSpec((3, BRANCH_WIDTH, D_MODEL), lambda i: (0, 0, 0)),
                  pl.BlockSpec((D_MODEL, D_MODEL), lambda i: (0, 0)),
                  row(D_MODEL), row(D_MODEL),
                  pl.BlockSpec((D_MODEL, LANES), lambda i: (0, 0)),
                  pl.BlockSpec((D_MODEL, 2 * LANES), lambda i: (0, 0)),
                  row(LANES)],
        out_specs=[pl.BlockSpec((tm * ROW_TILE, LANES), lambda i: (i, 0)),
                   small_spec, small_spec, small_spec],
        out_shape=[jax.ShapeDtypeStruct((s * ROW_TILE, LANES), F32), small, small, small],
        compiler_params=_params(("parallel",)),
    )(fox_o, y_gla, moba_o, p, p, p, x, wb, wo, lg, lb, wrh, wrl, br)


def _rank_kernel(sel_ref, idx_ref, dest_ref, cnt_ref, carry_ref, start_ref):
    phase = pl.program_id(0)
    i = pl.program_id(1)
    t = sel_ref.shape[0]
    sel = sel_ref[...]
    chosen = jnp.sum(sel, axis=0, keepdims=True)

    @pl.when(jnp.logical_and(phase == 0, i == 0))
    def _():
        carry_ref[...] = jnp.zeros_like(carry_ref)

    @pl.when(jnp.logical_and(phase == 1, i == 0))
    def _():
        counts = carry_ref[...]
        cnt_ref[...] = counts
        blocks = jnp.floor((counts + (EXPERT_BLOCK - 1)) * (1.0 / EXPERT_BLOCK))
        before = (_iota((LANES, LANES), 0) < _iota((LANES, LANES), 1)).astype(BF16)
        first = _dot(jnp.broadcast_to(blocks, (ROW_TILE, LANES)).astype(BF16), before)
        start_ref[...] = first[0:1, :] * float(EXPERT_BLOCK)
        carry_ref[...] = jnp.zeros_like(carry_ref)

    @pl.when(phase == 1)
    def _():
        stri = (_iota((t, t), 0) > _iota((t, t), 1)).astype(BF16)
        pos = _dot(stri, sel.astype(BF16)) + carry_ref[...] + start_ref[...]
        lane = _iota((t, LANES), 1).astype(F32)
        idxf = idx_ref[...]
        dest = jnp.zeros((t, LANES), F32)
        for r in range(TOP_K):
            mine = jnp.sum(jnp.where(lane == idxf[:, r:r + 1], pos, 0.0), axis=1, keepdims=True)
            dest = jnp.where(lane == float(r), mine, dest)
        dest_ref[...] = dest.T[:ROW_TILE, :].astype(jnp.int32)

    carry_ref[...] += chosen


def _rank(sel, idxf, t=512):
    s = sel.shape[0]
    spec = pl.BlockSpec((t, LANES), lambda ph, i: (i, 0))
    return pl.pallas_call(
        _rank_kernel,
        grid=(2, s // t),
        in_specs=[spec, spec],
        out_specs=[pl.BlockSpec((ROW_TILE, t), lambda ph, i: (0, i * ph)),
                   pl.BlockSpec((1, LANES), lambda ph, i: (0, 0))],
        out_shape=[jax.ShapeDtypeStruct((ROW_TILE, s), jnp.int32),
                   jax.ShapeDtypeStruct((1, LANES), F32)],
        scratch_shapes=[pltpu.VMEM((1, LANES), F32), pltpu.VMEM((1, LANES), F32)],
        compiler_params=_params(("arbitrary", "arbitrary")),
    )(sel, idxf)


DMA_UNROLL = 4


def _token_rows(ref, token):
    return ref.at[pl.ds(pl.multiple_of(token * ROW_TILE, ROW_TILE), ROW_TILE), :]


def _dispatch_kernel(dest_ref, x_ref, xin_ref, sem):
    n = dest_ref.shape[1]

    def issue(g, _):
        for u in range(DMA_UNROLL):
            t = g * DMA_UNROLL + u
            for k in range(TOP_K):
                pltpu.make_async_copy(_token_rows(x_ref, t), _token_rows(xin_ref, dest_ref[k, t]),
                                      sem).start(priority=k % 2)
        return 0

    lax.fori_loop(0, n // DMA_UNROLL, issue, 0)
    for k in range(TOP_K):
        pltpu.make_async_copy(x_ref, xin_ref.at[pl.ds(0, n * ROW_TILE), :], sem).wait()


def _dispatch(dest, x1_2d, n_rows, td=512):
    s = x1_2d.shape[0] // ROW_TILE
    return pl.pallas_call(
        _dispatch_kernel,
        grid=(s // td,),
        in_specs=[pl.BlockSpec((ROW_TILE, td), lambda i: (0, i), memory_space=pltpu.SMEM),
                  pl.BlockSpec((td * ROW_TILE, LANES), lambda i: (i, 0))],
        out_specs=pl.BlockSpec(memory_space=pl.ANY),
        out_shape=jax.ShapeDtypeStruct((n_rows * ROW_TILE, LANES), F32),
        scratch_shapes=[pltpu.SemaphoreType.DMA(())],
        compiler_params=pltpu.CompilerParams(dimension_semantics=("arbitrary",),
                                             has_side_effects=True),
    )(dest, x1_2d)


HALF = LANES // 2
W_CAST_ROWS = 64


def _expert_kernel(be_ref, nv_ref, xin_ref, wgu_ref, bgu_ref, wdn_ref, bdn_ref, y_ref,
                   wgu_s, wdn_s, perm_s):
    b = pl.program_id(0)
    blk = y_ref.shape[0] // ROW_TILE
    nvalid = nv_ref[b]
    new_expert = jnp.logical_or(b == 0, be_ref[b] != be_ref[jnp.maximum(b - 1, 0)])

    @pl.when(jnp.logical_and(new_expert, nvalid > 0))
    def _():
        def cast(r, _):
            r0 = pl.multiple_of(r * W_CAST_ROWS, W_CAST_ROWS)
            wgu_s[pl.ds(r0, W_CAST_ROWS), :] = wgu_ref[0, pl.ds(r0, W_CAST_ROWS), :].astype(BF16)
            return 0

        lax.fori_loop(0, D_MODEL // W_CAST_ROWS, cast, 0)
        for c in range(D_MODEL // LANES):
            cols = slice(c * LANES, (c + 1) * LANES)
            for g in range(D_MODEL // LANES):
                lo = g * LANES
                perm_s[c, pl.ds(lo, HALF, stride=2), :] = wdn_ref[0, lo:lo + HALF, cols]
                perm_s[c, pl.ds(lo + 1, HALF, stride=2), :] = wdn_ref[0, lo + HALF:lo + LANES, cols]
            wdn_s[:, cols] = perm_s[c].astype(BF16)

    @pl.when(nvalid > 0)
    def _():
        x = _load_row_tiles(xin_ref, blk)
        x = jnp.where(_iota((blk, D_MODEL), 0) < nvalid, x, 0.0).astype(BF16)
        even = _iota((blk, LANES), 1) % 2 == 0
        acts = []
        for g in range(D_MODEL // LANES):
            lo = g * 2 * LANES
            h = _dot(x, wgu_s[:, lo:lo + 2 * LANES]) + bgu_ref[0, :, lo:lo + 2 * LANES]
            h_a = h[:, :LANES]
            h_b = h[:, LANES:]
            gate = jnp.where(even, h_a, pltpu.roll(h_b, 1, 1))
            up = jnp.where(even, pltpu.roll(h_a, LANES - 1, 1), h_b)
            gate = jnp.minimum(gate, SWIGLU_LIMIT)
            up = jnp.clip(up, -SWIGLU_LIMIT, SWIGLU_LIMIT)
            acts.append(((up + 1.0) * (gate * _sigmoid(SWIGLU_ALPHA * gate))).astype(BF16))
        act = jnp.concatenate(acts, axis=1)
        _store_row_tiles(y_ref, _dot(act, wdn_s[...]) + bdn_ref[0])

    @pl.when(nvalid <= 0)
    def _():
        y_ref[...] = jnp.zeros_like(y_ref)


def _experts(block_expert, block_valid, xin2d, w_gu, b_gu, w_dn, b_dn, layer):
    blk = EXPERT_BLOCK
    nb = block_expert.shape[0]
    x_spec = pl.BlockSpec((blk * ROW_TILE, LANES), lambda b, be, nv: (b, 0))
    which = lambda b, be, nv: (layer * N_EXPERTS + be[b], 0, 0)
    return pl.pallas_call(
        _expert_kernel,
        grid_spec=pltpu.PrefetchScalarGridSpec(
            num_scalar_prefetch=2,
            grid=(nb,),
            in_specs=[x_spec,
                      pl.BlockSpec((1, D_MODEL, 2 * D_MODEL), which),
                      pl.BlockSpec((1, 1, 2 * D_MODEL), which),
                      pl.BlockSpec((1, D_MODEL, D_MODEL), which),
                      pl.BlockSpec((1, 1, D_MODEL), which)],
            out_specs=x_spec,
            scratch_shapes=[pltpu.VMEM((D_MODEL, 2 * D_MODEL), BF16),
                            pltpu.VMEM((D_MODEL, D_MODEL), BF16),
                            pltpu.VMEM((D_MODEL // LANES, D_MODEL, LANES), F32)]),
        out_shape=jax.ShapeDtypeStruct((nb * blk * ROW_TILE, LANES), F32),
        compiler_params=pltpu.CompilerParams(dimension_semantics=("arbitrary",),
                                             vmem_limit_bytes=56 * 1024 * 1024),
    )(block_expert, block_valid, xin2d, w_gu, b_gu, w_dn, b_dn)


def _combine_kernel(dest_ref, next_ref, yb_ref, gw_ref, x1_ref, lg_ref, lb_ref, out_ref,
                    buf_ref, sem):
    i = pl.program_id(0)
    tc = out_ref.shape[0]
    slot = i % 2

    def gather(rows_ref, into):
        def body(g, _):
            for u in range(DMA_UNROLL):
                t = g * DMA_UNROLL + u
                for k in range(TOP_K):
                    pltpu.make_async_copy(_token_rows(yb_ref, rows_ref[k, t]),
                                          _token_rows(buf_ref.at[into, k], t),
                                          sem.at[into]).start(priority=k % 2)
            return 0

        lax.fori_loop(0, tc // DMA_UNROLL, body, 0)

    @pl.when(i == 0)
    def _():
        gather(dest_ref, 0)

    @pl.when(i + 1 < pl.num_programs(0))
    def _():
        gather(next_ref, 1 - slot)

    for k in range(TOP_K):
        pltpu.make_async_copy(yb_ref.at[pl.ds(0, tc * ROW_TILE), :], buf_ref.at[slot, k],
                              sem.at[slot]).wait()

    gw = gw_ref[...]
    f = jnp.zeros((tc, D_MODEL), F32)
    for k in range(TOP_K):
        f += gw[:, k:k + 1] * _load_row_tiles(buf_ref.at[slot, k], tc)
    z = ALPHA * _load_row_tiles(x1_ref, tc) + f
    out_ref[...] = _layer_norm(z, lg_ref[...], lb_ref[...])


def _combine(dest, yb_rows, gw, x1_2d, lg, lb, tc=256):
    s = gw.shape[0]
    n = s // tc
    row = pl.BlockSpec((1, D_MODEL), lambda i: (0, 0))
    return pl.pallas_call(
        _combine_kernel,
        grid=(n,),
        in_specs=[pl.BlockSpec((ROW_TILE, tc), lambda i: (0, i), memory_space=pltpu.SMEM),
                  pl.BlockSpec((ROW_TILE, tc), lambda i: (0, jnp.minimum(i + 1, n - 1)),
                               memory_space=pltpu.SMEM),
                  pl.BlockSpec(memory_space=pl.ANY),
                  pl.BlockSpec((tc, LANES), lambda i: (i, 0)),
                  pl.BlockSpec((tc * ROW_TILE, LANES), lambda i: (i, 0)),
                  row, row],
        out_specs=pl.BlockSpec((tc, D_MODEL), lambda i: (i, 0)),
        out_shape=jax.ShapeDtypeStruct((s, D_MODEL), F32),
        scratch_shapes=[pltpu.VMEM((2, TOP_K, tc * ROW_TILE, LANES), F32),
                        pltpu.SemaphoreType.DMA((2,))],
        compiler_params=_params(("arbitrary",)),
    )(dest, dest, yb_rows, gw, x1_2d, lg, lb)


W_IN_SIZES = (256, 256, 256, 4, 128, 128, 256, 16, 256, 256, 256, 256, 3 * D_MODEL)
W_IN_WIDTH = sum(W_IN_SIZES)


def _w_in_kernel(w_ref, main_ref, tail_ref):
    w = w_ref[0]
    offs = [0]
    for n in W_IN_SIZES:
        offs.append(offs[-1] + n)
    fq, fk, fv, ff, gq, gk, gv, ga, gr, mq, mk, mv, gates = (
        w[:, offs[i]:offs[i + 1]] for i in range(len(W_IN_SIZES)))
    main_ref[...] = jnp.concatenate(
        [gates, fq, fk, fv, mq, mk, mv, gq, gk, gv, gr], axis=1).astype(BF16)
    pad = jnp.zeros((w.shape[0], LANES - 4 - GLA_RANK), F32)
    tail_ref[...] = jnp.concatenate([ff, ga, pad], axis=1).astype(BF16)


def _permute_w_in(w_in, layer, tr=128):
    return pl.pallas_call(
        _w_in_kernel,
        grid=(D_MODEL // tr,),
        in_specs=[pl.BlockSpec((1, tr, W_IN_WIDTH), lambda i: (layer, i, 0))],
        out_specs=[pl.BlockSpec((tr, P_WIDTH), lambda i: (i, 0)),
                   pl.BlockSpec((tr, LANES), lambda i: (i, 0))],
        out_shape=[jax.ShapeDtypeStruct((D_MODEL, P_WIDTH), BF16),
                   jax.ShapeDtypeStruct((D_MODEL, LANES), BF16)],
        compiler_params=_params(("parallel",)),
    )(w_in)


def _pad_row(v, n=LANES, fill=0.0):
    return jnp.pad(v.astype(F32), (0, n - v.shape[0]), constant_values=fill)[None, :]


def _mixer_layer(x, w_in, layer, fox_fb, gla_a_up, gla_a_b, gla_norm_g, w_branch, w_out, ln_g, ln_b,
                 w_router, b_router):
    w_main, w_tail = _permute_w_in(w_in, layer)
    p, tail = _inproj(x, w_main, w_tail)

    s = x.shape[0]
    tq, tk = _flash_tiles(s)
    fox_qt, fox_k, fox_vt, fox_stats = _fox_prep(p, tail, _pad_row(fox_fb))
    fox_o = _flash(_fox_first_tile(fox_stats, s, tq, tk), fox_qt, fox_k, fox_vt, tq, tk)
    moba_o = _flash(jnp.zeros((N_HEADS * (s // tq),), jnp.int32), *_moba_prep(p), tq, tk)
    aup_pad = jnp.zeros((LANES, GLA_KW), F32).at[TAIL_GA:TAIL_GA + GLA_RANK].set(gla_a_up)
    y_gla = _gla(p, tail, aup_pad, gla_a_b[None, :], gla_norm_g[None, :])

    wr = jnp.pad(w_router, ((0, 0), (0, LANES - N_EXPERTS)))
    wr_hi = wr.astype(BF16)
    wr_lo = jnp.concatenate([wr_hi, (wr - wr_hi.astype(F32)).astype(BF16)], axis=1)
    return _merge(fox_o, y_gla, moba_o, p, x, w_branch.astype(BF16), w_out.astype(BF16),
                  ln_g[None, :], ln_b[None, :], wr_hi, wr_lo, _pad_row(b_router))


def _moe_layer(x1_2d, sel, idxf, gw, w_gu, b_gu, w_dn, b_dn, ln_g, ln_b, layer):
    s = sel.shape[0]
    blk = EXPERT_BLOCK
    dest, counts = _rank(sel, idxf)

    counts = counts[0, :N_EXPERTS].astype(jnp.int32)
    padded = (counts + blk - 1) // blk * blk
    pad_end = jnp.cumsum(padded)
    pad_start = pad_end - padded
    n_blocks = (s * TOP_K + N_EXPERTS * (blk - 1)) // blk + 1
    block_start = jnp.arange(n_blocks, dtype=jnp.int32) * blk
    active = block_start < pad_end[-1]
    block_expert = jnp.sum((block_start[:, None] >= pad_end[None, :]).astype(jnp.int32), axis=1)
    last_expert = jnp.max(jnp.where(counts > 0, jnp.arange(N_EXPERTS, dtype=jnp.int32), 0))
    block_expert = jnp.where(active, block_expert, last_expert)
    block_valid = jnp.clip(pad_start[block_expert] + counts[block_expert] - block_start, 0, blk)
    block_valid = jnp.where(active, block_valid, 0).astype(jnp.int32)

    xin = _dispatch(dest, x1_2d, n_blocks * blk)
    yb = _experts(block_expert, block_valid, xin, w_gu, b_gu, w_dn, b_dn, layer)
    return _combine(dest, yb, gw, x1_2d, ln_g[None, :], ln_b[None, :])


def kernel(x, w_in, fox_fb, gla_a_up, gla_a_b, gla_norm_g, w_branch, w_out, ln1_g, ln1_b,
           w_router, b_router, w_gu, b_gu, w_dn, b_dn, ln2_g, ln2_b):
    b, s, d = x.shape
    assert b == 1 and d == D_MODEL
    h = x.reshape(s, d)
    n_le = DEPTH * N_EXPERTS
    w_gu = w_gu.reshape(n_le, D_MODEL, 2 * D_MODEL)
    b_gu = b_gu.reshape(n_le, 1, 2 * D_MODEL)
    w_dn = w_dn.reshape(n_le, D_MODEL, D_MODEL)
    b_dn = b_dn.reshape(n_le, 1, D_MODEL)
    for l in range(DEPTH):
        x1_2d, sel, idxf, gw = _mixer_layer(
            h, w_in, l, fox_fb[l], gla_a_up[l], gla_a_b[l], gla_norm_g[l], w_branch[l], w_out[l],
            ln1_g[l], ln1_b[l], w_router[l], b_router[l])
        h = _moe_layer(x1_2d, sel, idxf, gw, w_gu, b_gu, w_dn, b_dn, ln2_g[l], ln2_b[l], l)
    return h.reshape(b, s, d)
```

```python
import functools

import jax
import jax.numpy as jnp
from jax import lax
from jax.experimental import pallas as pl
from jax.experimental.pallas import tpu as pltpu

F32 = jnp.float32
BF16 = jnp.bfloat16

D_MODEL = 1024
DEPTH = 4
N_HEADS = 4
HEAD_DIM = 64
GLA_DK = 32
GLA_KW = N_HEADS * GLA_DK
GLA_RANK = 16
GLA_TAU = 16.0
MOBA_BLOCK = 256
MOBA_TOPK = 3
BRANCH_WIDTH = 256
N_EXPERTS = 32
TOP_K = 4
SWIGLU_LIMIT = 7.0
SWIGLU_ALPHA = 1.702
ALPHA = (2 * DEPTH) ** 0.25
LN_EPS = 1e-5
RMS_EPS = 1e-6
LOG2E = 1.4426950408889634

LANES = 128
ROW_TILE = 8
NEG = -1e30
UNSELECTED = -32768.0
VMEM_LIMIT = 48 * 1024 * 1024

P_GATES = 0
P_FOX = 3 * D_MODEL
P_MOBA = P_FOX + 3 * BRANCH_WIDTH
P_GLA = P_MOBA + 3 * BRANCH_WIDTH
P_WIDTH = P_GLA + 2 * GLA_KW + 2 * BRANCH_WIDTH
TAIL_FF = 0
TAIL_GA = 4

EXPERT_BLOCK = 512


def _params(sem):
    return pltpu.CompilerParams(dimension_semantics=sem, vmem_limit_bytes=VMEM_LIMIT)


def _split3(x):
    hi = x.astype(BF16)
    r1 = x - hi.astype(F32)
    mid = r1.astype(BF16)
    lo = (r1 - mid.astype(F32)).astype(BF16)
    return hi, mid, lo


def _split2(x):
    hi = x.astype(BF16)
    lo = (x - hi.astype(F32)).astype(BF16)
    return hi, lo


def _dot(a, b):
    return jnp.dot(a, b, preferred_element_type=F32)


def _dot_nt(a, b):
    return lax.dot_general(a, b, (((1,), (1,)), ((), ())), preferred_element_type=F32)


def _dot_tn(a, b):
    return lax.dot_general(a, b, (((0,), (0,)), ((), ())), preferred_element_type=F32)


def _log_sigmoid(t):
    return jnp.minimum(t, 0.0) - jnp.log1p(jnp.exp(-jnp.abs(t)))


def _sigmoid(t):
    return 0.5 * jnp.tanh(0.5 * t) + 0.5


def _iota(shape, axis):
    return lax.broadcasted_iota(jnp.int32, shape, axis)


def _transpose_bf16(x):
    return x.astype(F32).T.astype(BF16)


BF16_ROWS = 16
V_ROWS = HEAD_DIM + BF16_ROWS


def _values_t(v):
    ones_row = jnp.where(_iota((BF16_ROWS, v.shape[0]), 0) == 0, 1.0, 0.0).astype(BF16)
    return jnp.concatenate([_transpose_bf16(v), ones_row], axis=0)


def _aug_specs(s, t):
    aug = jax.ShapeDtypeStruct((N_HEADS, s, LANES), BF16)
    aug_spec = pl.BlockSpec((N_HEADS, t, LANES), lambda i: (0, i, 0))
    aug_t = jax.ShapeDtypeStruct((N_HEADS, LANES, s), BF16)
    aug_t_spec = pl.BlockSpec((N_HEADS, LANES, t), lambda i: (0, 0, i))
    aug_v = jax.ShapeDtypeStruct((N_HEADS, V_ROWS, s), BF16)
    aug_v_spec = pl.BlockSpec((N_HEADS, V_ROWS, t), lambda i: (0, 0, i))
    return aug, aug_spec, aug_t, aug_t_spec, aug_v, aug_v_spec


def _inproj_kernel(x_ref, w_ref, wt_ref, sc_ref, p_ref, t_ref, xb_ref):
    @pl.when(pl.program_id(1) == 0)
    def _():
        xb_ref[...] = x_ref[...].astype(BF16)
        t_ref[...] = _dot(xb_ref[...], wt_ref[...])

    p_ref[...] = (_dot(xb_ref[...], w_ref[...]) * sc_ref[...]).astype(BF16)


def _inproj(x, w, wt, tm=1024, tn=1792):
    s = x.shape[0]
    col = jnp.arange(P_WIDTH)
    is_q = ((col >= P_FOX) & (col < P_FOX + BRANCH_WIDTH)) | ((col >= P_MOBA) & (col < P_MOBA + BRANCH_WIDTH))
    scale = jnp.where(is_q, HEAD_DIM ** -0.5 * LOG2E, 1.0).astype(F32)[None, :]
    return pl.pallas_call(
        _inproj_kernel,
        grid=(s // tm, P_WIDTH // tn),
        in_specs=[pl.BlockSpec((tm, D_MODEL), lambda i, j: (i, 0)),
                  pl.BlockSpec((D_MODEL, tn), lambda i, j: (0, j)),
                  pl.BlockSpec((D_MODEL, LANES), lambda i, j: (0, 0)),
                  pl.BlockSpec((1, tn), lambda i, j: (0, j))],
        out_specs=[pl.BlockSpec((tm, tn), lambda i, j: (i, j)),
                   pl.BlockSpec((tm, LANES), lambda i, j: (i, 0))],
        out_shape=[jax.ShapeDtypeStruct((s, P_WIDTH), BF16),
                   jax.ShapeDtypeStruct((s, LANES), F32)],
        scratch_shapes=[pltpu.VMEM((tm, D_MODEL), BF16)],
        compiler_params=_params(("parallel", "arbitrary")),
    )(x, w, wt, scale)


def _fox_prep_kernel(q_ref, k_ref, v_ref, t_ref, fb_ref, qa_ref, ka_ref, va_ref, st_ref, carry_ref):
    t = q_ref.shape[0]

    @pl.when(pl.program_id(0) == 0)
    def _():
        carry_ref[...] = jnp.zeros_like(carry_ref)

    ls = _log_sigmoid(t_ref[...] + fb_ref[...])
    tri = (_iota((t, t), 0) >= _iota((t, t), 1)).astype(BF16)
    hi, mid, lo = _split3(ls)
    c = _dot(tri, hi) + _dot(tri, mid) + _dot(tri, lo) + carry_ref[...]
    carry_ref[...] = c[t - 1:t, :]

    c2 = LOG2E * c

    lane = _iota((t, HEAD_DIM), 1)
    q_aug = jnp.where(_iota((HEAD_DIM, t), 0) < 3, 1.0, 0.0).astype(BF16)
    q = q_ref[...]
    k = k_ref[...]
    v = v_ref[...]
    head_of = (_iota((BRANCH_WIDTH, LANES), 0) // HEAD_DIM == _iota((BRANCH_WIDTH, LANES), 1))
    head_of = head_of.astype(BF16)

    def max_norm(x):
        xf = x.astype(F32)
        hi, lo = _split2(xf * xf)
        norm2 = jnp.max(_dot(hi, head_of) + _dot(lo, head_of), axis=0, keepdims=True)
        return jnp.sqrt(norm2 * (1.0 + 2.0 ** -8))

    srow = _iota((ROW_TILE, LANES), 0)
    stat = jnp.where(srow == 0, max_norm(q), jnp.where(srow == 1, max_norm(k), jnp.where(
        srow == 2, c2[0:1, :], jnp.where(srow == 3, c2[t - 1:t, :], 0.0))))
    for h in range(N_HEADS):
        sl = slice(h * HEAD_DIM, (h + 1) * HEAD_DIM)
        nhi, nmid, nlo = (piece.astype(F32) for piece in _split3(-c2[:, h:h + 1]))
        k_aug = jnp.where(lane == 0, nhi, jnp.where(lane == 1, nmid, jnp.where(lane == 2, nlo, 0.0)))
        qa_ref[h] = jnp.concatenate([_transpose_bf16(q[:, sl]), q_aug], axis=0)
        ka_ref[h] = jnp.concatenate([k[:, sl], k_aug.astype(BF16)], axis=1)
        va_ref[h] = _values_t(v[:, sl])
    st_ref[0] = stat


def _fox_prep(p, tail, fb_row, t=512):
    s = p.shape[0]
    cb = P_FOX // BRANCH_WIDTH
    aug, aug_spec, aug_t, aug_t_spec, aug_v, aug_v_spec = _aug_specs(s, t)
    return pl.pallas_call(
        _fox_prep_kernel,
        grid=(s // t,),
        in_specs=[pl.BlockSpec((t, BRANCH_WIDTH), lambda i: (i, cb)),
                  pl.BlockSpec((t, BRANCH_WIDTH), lambda i: (i, cb + 1)),
                  pl.BlockSpec((t, BRANCH_WIDTH), lambda i: (i, cb + 2)),
                  pl.BlockSpec((t, LANES), lambda i: (i, 0)),
                  pl.BlockSpec((1, LANES), lambda i: (0, 0))],
        out_specs=[aug_t_spec, aug_spec, aug_v_spec,
                   pl.BlockSpec((1, ROW_TILE, LANES), lambda i: (i, 0, 0))],
        out_shape=[aug_t, aug, aug_v, jax.ShapeDtypeStruct((s // t, ROW_TILE, LANES), F32)],
        scratch_shapes=[pltpu.VMEM((1, LANES), F32)],
        compiler_params=_params(("arbitrary",)),
    )(p, p, p, tail, fb_row)


SKIP_LOG2 = 48.0


def _fox_first_tile(stats, s, tq, tk):
    tp = s // stats.shape[0]
    qn, kn, c_first, c_last = (stats[:, r, :N_HEADS] for r in range(4))
    nq, nk = s // tq, s // tk
    qn = jnp.max(qn.reshape(nq, tq // tp, N_HEADS), axis=1)
    c_q = c_first.reshape(nq, tq // tp, N_HEADS)[:, 0, :]
    c_k = c_last.reshape(nk, tk // tp, N_HEADS)[:, -1, :]
    bound = (2.0 * qn * jnp.max(kn, axis=0))[:, None, :] + c_q[:, None, :] - c_k[None, :, :]
    j = jnp.arange(nk, dtype=jnp.int32)[None, :, None]
    first = jnp.min(jnp.where(bound >= -SKIP_LOG2, j, nk), axis=1)
    return first.T.reshape(-1).astype(jnp.int32)


def _moba_prep_kernel(q_ref, k_ref, v_ref, qa_ref, ka_ref, va_ref, kmean_ref):
    i = pl.program_id(0)
    t = q_ref.shape[0]
    nbl = kmean_ref.shape[0]

    @pl.when(i == 0)
    def _():
        kmean_ref[...] = jnp.zeros_like(kmean_ref)

    q = q_ref[...]
    k = k_ref[...]
    v = v_ref[...]
    km = kmean_ref[...]
    k_aug = jnp.where(_iota((t, HEAD_DIM), 1) == i, 1.0, 0.0).astype(BF16)
    blk = _iota((HEAD_DIM, t), 0)
    for h in range(N_HEADS):
        sl = slice(h * HEAD_DIM, (h + 1) * HEAD_DIM)
        qt = _transpose_bf16(q[:, sl])
        kh_hi, kh_lo = _split2(km[:HEAD_DIM, sl])
        gate = _dot(kh_hi, qt) + _dot(kh_lo, qt)
        g = jnp.where(blk < i, gate, NEG)
        bias = jnp.where(blk == i, 0.0, UNSELECTED)
        for _ in range(MOBA_TOPK):
            mx = jnp.max(g, axis=0, keepdims=True)
            idx = jnp.min(jnp.where(g == mx, blk, HEAD_DIM), axis=0, keepdims=True)
            pick = jnp.logical_and(blk == idx, mx > 0.5 * NEG)
            bias = jnp.where(pick, 0.0, bias)
            g = jnp.where(blk == idx, NEG, g)
        qa_ref[h] = jnp.concatenate([qt, bias.astype(BF16)], axis=0)
        ka_ref[h] = jnp.concatenate([k[:, sl], k_aug], axis=1)
        va_ref[h] = _values_t(v[:, sl])
    kmean = jnp.mean(k.astype(F32), axis=0, keepdims=True)
    kmean_ref[...] = jnp.where(_iota((nbl, BRANCH_WIDTH), 0) == i, kmean, km)


def _moba_prep(p):
    s = p.shape[0]
    t = MOBA_BLOCK
    assert s % t == 0 and s // t <= HEAD_DIM, "block one-hot must fit the 64 augmentation lanes"
    cb = P_MOBA // BRANCH_WIDTH
    aug, aug_spec, aug_t, aug_t_spec, aug_v, aug_v_spec = _aug_specs(s, t)
    return pl.pallas_call(
        _moba_prep_kernel,
        grid=(s // t,),
        in_specs=[pl.BlockSpec((t, BRANCH_WIDTH), lambda i: (i, cb)),
                  pl.BlockSpec((t, BRANCH_WIDTH), lambda i: (i, cb + 1)),
                  pl.BlockSpec((t, BRANCH_WIDTH), lambda i: (i, cb + 2))],
        out_specs=[aug_t_spec, aug_spec, aug_v_spec],
        out_shape=[aug_t, aug, aug_v],
        scratch_shapes=[pltpu.VMEM((HEAD_DIM, BRANCH_WIDTH), F32)],
        compiler_params=_params(("arbitrary",)),
    )(p, p, p)


def _flash_kernel(first_ref, qt_ref, k_ref, vt_ref, o_ref, s0_ref, s1_ref, s2_ref, *, tq, tk):
    h = pl.program_id(0)
    i = pl.program_id(1)
    qt = qt_ref[0]
    n_diag = tq // tk
    n_full = i * n_diag

    def scores(j, first_query=0):
        return _dot(k_ref[0, pl.ds(pl.multiple_of(j * tk, tk), tk), :], qt[:, first_query:])

    def absorb(j, s, m, acc):
        m_new = jnp.maximum(m, jnp.max(s, axis=0, keepdims=True))
        p = jnp.exp2(s - m_new).astype(BF16)
        vt = vt_ref[0, :, pl.ds(pl.multiple_of(j * tk, tk), tk)]
        return m_new, jnp.exp2(m - m_new) * acc + _dot(vt, p)

    j0 = jnp.minimum(first_ref[h * pl.num_programs(1) + i], n_full)
    m = jnp.full((1, tq), NEG, F32)
    acc = jnp.zeros((V_ROWS, tq), F32)

    extra = (n_full - j0) % 3

    def plain(j, carry):
        return absorb(j, scores(j), *carry)

    m, acc = lax.fori_loop(j0, j0 + extra, plain, (m, acc))
    j0 = j0 + extra

    s0_ref[...] = scores(j0)

    def triple(g, carry):
        j = j0 + 3 * g
        s1_ref[...] = scores(j + 1)
        carry = absorb(j, s0_ref[...], *carry)
        s2_ref[...] = scores(j + 2)
        carry = absorb(j + 1, s1_ref[...], *carry)
        s0_ref[...] = scores(j + 3)
        return absorb(j + 2, s2_ref[...], *carry)

    m, acc = lax.fori_loop(0, (n_full - j0) // 3, triple, (m, acc))

    causal = _iota((tk, tk), 0) <= _iota((tk, tk), 1)
    s = s0_ref[...]
    for d in range(n_diag):
        lo = d * tk
        s_next = scores(n_full + d + 1, lo + tk) if d + 1 < n_diag else None
        tri = jnp.where(causal, s[:, :tk], NEG)
        s = tri if lo + tk == tq else jnp.concatenate([tri, s[:, tk:]], axis=1)
        m_d, acc_d = absorb(n_full + d, s, m[:, lo:], acc[:, lo:])
        m = m_d if lo == 0 else jnp.concatenate([m[:, :lo], m_d], axis=1)
        acc = acc_d if lo == 0 else jnp.concatenate([acc[:, :lo], acc_d], axis=1)
        s = s_next
    o_ref[0] = (acc[:HEAD_DIM, :] / acc[HEAD_DIM:HEAD_DIM + 1, :]).T.astype(o_ref.dtype)


def _flash(first_tile, qt, ka, vt, tq, tk):
    nh, s, _ = ka.shape
    return pl.pallas_call(
        functools.partial(_flash_kernel, tq=tq, tk=tk),
        grid_spec=pltpu.PrefetchScalarGridSpec(
            num_scalar_prefetch=1,
            grid=(nh, s // tq),
            in_specs=[pl.BlockSpec((1, LANES, tq), lambda h, i, f: (h, 0, i)),
                      pl.BlockSpec((1, s, LANES), lambda h, i, f: (h, 0, 0)),
                      pl.BlockSpec((1, V_ROWS, s), lambda h, i, f: (h, 0, 0))],
            out_specs=pl.BlockSpec((1, tq, HEAD_DIM), lambda h, i, f: (h, i, 0)),
            scratch_shapes=[pltpu.VMEM((tk, tq), F32)] * 3),
        out_shape=jax.ShapeDtypeStruct((nh, s, HEAD_DIM), BF16),
        compiler_params=_params(("parallel", "parallel")),
    )(first_tile, qt, ka, vt)


def _flash_tiles(s):
    tq = min(1024, s)
    return tq, min(512, tq)


GLA_SUB = 16
GLA_UNROLL = 4


def _gla_kernel(q_ref, k_ref, v_ref, r_ref, t_ref, aup_ref, ab_ref, g_ref, y_ref,
                st_ref, b_ref, o_ref):
    t = q_ref.shape[0]
    c = GLA_SUB

    @pl.when(pl.program_id(0) == 0)
    def _():
        st_ref[...] = jnp.zeros_like(st_ref)

    t_hi, t_lo = _split2(t_ref[...])
    a_hi, a_lo = _split2(aup_ref[...])
    z = _dot(t_hi, a_hi) + _dot(t_lo, a_hi) + _dot(t_hi, a_lo) + ab_ref[...]
    log_a = _log_sigmoid(z) * (1.0 / GLA_TAU)
    row = _iota((t, t), 0)
    col = _iota((t, t), 1)
    tri = jnp.logical_and(row // c == col // c, row >= col).astype(BF16)
    hi, mid, lo = _split3(log_a)
    b_ref[...] = _dot(tri, hi) + _dot(tri, mid) + _dot(tri, lo)

    expand = (_iota((GLA_KW, BRANCH_WIDTH), 0) // GLA_DK
              == _iota((GLA_KW, BRANCH_WIDTH), 1) // HEAD_DIM).astype(BF16)
    st_mask = (_iota((BRANCH_WIDTH, GLA_KW), 0) // HEAD_DIM
               == _iota((BRANCH_WIDTH, GLA_KW), 1) // GLA_DK).astype(F32)
    srow = _iota((c, GLA_KW), 0)
    scale = GLA_DK ** -0.5

    def step(r0, st):
        qs = q_ref[pl.ds(r0, c), :].astype(F32) * scale
        ks = k_ref[pl.ds(r0, c), :].astype(F32)
        vb = v_ref[pl.ds(r0, c), :]
        vf = vb.astype(F32)
        bs = b_ref[pl.ds(r0, c), :]
        b_last = bs[c - 1:c, :]

        o_inter = _dot_nt((qs * jnp.exp(bs)).astype(BF16), st.astype(BF16))

        pieces = []
        for ss in range(c):
            e = jnp.exp(jnp.where(srow >= ss, bs - bs[ss:ss + 1, :], NEG))
            pieces.append(e * qs * ks[ss:ss + 1, :])
        pm = jnp.concatenate(pieces, axis=0).astype(BF16)
        a = _dot(pm, expand)
        o_intra = a[:c, :] * vf[0:1, :]
        for ss in range(1, c):
            o_intra += a[ss * c:(ss + 1) * c, :] * vf[ss:ss + 1, :]
        o_ref[pl.ds(r0, c), :] = o_inter + o_intra

        ke = (ks * jnp.exp(b_last - bs)).astype(BF16)
        return st * jnp.exp(b_last) + _dot_tn(vb, ke) * st_mask

    def steps(g, _):
        st = st_ref[...]
        for u in range(GLA_UNROLL):
            st = step(pl.multiple_of((g * GLA_UNROLL + u) * c, c), st)
        st_ref[...] = st
        return 0

    lax.fori_loop(0, t // (c * GLA_UNROLL), steps, 0)

    o = o_ref[...]
    ones_bd = (_iota((BRANCH_WIDTH, BRANCH_WIDTH), 0) // HEAD_DIM
               == _iota((BRANCH_WIDTH, BRANCH_WIDTH), 1) // HEAD_DIM).astype(BF16)
    sq_hi, sq_mid, sq_lo = _split3(o * o)
    ms = (_dot(sq_hi, ones_bd) + _dot(sq_mid, ones_bd) + _dot(sq_lo, ones_bd)) * (1.0 / HEAD_DIM)
    gr = r_ref[...].astype(F32)
    y = o * lax.rsqrt(ms + RMS_EPS) * g_ref[...] * (gr * _sigmoid(gr))
    y_ref[...] = y.astype(y_ref.dtype)


def _gla(p, tail, aup_pad, ab_row, g_row, t=512):
    s = p.shape[0]
    cq = P_GLA // GLA_KW
    cv = (P_GLA + 2 * GLA_KW) // BRANCH_WIDTH
    return pl.pallas_call(
        _gla_kernel,
        grid=(s // t,),
        in_specs=[pl.BlockSpec((t, GLA_KW), lambda i: (i, cq)),
                  pl.BlockSpec((t, GLA_KW), lambda i: (i, cq + 1)),
                  pl.BlockSpec((t, BRANCH_WIDTH), lambda i: (i, cv)),
                  pl.BlockSpec((t, BRANCH_WIDTH), lambda i: (i, cv + 1)),
                  pl.BlockSpec((t, LANES), lambda i: (i, 0)),
                  pl.BlockSpec((LANES, GLA_KW), lambda i: (0, 0)),
                  pl.BlockSpec((1, GLA_KW), lambda i: (0, 0)),
                  pl.BlockSpec((1, BRANCH_WIDTH), lambda i: (0, 0))],
        out_specs=pl.BlockSpec((t, BRANCH_WIDTH), lambda i: (i, 0)),
        out_shape=jax.ShapeDtypeStruct((s, BRANCH_WIDTH), BF16),
        scratch_shapes=[pltpu.VMEM((BRANCH_WIDTH, GLA_KW), F32),
                        pltpu.VMEM((t, GLA_KW), F32),
                        pltpu.VMEM((t, BRANCH_WIDTH), F32)],
        compiler_params=_params(("arbitrary",)),
    )(p, p, p, p, tail, aup_pad, ab_row, g_row)


def _layer_norm(z, g, b):
    mu = jnp.mean(z, axis=1, keepdims=True)
    zc = z - mu
    var = jnp.mean(zc * zc, axis=1, keepdims=True)
    return zc * lax.rsqrt(var + LN_EPS) * g + b


def _store_row_tiles(ref, val):
    n = val.shape[0]
    for a in range(ROW_TILE):
        ref[pl.ds(a, n, stride=ROW_TILE), :] = val[:, a * LANES:(a + 1) * LANES]


def _load_row_tiles(ref, n, base=0):
    return jnp.concatenate(
        [ref[pl.ds(base + a, n, stride=ROW_TILE), :] for a in range(ROW_TILE)], axis=1)


def _merge_kernel(fox_ref, gla_ref, moba_ref, g0_ref, g1_ref, g2_ref, x_ref, wb_ref, wo_ref,
                  lg_ref, lb_ref, wrh_ref, wrl_ref, br_ref,
                  x1_ref, sel_ref, idx_ref, gw_ref):
    tm = x_ref.shape[0]

    def heads(ref):
        return jnp.concatenate([ref[h] for h in range(N_HEADS)], axis=1)

    merged = _sigmoid(g0_ref[...]) * _dot(heads(fox_ref), wb_ref[0]).astype(BF16)
    merged += _sigmoid(g1_ref[...]) * _dot(gla_ref[...], wb_ref[1]).astype(BF16)
    merged += _sigmoid(g2_ref[...]) * _dot(heads(moba_ref), wb_ref[2]).astype(BF16)
    z = ALPHA * x_ref[...] + _dot(merged, wo_ref[...])
    x1 = _layer_norm(z, lg_ref[...], lb_ref[...])
    _store_row_tiles(x1_ref, x1)

    x_hi, x_lo = _split2(x1)
    both = _dot(x_hi, wrl_ref[...])
    logits = both[:, :LANES] + both[:, LANES:] + _dot(x_lo, wrh_ref[...]) + br_ref[...]
    lane = _iota((tm, LANES), 1)
    lg = jnp.where(lane < N_EXPERTS, logits, NEG)
    sel = jnp.zeros((tm, LANES), F32)
    idxf = jnp.zeros((tm, LANES), F32)
    ew = jnp.zeros((tm, LANES), F32)
    top = None
    for r in range(TOP_K):
        mx = jnp.max(lg, axis=1, keepdims=True)
        idx = jnp.min(jnp.where(lg == mx, lane, LANES), axis=1, keepdims=True)
        hit = lane == idx
        top = mx if top is None else top
        sel = jnp.where(hit, 1.0, sel)
        idxf = jnp.where(lane == r, idx.astype(F32), idxf)
        ew = jnp.where(lane == r, jnp.exp(mx - top), ew)
        lg = jnp.where(hit, NEG, lg)
    sel_ref[...] = sel
    idx_ref[...] = idxf
    gw_ref[...] = ew / jnp.sum(ew, axis=1, keepdims=True)


def _merge(fox_o, y_gla, moba_o, p, x, wb, wo, lg, lb, wrh, wrl, br, tm=256):
    s = x.shape[0]
    head_spec = pl.BlockSpec((N_HEADS, tm, HEAD_DIM), lambda i: (0, i, 0))
    row = lambda n: pl.BlockSpec((1, n), lambda i: (0, 0))
    small = jax.ShapeDtypeStruct((s, LANES), F32)
    small_spec = pl.BlockSpec((tm, LANES), lambda i: (i, 0))
    return pl.pallas_call(
        _merge_kernel,
        grid=(s // tm,),
        in_specs=[head_spec,
                  pl.BlockSpec((tm, BRANCH_WIDTH), lambda i: (i, 0)),
                  head_spec,
                  pl.BlockSpec((tm, D_MODEL), lambda i: (i, 0)),
                  pl.BlockSpec((tm, D_MODEL), lambda i: (i, 1)),
                  pl.BlockSpec((tm, D_MODEL), lambda i: (i, 2)),
                  pl.BlockSpec((tm, D_MODEL), lambda i: (i, 0)),
                  pl.BlockSpec((3, BRANCH_WIDTH, D_MODEL), lambda i: (0, 0, 0)),
                  pl.BlockSpec((D_MODEL, D_MODEL), lambda i: (0, 0)),
                  row(D_MODEL), row(D_MODEL),
                  pl.BlockSpec((D_MODEL, LANES), lambda i: (0, 0)),
                  pl.BlockSpec((D_MODEL, 2 * LANES), lambda i: (0, 0)),
                  row(LANES)],
        out_specs=[pl.BlockSpec((tm * ROW_TILE, LANES), lambda i: (i, 0)),
                   small_spec, small_spec, small_spec],
        out_shape=[jax.ShapeDtypeStruct((s * ROW_TILE, LANES), F32), small, small, small],
        compiler_params=_params(("parallel",)),
    )(fox_o, y_gla, moba_o, p, p, p, x, wb, wo, lg, lb, wrh, wrl, br)


def _rank_kernel(sel_ref, idx_ref, dest_ref, cnt_ref, carry_ref, start_ref):
    phase = pl.program_id(0)
    i = pl.program_id(1)
    t = sel_ref.shape[0]
    sel = sel_ref[...]
    chosen = jnp.sum(sel, axis=0, keepdims=True)

    @pl.when(jnp.logical_and(phase == 0, i == 0))
    def _():
        carry_ref[...] = jnp.zeros_like(carry_ref)

    @pl.when(jnp.logical_and(phase == 1, i == 0))
    def _():
        counts = carry_ref[...]
        cnt_ref[...] = counts
        blocks = jnp.floor((counts + (EXPERT_BLOCK - 1)) * (1.0 / EXPERT_BLOCK))
        before = (_iota((LANES, LANES), 0) < _iota((LANES, LANES), 1)).astype(BF16)
        first = _dot(jnp.broadcast_to(blocks, (ROW_TILE, LANES)).astype(BF16), before)
        start_ref[...] = first[0:1, :] * float(EXPERT_BLOCK)
        carry_ref[...] = jnp.zeros_like(carry_ref)

    @pl.when(phase == 1)
    def _():
        stri = (_iota((t, t), 0) > _iota((t, t), 1)).astype(BF16)
        pos = _dot(stri, sel.astype(BF16)) + carry_ref[...] + start_ref[...]
        lane = _iota((t, LANES), 1).astype(F32)
        idxf = idx_ref[...]
        dest = jnp.zeros((t, LANES), F32)
        for r in range(TOP_K):
            mine = jnp.sum(jnp.where(lane == idxf[:, r:r + 1], pos, 0.0), axis=1, keepdims=True)
            dest = jnp.where(lane == float(r), mine, dest)
        dest_ref[...] = dest.T[:ROW_TILE, :].astype(jnp.int32)

    carry_ref[...] += chosen


def _rank(sel, idxf, t=512):
    s = sel.shape[0]
    spec = pl.BlockSpec((t, LANES), lambda ph, i: (i, 0))
    return pl.pallas_call(
        _rank_kernel,
        grid=(2, s // t),
        in_specs=[spec, spec],
        out_specs=[pl.BlockSpec((ROW_TILE, t), lambda ph, i: (0, i * ph)),
                   pl.BlockSpec((1, LANES), lambda ph, i: (0, 0))],
        out_shape=[jax.ShapeDtypeStruct((ROW_TILE, s), jnp.int32),
                   jax.ShapeDtypeStruct((1, LANES), F32)],
        scratch_shapes=[pltpu.VMEM((1, LANES), F32), pltpu.VMEM((1, LANES), F32)],
        compiler_params=_params(("arbitrary", "arbitrary")),
    )(sel, idxf)


DMA_UNROLL = 4


def _token_rows(ref, token):
    return ref.at[pl.ds(pl.multiple_of(token * ROW_TILE, ROW_TILE), ROW_TILE), :]


def _dispatch_kernel(dest_ref, x_ref, xin_ref, sem):
    n = dest_ref.shape[1]

    def issue(g, _):
        for u in range(DMA_UNROLL):
            t = g * DMA_UNROLL + u
            for k in range(TOP_K):
                pltpu.make_async_copy(_token_rows(x_ref, t), _token_rows(xin_ref, dest_ref[k, t]),
                                      sem).start(priority=k % 2)
        return 0

    lax.fori_loop(0, n // DMA_UNROLL, issue, 0)
    for k in range(TOP_K):
        pltpu.make_async_copy(x_ref, xin_ref.at[pl.ds(0, n * ROW_TILE), :], sem).wait()


def _dispatch(dest, x1_2d, n_rows, td=512):
    s = x1_2d.shape[0] // ROW_TILE
    return pl.pallas_call(
        _dispatch_kernel,
        grid=(s // td,),
        in_specs=[pl.BlockSpec((ROW_TILE, td), lambda i: (0, i), memory_space=pltpu.SMEM),
                  pl.BlockSpec((td * ROW_TILE, LANES), lambda i: (i, 0))],
        out_specs=pl.BlockSpec(memory_space=pl.ANY),
        out_shape=jax.ShapeDtypeStruct((n_rows * ROW_TILE, LANES), F32),
        scratch_shapes=[pltpu.SemaphoreType.DMA(())],
        compiler_params=pltpu.CompilerParams(dimension_semantics=("arbitrary",),
                                             has_side_effects=True),
    )(dest, x1_2d)


HALF = LANES // 2
W_CAST_ROWS = 64


def _expert_kernel(be_ref, nv_ref, xin_ref, wgu_ref, bgu_ref, wdn_ref, bdn_ref, y_ref,
                   wgu_s, wdn_s, perm_s):
    b = pl.program_id(0)
    blk = y_ref.shape[0] // ROW_TILE
    nvalid = nv_ref[b]
    new_expert = jnp.logical_or(b == 0, be_ref[b] != be_ref[jnp.maximum(b - 1, 0)])

    @pl.when(jnp.logical_and(new_expert, nvalid > 0))
    def _():
        def cast(r, _):
            r0 = pl.multiple_of(r * W_CAST_ROWS, W_CAST_ROWS)
            wgu_s[pl.ds(r0, W_CAST_ROWS), :] = wgu_ref[0, pl.ds(r0, W_CAST_ROWS), :].astype(BF16)
            return 0

        lax.fori_loop(0, D_MODEL // W_CAST_ROWS, cast, 0)
        for c in range(D_MODEL // LANES):
            cols = slice(c * LANES, (c + 1) * LANES)
            for g in range(D_MODEL // LANES):
                lo = g * LANES
                perm_s[c, pl.ds(lo, HALF, stride=2), :] = wdn_ref[0, lo:lo + HALF, cols]
                perm_s[c, pl.ds(lo + 1, HALF, stride=2), :] = wdn_ref[0, lo + HALF:lo + LANES, cols]
            wdn_s[:, cols] = perm_s[c].astype(BF16)

    @pl.when(nvalid > 0)
    def _():
        x = _load_row_tiles(xin_ref, blk)
        x = jnp.where(_iota((blk, D_MODEL), 0) < nvalid, x, 0.0).astype(BF16)
        even = _iota((blk, LANES), 1) % 2 == 0
        acts = []
        for g in range(D_MODEL // LANES):
            lo = g * 2 * LANES
            h = _dot(x, wgu_s[:, lo:lo + 2 * LANES]) + bgu_ref[0, :, lo:lo + 2 * LANES]
            h_a = h[:, :LANES]
            h_b = h[:, LANES:]
            gate = jnp.where(even, h_a, pltpu.roll(h_b, 1, 1))
            up = jnp.where(even, pltpu.roll(h_a, LANES - 1, 1), h_b)
            gate = jnp.minimum(gate, SWIGLU_LIMIT)
            up = jnp.clip(up, -SWIGLU_LIMIT, SWIGLU_LIMIT)
            acts.append(((up + 1.0) * (gate * _sigmoid(SWIGLU_ALPHA * gate))).astype(BF16))
        act = jnp.concatenate(acts, axis=1)
        _store_row_tiles(y_ref, _dot(act, wdn_s[...]) + bdn_ref[0])

    @pl.when(nvalid <= 0)
    def _():
        y_ref[...] = jnp.zeros_like(y_ref)


def _experts(block_expert, block_valid, xin2d, w_gu, b_gu, w_dn, b_dn, layer):
    blk = EXPERT_BLOCK
    nb = block_expert.shape[0]
    x_spec = pl.BlockSpec((blk * ROW_TILE, LANES), lambda b, be, nv: (b, 0))
    which = lambda b, be, nv: (layer * N_EXPERTS + be[b], 0, 0)
    return pl.pallas_call(
        _expert_kernel,
        grid_spec=pltpu.PrefetchScalarGridSpec(
            num_scalar_prefetch=2,
            grid=(nb,),
            in_specs=[x_spec,
                      pl.BlockSpec((1, D_MODEL, 2 * D_MODEL), which),
                      pl.BlockSpec((1, 1, 2 * D_MODEL), which),
                      pl.BlockSpec((1, D_MODEL, D_MODEL), which),
                      pl.BlockSpec((1, 1, D_MODEL), which)],
            out_specs=x_spec,
            scratch_shapes=[pltpu.VMEM((D_MODEL, 2 * D_MODEL), BF16),
                            pltpu.VMEM((D_MODEL, D_MODEL), BF16),
                            pltpu.VMEM((D_MODEL // LANES, D_MODEL, LANES), F32)]),
        out_shape=jax.ShapeDtypeStruct((nb * blk * ROW_TILE, LANES), F32),
        compiler_params=pltpu.CompilerParams(dimension_semantics=("arbitrary",),
                                             vmem_limit_bytes=56 * 1024 * 1024),
    )(block_expert, block_valid, xin2d, w_gu, b_gu, w_dn, b_dn)


def _combine_kernel(dest_ref, next_ref, yb_ref, gw_ref, x1_ref, lg_ref, lb_ref, out_ref,
                    buf_ref, sem):
    i = pl.program_id(0)
    tc = out_ref.shape[0]
    slot = i % 2

    def gather(rows_ref, into):
        def body(g, _):
            for u in range(DMA_UNROLL):
                t = g * DMA_UNROLL + u
                for k in range(TOP_K):
                    pltpu.make_async_copy(_token_rows(yb_ref, rows_ref[k, t]),
                                          _token_rows(buf_ref.at[into, k], t),
                                          sem.at[into]).start(priority=k % 2)
            return 0

        lax.fori_loop(0, tc // DMA_UNROLL, body, 0)

    @pl.when(i == 0)
    def _():
        gather(dest_ref, 0)

    @pl.when(i + 1 < pl.num_programs(0))
    def _():
        gather(next_ref, 1 - slot)

    for k in range(TOP_K):
        pltpu.make_async_copy(yb_ref.at[pl.ds(0, tc * ROW_TILE), :], buf_ref.at[slot, k],
                              sem.at[slot]).wait()

    gw = gw_ref[...]
    f = jnp.zeros((tc, D_MODEL), F32)
    for k in range(TOP_K):
        f += gw[:, k:k + 1] * _load_row_tiles(buf_ref.at[slot, k], tc)
    z = ALPHA * _load_row_tiles(x1_ref, tc) + f
    out_ref[...] = _layer_norm(z, lg_ref[...], lb_ref[...])


def _combine(dest, yb_rows, gw, x1_2d, lg, lb, tc=256):
    s = gw.shape[0]
    n = s // tc
    row = pl.BlockSpec((1, D_MODEL), lambda i: (0, 0))
    return pl.pallas_call(
        _combine_kernel,
        grid=(n,),
        in_specs=[pl.BlockSpec((ROW_TILE, tc), lambda i: (0, i), memory_space=pltpu.SMEM),
                  pl.BlockSpec((ROW_TILE, tc), lambda i: (0, jnp.minimum(i + 1, n - 1)),
                               memory_space=pltpu.SMEM),
                  pl.BlockSpec(memory_space=pl.ANY),
                  pl.BlockSpec((tc, LANES), lambda i: (i, 0)),
                  pl.BlockSpec((tc * ROW_TILE, LANES), lambda i: (i, 0)),
                  row, row],
        out_specs=pl.BlockSpec((tc, D_MODEL), lambda i: (i, 0)),
        out_shape=jax.ShapeDtypeStruct((s, D_MODEL), F32),
        scratch_shapes=[pltpu.VMEM((2, TOP_K, tc * ROW_TILE, LANES), F32),
                        pltpu.SemaphoreType.DMA((2,))],
        compiler_params=_params(("arbitrary",)),
    )(dest, dest, yb_rows, gw, x1_2d, lg, lb)


W_IN_SIZES = (256, 256, 256, 4, 128, 128, 256, 16, 256, 256, 256, 256, 3 * D_MODEL)
W_IN_WIDTH = sum(W_IN_SIZES)


def _w_in_kernel(w_ref, main_ref, tail_ref):
    w = w_ref[0]
    offs = [0]
    for n in W_IN_SIZES:
        offs.append(offs[-1] + n)
    fq, fk, fv, ff, gq, gk, gv, ga, gr, mq, mk, mv, gates = (
        w[:, offs[i]:offs[i + 1]] for i in range(len(W_IN_SIZES)))
    main_ref[...] = jnp.concatenate(
        [gates, fq, fk, fv, mq, mk, mv, gq, gk, gv, gr], axis=1).astype(BF16)
    pad = jnp.zeros((w.shape[0], LANES - 4 - GLA_RANK), F32)
    tail_ref[...] = jnp.concatenate([ff, ga, pad], axis=1).astype(BF16)


def _permute_w_in(w_in, layer, tr=128):
    return pl.pallas_call(
        _w_in_kernel,
        grid=(D_MODEL // tr,),
        in_specs=[pl.BlockSpec((1, tr, W_IN_WIDTH), lambda i: (layer, i, 0))],
        out_specs=[pl.BlockSpec((tr, P_WIDTH), lambda i: (i, 0)),
                   pl.BlockSpec((tr, LANES), lambda i: (i, 0))],
        out_shape=[jax.ShapeDtypeStruct((D_MODEL, P_WIDTH), BF16),
                   jax.ShapeDtypeStruct((D_MODEL, LANES), BF16)],
        compiler_params=_params(("parallel",)),
    )(w_in)


def _pad_row(v, n=LANES, fill=0.0):
    return jnp.pad(v.astype(F32), (0, n - v.shape[0]), constant_values=fill)[None, :]


def _mixer_layer(x, w_in, layer, fox_fb, gla_a_up, gla_a_b, gla_norm_g, w_branch, w_out, ln_g, ln_b,
                 w_router, b_router):
    w_main, w_tail = _permute_w_in(w_in, layer)
    p, tail = _inproj(x, w_main, w_tail)

    s = x.shape[0]
    tq, tk = _flash_tiles(s)
    fox_qt, fox_k, fox_vt, fox_stats = _fox_prep(p, tail, _pad_row(fox_fb))
    fox_o = _flash(_fox_first_tile(fox_stats, s, tq, tk), fox_qt, fox_k, fox_vt, tq, tk)
    moba_o = _flash(jnp.zeros((N_HEADS * (s // tq),), jnp.int32), *_moba_prep(p), tq, tk)
    aup_pad = jnp.zeros((LANES, GLA_KW), F32).at[TAIL_GA:TAIL_GA + GLA_RANK].set(gla_a_up)
    y_gla = _gla(p, tail, aup_pad, gla_a_b[None, :], gla_norm_g[None, :])

    wr = jnp.pad(w_router, ((0, 0), (0, LANES - N_EXPERTS)))
    wr_hi = wr.astype(BF16)
    wr_lo = jnp.concatenate([wr_hi, (wr - wr_hi.astype(F32)).astype(BF16)], axis=1)
    return _merge(fox_o, y_gla, moba_o, p, x, w_branch.astype(BF16), w_out.astype(BF16),
                  ln_g[None, :], ln_b[None, :], wr_hi, wr_lo, _pad_row(b_router))


def _moe_layer(x1_2d, sel, idxf, gw, w_gu, b_gu, w_dn, b_dn, ln_g, ln_b, layer):
    s = sel.shape[0]
    blk = EXPERT_BLOCK
    dest, counts = _rank(sel, idxf)

    counts = counts[0, :N_EXPERTS].astype(jnp.int32)
    padded = (counts + blk - 1) // blk * blk
    pad_end = jnp.cumsum(padded)
    pad_start = pad_end - padded
    n_blocks = (s * TOP_K + N_EXPERTS * (blk - 1)) // blk + 1
    block_start = jnp.arange(n_blocks, dtype=jnp.int32) * blk
    active = block_start < pad_end[-1]
    block_expert = jnp.sum((block_start[:, None] >= pad_end[None, :]).astype(jnp.int32), axis=1)
    last_expert = jnp.max(jnp.where(counts > 0, jnp.arange(N_EXPERTS, dtype=jnp.int32), 0))
    block_expert = jnp.where(active, block_expert, last_expert)
    block_valid = jnp.clip(pad_start[block_expert] + counts[block_expert] - block_start, 0, blk)
    block_valid = jnp.where(active, block_valid, 0).astype(jnp.int32)

    xin = _dispatch(dest, x1_2d, n_blocks * blk)
    yb = _experts(block_expert, block_valid, xin, w_gu, b_gu, w_dn, b_dn, layer)
    return _combine(dest, yb, gw, x1_2d, ln_g[None, :], ln_b[None, :])


def kernel(x, w_in, fox_fb, gla_a_up, gla_a_b, gla_norm_g, w_branch, w_out, ln1_g, ln1_b,
           w_router, b_router, w_gu, b_gu, w_dn, b_dn, ln2_g, ln2_b):
    b, s, d = x.shape
    assert b == 1 and d == D_MODEL
    h = x.reshape(s, d)
    n_le = DEPTH * N_EXPERTS
    w_gu = w_gu.reshape(n_le, D_MODEL, 2 * D_MODEL)
    b_gu = b_gu.reshape(n_le, 1, 2 * D_MODEL)
    w_dn = w_dn.reshape(n_le, D_MODEL, D_MODEL)
    b_dn = b_dn.reshape(n_le, 1, D_MODEL)
    for l in range(DEPTH):
        x1_2d, sel, idxf, gw = _mixer_layer(
            h, w_in, l, fox_fb[l], gla_a_up[l], gla_a_b[l], gla_norm_g[l], w_branch[l], w_out[l],
            ln1_g[l], ln1_b[l], w_router[l], b_router[l])
        h = _moe_layer(x1_2d, sel, idxf, gw, w_gu, b_gu, w_dn, b_dn, ln2_g[l], ln2_b[l], l)
    return h.reshape(b, s, d)
```

```python
import functools

import jax
import jax.numpy as jnp
from jax import lax
from jax.experimental import pallas as pl
from jax.experimental.pallas import tpu as pltpu

F32 = jnp.float32
BF16 = jnp.bfloat16

D_MODEL = 1024
DEPTH = 4
N_HEADS = 4
HEAD_DIM = 64
GLA_DK = 32
GLA_KW = N_HEADS * GLA_DK
GLA_RANK = 16
GLA_TAU = 16.0
MOBA_BLOCK = 256
MOBA_TOPK = 3
BRANCH_WIDTH = 256
N_EXPERTS = 32
TOP_K = 4
SWIGLU_LIMIT = 7.0
SWIGLU_ALPHA = 1.702
ALPHA = (2 * DEPTH) ** 0.25
LN_EPS = 1e-5
RMS_EPS = 1e-6
LOG2E = 1.4426950408889634

LANES = 128
ROW_TILE = 8
NEG = -1e30
UNSELECTED = -32768.0
VMEM_LIMIT = 48 * 1024 * 1024

P_GATES = 0
P_FOX = 3 * D_MODEL
P_MOBA = P_FOX + 3 * BRANCH_WIDTH
P_GLA = P_MOBA + 3 * BRANCH_WIDTH
P_WIDTH = P_GLA + 2 * GLA_KW + 2 * BRANCH_WIDTH
TAIL_FF = 0
TAIL_GA = 4

EXPERT_BLOCK = 512


def _params(sem):
    return pltpu.CompilerParams(dimension_semantics=sem, vmem_limit_bytes=VMEM_LIMIT)


def _split3(x):
    hi = x.astype(BF16)
    r1 = x - hi.astype(F32)
    mid = r1.astype(BF16)
    lo = (r1 - mid.astype(F32)).astype(BF16)
    return hi, mid, lo


def _split2(x):
    hi = x.astype(BF16)
    lo = (x - hi.astype(F32)).astype(BF16)
    return hi, lo


def _dot(a, b):
    return jnp.dot(a, b, preferred_element_type=F32)


def _dot_nt(a, b):
    return lax.dot_general(a, b, (((1,), (1,)), ((), ())), preferred_element_type=F32)


def _dot_tn(a, b):
    return lax.dot_general(a, b, (((0,), (0,)), ((), ())), preferred_element_type=F32)


def _log_sigmoid(t):
    return jnp.minimum(t, 0.0) - jnp.log1p(jnp.exp(-jnp.abs(t)))


def _sigmoid(t):
    return 0.5 * jnp.tanh(0.5 * t) + 0.5


def _iota(shape, axis):
    return lax.broadcasted_iota(jnp.int32, shape, axis)


def _transpose_bf16(x):
    return x.astype(F32).T.astype(BF16)


BF16_ROWS = 16
V_ROWS = HEAD_DIM + BF16_ROWS


def _values_t(v):
    ones_row = jnp.where(_iota((BF16_ROWS, v.shape[0]), 0) == 0, 1.0, 0.0).astype(BF16)
    return jnp.concatenate([_transpose_bf16(v), ones_row], axis=0)


def _aug_specs(s, t):
    aug = jax.ShapeDtypeStruct((N_HEADS, s, LANES), BF16)
    aug_spec = pl.BlockSpec((N_HEADS, t, LANES), lambda i: (0, i, 0))
    aug_t = jax.ShapeDtypeStruct((N_HEADS, LANES, s), BF16)
    aug_t_spec = pl.BlockSpec((N_HEADS, LANES, t), lambda i: (0, 0, i))
    aug_v = jax.ShapeDtypeStruct((N_HEADS, V_ROWS, s), BF16)
    aug_v_spec = pl.BlockSpec((N_HEADS, V_ROWS, t), lambda i: (0, 0, i))
    return aug, aug_spec, aug_t, aug_t_spec, aug_v, aug_v_spec


def _inproj_kernel(x_ref, w_ref, wt_ref, sc_ref, p_ref, t_ref, xb_ref):
    @pl.when(pl.program_id(1) == 0)
    def _():
        xb_ref[...] = x_ref[...].astype(BF16)
        t_ref[...] = _dot(xb_ref[...], wt_ref[...])

    p_ref[...] = (_dot(xb_ref[...], w_ref[...]) * sc_ref[...]).astype(BF16)


def _inproj(x, w, wt, tm=1024, tn=1792):
    s = x.shape[0]
    col = jnp.arange(P_WIDTH)
    is_q = ((col >= P_FOX) & (col < P_FOX + BRANCH_WIDTH)) | ((col >= P_MOBA) & (col < P_MOBA + BRANCH_WIDTH))
    scale = jnp.where(is_q, HEAD_DIM ** -0.5 * LOG2E, 1.0).astype(F32)[None, :]
    return pl.pallas_call(
        _inproj_kernel,
        grid=(s // tm, P_WIDTH // tn),
        in_specs=[pl.BlockSpec((tm, D_MODEL), lambda i, j: (i, 0)),
                  pl.BlockSpec((D_MODEL, tn), lambda i, j: (0, j)),
                  pl.BlockSpec((D_MODEL, LANES), lambda i, j: (0, 0)),
                  pl.BlockSpec((1, tn), lambda i, j: (0, j))],
        out_specs=[pl.BlockSpec((tm, tn), lambda i, j: (i, j)),
                   pl.BlockSpec((tm, LANES), lambda i, j: (i, 0))],
        out_shape=[jax.ShapeDtypeStruct((s, P_WIDTH), BF16),
                   jax.ShapeDtypeStruct((s, LANES), F32)],
        scratch_shapes=[pltpu.VMEM((tm, D_MODEL), BF16)],
        compiler_params=_params(("parallel", "arbitrary")),
    )(x, w, wt, scale)


def _fox_prep_kernel(q_ref, k_ref, v_ref, t_ref, fb_ref, qa_ref, ka_ref, va_ref, st_ref, carry_ref):
    t = q_ref.shape[0]

    @pl.when(pl.program_id(0) == 0)
    def _():
        carry_ref[...] = jnp.zeros_like(carry_ref)

    ls = _log_sigmoid(t_ref[...] + fb_ref[...])
    tri = (_iota((t, t), 0) >= _iota((t, t), 1)).astype(BF16)
    hi, mid, lo = _split3(ls)
    c = _dot(tri, hi) + _dot(tri, mid) + _dot(tri, lo) + carry_ref[...]
    carry_ref[...] = c[t - 1:t, :]

    c2 = LOG2E * c

    lane = _iota((t, HEAD_DIM), 1)
    q_aug = jnp.where(_iota((HEAD_DIM, t), 0) < 3, 1.0, 0.0).astype(BF16)
    q = q_ref[...]
    k = k_ref[...]
    v = v_ref[...]
    head_of = (_iota((BRANCH_WIDTH, LANES), 0) // HEAD_DIM == _iota((BRANCH_WIDTH, LANES), 1))
    head_of = head_of.astype(BF16)

    def max_norm(x):
        xf = x.astype(F32)
        hi, lo = _split2(xf * xf)
        norm2 = jnp.max(_dot(hi, head_of) + _dot(lo, head_of), axis=0, keepdims=True)
        return jnp.sqrt(norm2 * (1.0 + 2.0 ** -8))

    srow = _iota((ROW_TILE, LANES), 0)
    stat = jnp.where(srow == 0, max_norm(q), jnp.where(srow == 1, max_norm(k), jnp.where(
        srow == 2, c2[0:1, :], jnp.where(srow == 3, c2[t - 1:t, :], 0.0))))
    for h in range(N_HEADS):
        sl = slice(h * HEAD_DIM, (h + 1) * HEAD_DIM)
        nhi, nmid, nlo = (piece.astype(F32) for piece in _split3(-c2[:, h:h + 1]))
        k_aug = jnp.where(lane == 0, nhi, jnp.where(lane == 1, nmid, jnp.where(lane == 2, nlo, 0.0)))
        qa_ref[h] = jnp.concatenate([_transpose_bf16(q[:, sl]), q_aug], axis=0)
        ka_ref[h] = jnp.concatenate([k[:, sl], k_aug.astype(BF16)], axis=1)
        va_ref[h] = _values_t(v[:, sl])
    st_ref[0] = stat


def _fox_prep(p, tail, fb_row, t=512):
    s = p.shape[0]
    cb = P_FOX // BRANCH_WIDTH
    aug, aug_spec, aug_t, aug_t_spec, aug_v, aug_v_spec = _aug_specs(s, t)
    return pl.pallas_call(
        _fox_prep_kernel,
        grid=(s // t,),
        in_specs=[pl.BlockSpec((t, BRANCH_WIDTH), lambda i: (i, cb)),
                  pl.BlockSpec((t, BRANCH_WIDTH), lambda i: (i, cb + 1)),
                  pl.BlockSpec((t, BRANCH_WIDTH), lambda i: (i, cb + 2)),
                  pl.BlockSpec((t, LANES), lambda i: (i, 0)),
                  pl.BlockSpec((1, LANES), lambda i: (0, 0))],
        out_specs=[aug_t_spec, aug_spec, aug_v_spec,
                   pl.BlockSpec((1, ROW_TILE, LANES), lambda i: (i, 0, 0))],
        out_shape=[aug_t, aug, aug_v, jax.ShapeDtypeStruct((s // t, ROW_TILE, LANES), F32)],
        scratch_shapes=[pltpu.VMEM((1, LANES), F32)],
        compiler_params=_params(("arbitrary",)),
    )(p, p, p, tail, fb_row)


SKIP_LOG2 = 48.0


def _fox_first_tile(stats, s, tq, tk):
    tp = s // stats.shape[0]
    qn, kn, c_first, c_last = (stats[:, r, :N_HEADS] for r in range(4))
    nq, nk = s // tq, s // tk
    qn = jnp.max(qn.reshape(nq, tq // tp, N_HEADS), axis=1)
    c_q = c_first.reshape(nq, tq // tp, N_HEADS)[:, 0, :]
    c_k = c_last.reshape(nk, tk // tp, N_HEADS)[:, -1, :]
    bound = (2.0 * qn * jnp.max(kn, axis=0))[:, None, :] + c_q[:, None, :] - c_k[None, :, :]
    j = jnp.arange(nk, dtype=jnp.int32)[None, :, None]
    first = jnp.min(jnp.where(bound >= -SKIP_LOG2, j, nk), axis=1)
    return first.T.reshape(-1).astype(jnp.int32)


def _moba_prep_kernel(q_ref, k_ref, v_ref, qa_ref, ka_ref, va_ref, kmean_ref):
    i = pl.program_id(0)
    t = q_ref.shape[0]
    nbl = kmean_ref.shape[0]

    @pl.when(i == 0)
    def _():
        kmean_ref[...] = jnp.zeros_like(kmean_ref)

    q = q_ref[...]
    k = k_ref[...]
    v = v_ref[...]
    km = kmean_ref[...]
    k_aug = jnp.where(_iota((t, HEAD_DIM), 1) == i, 1.0, 0.0).astype(BF16)
    blk = _iota((HEAD_DIM, t), 0)
    for h in range(N_HEADS):
        sl = slice(h * HEAD_DIM, (h + 1) * HEAD_DIM)
        qt = _transpose_bf16(q[:, sl])
        kh_hi, kh_lo = _split2(km[:HEAD_DIM, sl])
        gate = _dot(kh_hi, qt) + _dot(kh_lo, qt)
        g = jnp.where(blk < i, gate, NEG)
        bias = jnp.where(blk == i, 0.0, UNSELECTED)
        for _ in range(MOBA_TOPK):
            mx = jnp.max(g, axis=0, keepdims=True)
            idx = jnp.min(jnp.where(g == mx, blk, HEAD_DIM), axis=0, keepdims=True)
            pick = jnp.logical_and(blk == idx, mx > 0.5 * NEG)
            bias = jnp.where(pick, 0.0, bias)
            g = jnp.where(blk == idx, NEG, g)
        qa_ref[h] = jnp.concatenate([qt, bias.astype(BF16)], axis=0)
        ka_ref[h] = jnp.concatenate([k[:, sl], k_aug], axis=1)
        va_ref[h] = _values_t(v[:, sl])
    kmean = jnp.mean(k.astype(F32), axis=0, keepdims=True)
    kmean_ref[...] = jnp.where(_iota((nbl, BRANCH_WIDTH), 0) == i, kmean, km)


def _moba_prep(p):
    s = p.shape[0]
    t = MOBA_BLOCK
    assert s % t == 0 and s // t <= HEAD_DIM, "block one-hot must fit the 64 augmentation lanes"
    cb = P_MOBA // BRANCH_WIDTH
    aug, aug_spec, aug_t, aug_t_spec, aug_v, aug_v_spec = _aug_specs(s, t)
    return pl.pallas_call(
        _moba_prep_kernel,
        grid=(s // t,),
        in_specs=[pl.BlockSpec((t, BRANCH_WIDTH), lambda i: (i, cb)),
                  pl.BlockSpec((t, BRANCH_WIDTH), lambda i: (i, cb + 1)),
                  pl.BlockSpec((t, BRANCH_WIDTH), lambda i: (i, cb + 2))],
        out_specs=[aug_t_spec, aug_spec, aug_v_spec],
        out_shape=[aug_t, aug, aug_v],
        scratch_shapes=[pltpu.VMEM((HEAD_DIM, BRANCH_WIDTH), F32)],
        compiler_params=_params(("arbitrary",)),
    )(p, p, p)


def _flash_kernel(first_ref, qt_ref, k_ref, vt_ref, o_ref, s0_ref, s1_ref, s2_ref, *, tq, tk):
    h = pl.program_id(0)
    i = pl.program_id(1)
    qt = qt_ref[0]
    n_diag = tq // tk
    n_full = i * n_diag

    def scores(j, first_query=0):
        return _dot(k_ref[0, pl.ds(pl.multiple_of(j * tk, tk), tk), :], qt[:, first_query:])

    def absorb(j, s, m, acc):
        m_new = jnp.maximum(m, jnp.max(s, axis=0, keepdims=True))
        p = jnp.exp2(s - m_new).astype(BF16)
        vt = vt_ref[0, :, pl.ds(pl.multiple_of(j * tk, tk), tk)]
        return m_new, jnp.exp2(m - m_new) * acc + _dot(vt, p)

    j0 = jnp.minimum(first_ref[h * pl.num_programs(1) + i], n_full)
    m = jnp.full((1, tq), NEG, F32)
    acc = jnp.zeros((V_ROWS, tq), F32)

    extra = (n_full - j0) % 3

    def plain(j, carry):
        return absorb(j, scores(j), *carry)

    m, acc = lax.fori_loop(j0, j0 + extra, plain, (m, acc))
    j0 = j0 + extra

    s0_ref[...] = scores(j0)

    def triple(g, carry):
        j = j0 + 3 * g
        s1_ref[...] = scores(j + 1)
        carry = absorb(j, s0_ref[...], *carry)
        s2_ref[...] = scores(j + 2)
        carry = absorb(j + 1, s1_ref[...], *carry)
        s0_ref[...] = scores(j + 3)
        return absorb(j + 2, s2_ref[...], *carry)

    m, acc = lax.fori_loop(0, (n_full - j0) // 3, triple, (m, acc))

    causal = _iota((tk, tk), 0) <= _iota((tk, tk), 1)
    s = s0_ref[...]
    for d in range(n_diag):
        lo = d * tk
        s_next = scores(n_full + d + 1, lo + tk) if d + 1 < n_diag else None
        tri = jnp.where(causal, s[:, :tk], NEG)
        s = tri if lo + tk == tq else jnp.concatenate([tri, s[:, tk:]], axis=1)
        m_d, acc_d = absorb(n_full + d, s, m[:, lo:], acc[:, lo:])
        m = m_d if lo == 0 else jnp.concatenate([m[:, :lo], m_d], axis=1)
        acc = acc_d if lo == 0 else jnp.concatenate([acc[:, :lo], acc_d], axis=1)
        s = s_next
    o_ref[0] = (acc[:HEAD_DIM, :] / acc[HEAD_DIM:HEAD_DIM + 1, :]).T.astype(o_ref.dtype)


def _flash(first_tile, qt, ka, vt, tq, tk):
    nh, s, _ = ka.shape
    return pl.pallas_call(
        functools.partial(_flash_kernel, tq=tq, tk=tk),
        grid_spec=pltpu.PrefetchScalarGridSpec(
            num_scalar_prefetch=1,
            grid=(nh, s // tq),
            in_specs=[pl.BlockSpec((1, LANES, tq), lambda h, i, f: (h, 0, i)),
                      pl.BlockSpec((1, s, LANES), lambda h, i, f: (h, 0, 0)),
                      pl.BlockSpec((1, V_ROWS, s), lambda h, i, f: (h, 0, 0))],
            out_specs=pl.BlockSpec((1, tq, HEAD_DIM), lambda h, i, f: (h, i, 0)),
            scratch_shapes=[pltpu.VMEM((tk, tq), F32)] * 3),
        out_shape=jax.ShapeDtypeStruct((nh, s, HEAD_DIM), BF16),
        compiler_params=_params(("parallel", "parallel")),
    )(first_tile, qt, ka, vt)


def _flash_tiles(s):
    tq = min(1024, s)
    return tq, min(512, tq)


GLA_SUB = 16
GLA_UNROLL = 4


def _gla_kernel(q_ref, k_ref, v_ref, r_ref, t_ref, aup_ref, ab_ref, g_ref, y_ref,
                st_ref, b_ref, o_ref):
    t = q_ref.shape[0]
    c = GLA_SUB

    @pl.when(pl.program_id(0) == 0)
    def _():
        st_ref[...] = jnp.zeros_like(st_ref)

    t_hi, t_lo = _split2(t_ref[...])
    a_hi, a_lo = _split2(aup_ref[...])
    z = _dot(t_hi, a_hi) + _dot(t_lo, a_hi) + _dot(t_hi, a_lo) + ab_ref[...]
    log_a = _log_sigmoid(z) * (1.0 / GLA_TAU)
    row = _iota((t, t), 0)
    col = _iota((t, t), 1)
    tri = jnp.logical_and(row // c == col // c, row >= col).astype(BF16)
    hi, mid, lo = _split3(log_a)
    b_ref[...] = _dot(tri, hi) + _dot(tri, mid) + _dot(tri, lo)

    expand = (_iota((GLA_KW, BRANCH_WIDTH), 0) // GLA_DK
              == _iota((GLA_KW, BRANCH_WIDTH), 1) // HEAD_DIM).astype(BF16)
    st_mask = (_iota((BRANCH_WIDTH, GLA_KW), 0) // HEAD_DIM
               == _iota((BRANCH_WIDTH, GLA_KW), 1) // GLA_DK).astype(F32)
    srow = _iota((c, GLA_KW), 0)
    scale = GLA_DK ** -0.5

    def step(r0, st):
        qs = q_ref[pl.ds(r0, c), :].astype(F32) * scale
        ks = k_ref[pl.ds(r0, c), :].astype(F32)
        vb = v_ref[pl.ds(r0, c), :]
        vf = vb.astype(F32)
        bs = b_ref[pl.ds(r0, c), :]
        b_last = bs[c - 1:c, :]

        o_inter = _dot_nt((qs * jnp.exp(bs)).astype(BF16), st.astype(BF16))

        pieces = []
        for ss in range(c):
            e = jnp.exp(jnp.where(srow >= ss, bs - bs[ss:ss + 1, :], NEG))
            pieces.append(e * qs * ks[ss:ss + 1, :])
        pm = jnp.concatenate(pieces, axis=0).astype(BF16)
        a = _dot(pm, expand)
        o_intra = a[:c, :] * vf[0:1, :]
        for ss in range(1, c):
            o_intra += a[ss * c:(ss + 1) * c, :] * vf[ss:ss + 1, :]
        o_ref[pl.ds(r0, c), :] = o_inter + o_intra

        ke = (ks * jnp.exp(b_last - bs)).astype(BF16)
        return st * jnp.exp(b_last) + _dot_tn(vb, ke) * st_mask

    def steps(g, _):
        st = st_ref[...]
        for u in range(GLA_UNROLL):
            st = step(pl.multiple_of((g * GLA_UNROLL + u) * c, c), st)
        st_ref[...] = st
        return 0

    lax.fori_loop(0, t // (c * GLA_UNROLL), steps, 0)

    o = o_ref[...]
    ones_bd = (_iota((BRANCH_WIDTH, BRANCH_WIDTH), 0) // HEAD_DIM
               == _iota((BRANCH_WIDTH, BRANCH_WIDTH), 1) // HEAD_DIM).astype(BF16)
    sq_hi, sq_mid, sq_lo = _split3(o * o)
    ms = (_dot(sq_hi, ones_bd) + _dot(sq_mid, ones_bd) + _dot(sq_lo, ones_bd)) * (1.0 / HEAD_DIM)
    gr = r_ref[...].astype(F32)
    y = o * lax.rsqrt(ms + RMS_EPS) * g_ref[...] * (gr * _sigmoid(gr))
    y_ref[...] = y.astype(y_ref.dtype)


def _gla(p, tail, aup_pad, ab_row, g_row, t=512):
    s = p.shape[0]
    cq = P_GLA // GLA_KW
    cv = (P_GLA + 2 * GLA_KW) // BRANCH_WIDTH
    return pl.pallas_call(
        _gla_kernel,
        grid=(s // t,),
        in_specs=[pl.BlockSpec((t, GLA_KW), lambda i: (i, cq)),
                  pl.BlockSpec((t, GLA_KW), lambda i: (i, cq + 1)),
                  pl.BlockSpec((t, BRANCH_WIDTH), lambda i: (i, cv)),
                  pl.BlockSpec((t, BRANCH_WIDTH), lambda i: (i, cv + 1)),
                  pl.BlockSpec((t, LANES), lambda i: (i, 0)),
                  pl.BlockSpec((LANES, GLA_KW), lambda i: (0, 0)),
                  pl.BlockSpec((1, GLA_KW), lambda i: (0, 0)),
                  pl.BlockSpec((1, BRANCH_WIDTH), lambda i: (0, 0))],
        out_specs=pl.BlockSpec((t, BRANCH_WIDTH), lambda i: (i, 0)),
        out_shape=jax.ShapeDtypeStruct((s, BRANCH_WIDTH), BF16),
        scratch_shapes=[pltpu.VMEM((BRANCH_WIDTH, GLA_KW), F32),
                        pltpu.VMEM((t, GLA_KW), F32),
                        pltpu.VMEM((t, BRANCH_WIDTH), F32)],
        compiler_params=_params(("arbitrary",)),
    )(p, p, p, p, tail, aup_pad, ab_row, g_row)


def _layer_norm(z, g, b):
    mu = jnp.mean(z, axis=1, keepdims=True)
    zc = z - mu
    var = jnp.mean(zc * zc, axis=1, keepdims=True)
    return zc * lax.rsqrt(var + LN_EPS) * g + b


def _store_row_tiles(ref, val):
    n = val.shape[0]
    for a in range(ROW_TILE):
        ref[pl.ds(a, n, stride=ROW_TILE), :] = val[:, a * LANES:(a + 1) * LANES]


def _load_row_tiles(ref, n, base=0):
    return jnp.concatenate(
        [ref[pl.ds(base + a, n, stride=ROW_TILE), :] for a in range(ROW_TILE)], axis=1)


def _merge_kernel(fox_ref, gla_ref, moba_ref, g0_ref, g1_ref, g2_ref, x_ref, wb_ref, wo_ref,
                  lg_ref, lb_ref, wrh_ref, wrl_ref, br_ref,
                  x1_ref, sel_ref, idx_ref, gw_ref):
    tm = x_ref.shape[0]

    def heads(ref):
        return jnp.concatenate([ref[h] for h in range(N_HEADS)], axis=1)

    merged = _sigmoid(g0_ref[...]) * _dot(heads(fox_ref), wb_ref[0]).astype(BF16)
    merged += _sigmoid(g1_ref[...]) * _dot(gla_ref[...], wb_ref[1]).astype(BF16)
    merged += _sigmoid(g2_ref[...]) * _dot(heads(moba_ref), wb_ref[2]).astype(BF16)
    z = ALPHA * x_ref[...] + _dot(merged, wo_ref[...])
    x1 = _layer_norm(z, lg_ref[...], lb_ref[...])
    _store_row_tiles(x1_ref, x1)

    x_hi, x_lo = _split2(x1)
    both = _dot(x_hi, wrl_ref[...])
    logits = both[:, :LANES] + both[:, LANES:] + _dot(x_lo, wrh_ref[...]) + br_ref[...]
    lane = _iota((tm, LANES), 1)
    lg = jnp.where(lane < N_EXPERTS, logits, NEG)
    sel = jnp.zeros((tm, LANES), F32)
    idxf = jnp.zeros((tm, LANES), F32)
    ew = jnp.zeros((tm, LANES), F32)
    top = None
    for r in range(TOP_K):
        mx = jnp.max(lg, axis=1, keepdims=True)
        idx = jnp.min(jnp.where(lg == mx, lane, LANES), axis=1, keepdims=True)
        hit = lane == idx
        top = mx if top is None else top
        sel = jnp.where(hit, 1.0, sel)
        idxf = jnp.where(lane == r, idx.astype(F32), idxf)
        ew = jnp.where(lane == r, jnp.exp(mx - top), ew)
        lg = jnp.where(hit, NEG, lg)
    sel_ref[...] = sel
    idx_ref[...] = idxf
    gw_ref[...] = ew / jnp.sum(ew, axis=1, keepdims=True)


def _merge(fox_o, y_gla, moba_o, p, x, wb, wo, lg, lb, wrh, wrl, br, tm=256):
    s = x.shape[0]
    head_spec = pl.BlockSpec((N_HEADS, tm, HEAD_DIM), lambda i: (0, i, 0))
    row = lambda n: pl.BlockSpec((1, n), lambda i: (0, 0))
    small = jax.ShapeDtypeStruct((s, LANES), F32)
    small_spec = pl.BlockSpec((tm, LANES), lambda i: (i, 0))
    return pl.pallas_call(
        _merge_kernel,
        grid=(s // tm,),
        in_specs=[head_spec,
                  pl.BlockSpec((tm, BRANCH_WIDTH), lambda i: (i, 0)),
                  head_spec,
                  pl.BlockSpec((tm, D_MODEL), lambda i: (i, 0)),
                  pl.BlockSpec((tm, D_MODEL), lambda i: (i, 1)),
                  pl.BlockSpec((tm, D_MODEL), lambda i: (i, 2)),
                  pl.BlockSpec((tm, D_MODEL), lambda i: (i, 0)),
                  pl.BlockSpec((3, BRANCH_WIDTH, D_MODEL), lambda i: (0, 0, 0)),
                  pl.BlockSpec((D_MODEL, D_MODEL), lambda i: (0, 0)),
                  row(D_MODEL), row(D_MODEL),
                  pl.BlockSpec((D_MODEL, LANES), lambda i: (0, 0)),
                  pl.BlockSpec((D_MODEL, 2 * LANES), lambda i: (0, 0)),
                  row(LANES)],
        out_specs=[pl.BlockSpec((tm * ROW_TILE, LANES), lambda i: (i, 0)),
                   small_spec, small_spec, small_spec],
        out_shape=[jax.ShapeDtypeStruct((s * ROW_TILE, LANES), F32), small, small, small],
        compiler_params=_params(("parallel",)),
    )(fox_o, y_gla, moba_o, p, p, p, x, wb, wo, lg, lb, wrh, wrl, br)


def _rank_kernel(sel_ref, idx_ref, dest_ref, tab_ref, carry_ref, start_ref):
    phase = pl.program_id(0)
    i = pl.program_id(1)
    t = sel_ref.shape[0]
    sel = sel_ref[...]
    chosen = jnp.sum(sel, axis=0, keepdims=True)

    @pl.when(jnp.logical_and(phase == 0, i == 0))
    def _():
        carry_ref[...] = jnp.zeros_like(carry_ref)

    @pl.when(jnp.logical_and(phase == 1, i == 0))
    def _():
        counts = carry_ref[...]
        blocks = jnp.floor((counts + (EXPERT_BLOCK - 1)) * (1.0 / EXPERT_BLOCK))
        before = (_iota((LANES, LANES), 0) < _iota((LANES, LANES), 1)).astype(BF16)
        first = _dot(jnp.broadcast_to(blocks, (ROW_TILE, LANES)).astype(BF16), before)[0:1, :]
        start_ref[...] = first * float(EXPERT_BLOCK)
        carry_ref[...] = jnp.zeros_like(carry_ref)

        nb = tab_ref.shape[0]
        b = _iota((nb, LANES), 0).astype(F32)
        lane = _iota((nb, LANES), 1)
        is_expert = lane < N_EXPERTS
        end = first + blocks
        owner = jnp.sum(jnp.where(jnp.logical_and(is_expert, b >= end), 1.0, 0.0),
                        axis=1, keepdims=True)
        total = jnp.max(jnp.where(lane[0:1, :] < N_EXPERTS, end, 0.0), axis=1, keepdims=True)
        last = jnp.max(jnp.where(counts > 0.0, lane[0:1, :].astype(F32), 0.0), axis=1, keepdims=True)
        active = b[:, 0:1] < total
        owner = jnp.where(active, owner, last)
        mine = lane.astype(F32) == owner
        rows_left = jnp.sum(jnp.where(mine, counts + (first - b) * float(EXPERT_BLOCK), 0.0),
                            axis=1, keepdims=True)
        valid = jnp.where(active, jnp.clip(rows_left, 0.0, float(EXPERT_BLOCK)), 0.0)
        tab_ref[...] = jnp.where(lane == 0, owner, jnp.where(lane == 1, valid, 0.0)).astype(jnp.int32)

    @pl.when(phase == 1)
    def _():
        stri = (_iota((t, t), 0) > _iota((t, t), 1)).astype(BF16)
        pos = _dot(stri, sel.astype(BF16)) + carry_ref[...] + start_ref[...]
        lane = _iota((t, LANES), 1).astype(F32)
        idxf = idx_ref[...]
        dest = jnp.zeros((t, LANES), F32)
        for r in range(TOP_K):
            mine = jnp.sum(jnp.where(lane == idxf[:, r:r + 1], pos, 0.0), axis=1, keepdims=True)
            dest = jnp.where(lane == float(r), mine, dest)
        dest_ref[...] = dest.T[:ROW_TILE, :].astype(jnp.int32)

    carry_ref[...] += chosen


def _n_expert_blocks(s):
    n = (s * TOP_K + N_EXPERTS * (EXPERT_BLOCK - 1)) // EXPERT_BLOCK + 1
    return -(-n // ROW_TILE) * ROW_TILE


def _rank(sel, idxf, t=512):
    s = sel.shape[0]
    nb = _n_expert_blocks(s)
    spec = pl.BlockSpec((t, LANES), lambda ph, i: (i, 0))
    return pl.pallas_call(
        _rank_kernel,
        grid=(2, s // t),
        in_specs=[spec, spec],
        out_specs=[pl.BlockSpec((ROW_TILE, t), lambda ph, i: (0, i * ph)),
                   pl.BlockSpec((nb, LANES), lambda ph, i: (0, 0))],
        out_shape=[jax.ShapeDtypeStruct((ROW_TILE, s), jnp.int32),
                   jax.ShapeDtypeStruct((nb, LANES), jnp.int32)],
        scratch_shapes=[pltpu.VMEM((1, LANES), F32), pltpu.VMEM((1, LANES), F32)],
        compiler_params=_params(("arbitrary", "arbitrary")),
    )(sel, idxf)


DMA_UNROLL = 4


def _token_rows(ref, token):
    return ref.at[pl.ds(pl.multiple_of(token * ROW_TILE, ROW_TILE), ROW_TILE), :]


def _dispatch_kernel(dest_ref, x_ref, xin_ref, sem):
    n = dest_ref.shape[1]

    def issue(g, _):
        for u in range(DMA_UNROLL):
            t = g * DMA_UNROLL + u
            for k in range(TOP_K):
                pltpu.make_async_copy(_token_rows(x_ref, t), _token_rows(xin_ref, dest_ref[k, t]),
                                      sem).start(priority=k % 2)
        return 0

    lax.fori_loop(0, n // DMA_UNROLL, issue, 0)
    for k in range(TOP_K):
        pltpu.make_async_copy(x_ref, xin_ref.at[pl.ds(0, n * ROW_TILE), :], sem).wait()


def _dispatch(dest, x1_2d, n_rows, td=512):
    s = x1_2d.shape[0] // ROW_TILE
    return pl.pallas_call(
        _dispatch_kernel,
        grid=(s // td,),
        in_specs=[pl.BlockSpec((ROW_TILE, td), lambda i: (0, i), memory_space=pltpu.SMEM),
                  pl.BlockSpec((td * ROW_TILE, LANES), lambda i: (i, 0))],
        out_specs=pl.BlockSpec(memory_space=pl.ANY),
        out_shape=jax.ShapeDtypeStruct((n_rows * ROW_TILE, LANES), F32),
        scratch_shapes=[pltpu.SemaphoreType.DMA(())],
        compiler_params=pltpu.CompilerParams(dimension_semantics=("arbitrary",),
                                             has_side_effects=True),
    )(dest, x1_2d)


HALF = LANES // 2
W_CAST_ROWS = 64


def _expert_kernel(be_ref, nv_ref, xin_ref, wgu_ref, bgu_ref, wdn_ref, bdn_ref, y_ref,
                   wgu_s, wdn_s, perm_s):
    b = pl.program_id(0)
    blk = y_ref.shape[0] // ROW_TILE
    nvalid = nv_ref[b]
    new_expert = jnp.logical_or(b == 0, be_ref[b] != be_ref[jnp.maximum(b - 1, 0)])

    @pl.when(jnp.logical_and(new_expert, nvalid > 0))
    def _():
        def cast(r, _):
            r0 = pl.multiple_of(r * W_CAST_ROWS, W_CAST_ROWS)
            wgu_s[pl.ds(r0, W_CAST_ROWS), :] = wgu_ref[0, pl.ds(r0, W_CAST_ROWS), :].astype(BF16)
            return 0

        lax.fori_loop(0, D_MODEL // W_CAST_ROWS, cast, 0)
        for c in range(D_MODEL // LANES):
            cols = slice(c * LANES, (c + 1) * LANES)
            for g in range(D_MODEL // LANES):
                lo = g * LANES
                perm_s[c, pl.ds(lo, HALF, stride=2), :] = wdn_ref[0, lo:lo + HALF, cols]
                perm_s[c, pl.ds(lo + 1, HALF, stride=2), :] = wdn_ref[0, lo + HALF:lo + LANES, cols]
            wdn_s[:, cols] = perm_s[c].astype(BF16)

    @pl.when(nvalid > 0)
    def _():
        x = _load_row_tiles(xin_ref, blk)
        x = jnp.where(_iota((blk, D_MODEL), 0) < nvalid, x, 0.0).astype(BF16)
        even = _iota((blk, LANES), 1) % 2 == 0
        acts = []
        for g in range(D_MODEL // LANES):
            lo = g * 2 * LANES
            h = _dot(x, wgu_s[:, lo:lo + 2 * LANES]) + bgu_ref[0, :, lo:lo + 2 * LANES]
            h_a = h[:, :LANES]
            h_b = h[:, LANES:]
            gate = jnp.where(even, h_a, pltpu.roll(h_b, 1, 1))
            up = jnp.where(even, pltpu.roll(h_a, LANES - 1, 1), h_b)
            gate = jnp.minimum(gate, SWIGLU_LIMIT)
            up = jnp.clip(up, -SWIGLU_LIMIT, SWIGLU_LIMIT)
            acts.append(((up + 1.0) * (gate * _sigmoid(SWIGLU_ALPHA * gate))).astype(BF16))
        act = jnp.concatenate(acts, axis=1)
        _store_row_tiles(y_ref, _dot(act, wdn_s[...]) + bdn_ref[0])

    @pl.when(nvalid <= 0)
    def _():
        y_ref[...] = jnp.zeros_like(y_ref)


def _experts(block_expert, block_valid, xin2d, w_gu, b_gu, w_dn, b_dn, layer):
    blk = EXPERT_BLOCK
    nb = block_expert.shape[0]
    x_spec = pl.BlockSpec((blk * ROW_TILE, LANES), lambda b, be, nv: (b, 0))
    which = lambda b, be, nv: (layer * N_EXPERTS + be[b], 0, 0)
    return pl.pallas_call(
        _expert_kernel,
        grid_spec=pltpu.PrefetchScalarGridSpec(
            num_scalar_prefetch=2,
            grid=(nb,),
            in_specs=[x_spec,
                      pl.BlockSpec((1, D_MODEL, 2 * D_MODEL), which),
                      pl.BlockSpec((1, 1, 2 * D_MODEL), which),
                      pl.BlockSpec((1, D_MODEL, D_MODEL), which),
                      pl.BlockSpec((1, 1, D_MODEL), which)],
            out_specs=x_spec,
            scratch_shapes=[pltpu.VMEM((D_MODEL, 2 * D_MODEL), BF16),
                            pltpu.VMEM((D_MODEL, D_MODEL), BF16),
                            pltpu.VMEM((D_MODEL // LANES, D_MODEL, LANES), F32)]),
        out_shape=jax.ShapeDtypeStruct((nb * blk * ROW_TILE, LANES), F32),
        compiler_params=pltpu.CompilerParams(dimension_semantics=("arbitrary",),
                                             vmem_limit_bytes=56 * 1024 * 1024),
    )(block_expert, block_valid, xin2d, w_gu, b_gu, w_dn, b_dn)


def _combine_kernel(dest_ref, next_ref, yb_ref, gw_ref, x1_ref, lg_ref, lb_ref, out_ref,
                    buf_ref, sem):
    i = pl.program_id(0)
    tc = out_ref.shape[0]
    slot = i % 2

    def gather(rows_ref, into):
        def body(g, _):
            for u in range(DMA_UNROLL):
                t = g * DMA_UNROLL + u
                for k in range(TOP_K):
                    pltpu.make_async_copy(_token_rows(yb_ref, rows_ref[k, t]),
                                          _token_rows(buf_ref.at[into, k], t),
                                          sem.at[into]).start(priority=k % 2)
            return 0

        lax.fori_loop(0, tc // DMA_UNROLL, body, 0)

    @pl.when(i == 0)
    def _():
        gather(dest_ref, 0)

    @pl.when(i + 1 < pl.num_programs(0))
    def _():
        gather(next_ref, 1 - slot)

    for k in range(TOP_K):
        pltpu.make_async_copy(yb_ref.at[pl.ds(0, tc * ROW_TILE), :], buf_ref.at[slot, k],
                              sem.at[slot]).wait()

    gw = gw_ref[...]
    f = jnp.zeros((tc, D_MODEL), F32)
    for k in range(TOP_K):
        f += gw[:, k:k + 1] * _load_row_tiles(buf_ref.at[slot, k], tc)
    z = ALPHA * _load_row_tiles(x1_ref, tc) + f
    out_ref[...] = _layer_norm(z, lg_ref[...], lb_ref[...])


def _combine(dest, yb_rows, gw, x1_2d, lg, lb, tc=256):
    s = gw.shape[0]
    n = s // tc
    row = pl.BlockSpec((1, D_MODEL), lambda i: (0, 0))
    return pl.pallas_call(
        _combine_kernel,
        grid=(n,),
        in_specs=[pl.BlockSpec((ROW_TILE, tc), lambda i: (0, i), memory_space=pltpu.SMEM),
                  pl.BlockSpec((ROW_TILE, tc), lambda i: (0, jnp.minimum(i + 1, n - 1)),
                               memory_space=pltpu.SMEM),
                  pl.BlockSpec(memory_space=pl.ANY),
                  pl.BlockSpec((tc, LANES), lambda i: (i, 0)),
                  pl.BlockSpec((tc * ROW_TILE, LANES), lambda i: (i, 0)),
                  row, row],
        out_specs=pl.BlockSpec((tc, D_MODEL), lambda i: (i, 0)),
        out_shape=jax.ShapeDtypeStruct((s, D_MODEL), F32),
        scratch_shapes=[pltpu.VMEM((2, TOP_K, tc * ROW_TILE, LANES), F32),
                        pltpu.SemaphoreType.DMA((2,))],
        compiler_params=_params(("arbitrary",)),
    )(dest, dest, yb_rows, gw, x1_2d, lg, lb)


W_IN_SIZES = (256, 256, 256, 4, 128, 128, 256, 16, 256, 256, 256, 256, 3 * D_MODEL)
W_IN_WIDTH = sum(W_IN_SIZES)


def _w_in_kernel(w_ref, main_ref, tail_ref):
    w = w_ref[0]
    offs = [0]
    for n in W_IN_SIZES:
        offs.append(offs[-1] + n)
    fq, fk, fv, ff, gq, gk, gv, ga, gr, mq, mk, mv, gates = (
        w[:, offs[i]:offs[i + 1]] for i in range(len(W_IN_SIZES)))
    main_ref[...] = jnp.concatenate(
        [gates, fq, fk, fv, mq, mk, mv, gq, gk, gv, gr], axis=1).astype(BF16)
    pad = jnp.zeros((w.shape[0], LANES - 4 - GLA_RANK), F32)
    tail_ref[...] = jnp.concatenate([ff, ga, pad], axis=1).astype(BF16)


def _permute_w_in(w_in, layer, tr=128):
    return pl.pallas_call(
        _w_in_kernel,
        grid=(D_MODEL // tr,),
        in_specs=[pl.BlockSpec((1, tr, W_IN_WIDTH), lambda i: (layer, i, 0))],
        out_specs=[pl.BlockSpec((tr, P_WIDTH), lambda i: (i, 0)),
                   pl.BlockSpec((tr, LANES), lambda i: (i, 0))],
        out_shape=[jax.ShapeDtypeStruct((D_MODEL, P_WIDTH), BF16),
                   jax.ShapeDtypeStruct((D_MODEL, LANES), BF16)],
        compiler_params=_params(("parallel",)),
    )(w_in)


def _pad_row(v, n=LANES, fill=0.0):
    return jnp.pad(v.astype(F32), (0, n - v.shape[0]), constant_values=fill)[None, :]


def _mixer_layer(x, w_in, layer, fox_fb, gla_a_up, gla_a_b, gla_norm_g, w_branch, w_out, ln_g, ln_b,
                 w_router, b_router):
    w_main, w_tail = _permute_w_in(w_in, layer)
    p, tail = _inproj(x, w_main, w_tail)

    s = x.shape[0]
    tq, tk = _flash_tiles(s)
    fox_qt, fox_k, fox_vt, fox_stats = _fox_prep(p, tail, _pad_row(fox_fb))
    fox_o = _flash(_fox_first_tile(fox_stats, s, tq, tk), fox_qt, fox_k, fox_vt, tq, tk)
    moba_o = _flash(jnp.zeros((N_HEADS * (s // tq),), jnp.int32), *_moba_prep(p), tq, tk)
    aup_pad = jnp.zeros((LANES, GLA_KW), F32).at[TAIL_GA:TAIL_GA + GLA_RANK].set(gla_a_up)
    y_gla = _gla(p, tail, aup_pad, gla_a_b[None, :], gla_norm_g[None, :])

    wr = jnp.pad(w_router, ((0, 0), (0, LANES - N_EXPERTS)))
    wr_hi = wr.astype(BF16)
    wr_lo = jnp.concatenate([wr_hi, (wr - wr_hi.astype(F32)).astype(BF16)], axis=1)
    return _merge(fox_o, y_gla, moba_o, p, x, w_branch.astype(BF16), w_out.astype(BF16),
                  ln_g[None, :], ln_b[None, :], wr_hi, wr_lo, _pad_row(b_router))


def _moe_layer(x1_2d, sel, idxf, gw, w_gu, b_gu, w_dn, b_dn, ln_g, ln_b, layer):
    s = sel.shape[0]
    blk = EXPERT_BLOCK
    dest, tables = _rank(sel, idxf)
    n_blocks = tables.shape[0]
    block_expert = tables[:, 0]
    block_valid = tables[:, 1]

    xin = _dispatch(dest, x1_2d, n_blocks * blk)
    yb = _experts(block_expert, block_valid, xin, w_gu, b_gu, w_dn, b_dn, layer)
    return _combine(dest, yb, gw, x1_2d, ln_g[None, :], ln_b[None, :])


def kernel(x, w_in, fox_fb, gla_a_up, gla_a_b, gla_norm_g, w_branch, w_out, ln1_g, ln1_b,
           w_router, b_router, w_gu, b_gu, w_dn, b_dn, ln2_g, ln2_b):
    b, s, d = x.shape
    assert b == 1 and d == D_MODEL
    h = x.reshape(s, d)
    n_le = DEPTH * N_EXPERTS
    w_gu = w_gu.reshape(n_le, D_MODEL, 2 * D_MODEL)
    b_gu = b_gu.reshape(n_le, 1, 2 * D_MODEL)
    w_dn = w_dn.reshape(n_le, D_MODEL, D_MODEL)
    b_dn = b_dn.reshape(n_le, 1, D_MODEL)
    for l in range(DEPTH):
        x1_2d, sel, idxf, gw = _mixer_layer(
            h, w_in, l, fox_fb[l], gla_a_up[l], gla_a_b[l], gla_norm_g[l], w_branch[l], w_out[l],
            ln1_g[l], ln1_b[l], w_router[l], b_router[l])
        h = _moe_layer(x1_2d, sel, idxf, gw, w_gu, b_gu, w_dn, b_dn, ln2_g[l], ln2_b[l], l)
    return h.reshape(b, s, d)
```

```python
import functools

import jax
import jax.numpy as jnp
from jax import lax
from jax.experimental import pallas as pl
from jax.experimental.pallas import tpu as pltpu

F32 = jnp.float32
BF16 = jnp.bfloat16

D_MODEL = 1024
DEPTH = 4
N_HEADS = 4
HEAD_DIM = 64
GLA_DK = 32
GLA_KW = N_HEADS * GLA_DK
GLA_RANK = 16
GLA_TAU = 16.0
MOBA_BLOCK = 256
MOBA_TOPK = 3
BRANCH_WIDTH = 256
N_EXPERTS = 32
TOP_K = 4
SWIGLU_LIMIT = 7.0
SWIGLU_ALPHA = 1.702
ALPHA = (2 * DEPTH) ** 0.25
LN_EPS = 1e-5
RMS_EPS = 1e-6
LOG2E = 1.4426950408889634

LANES = 128
ROW_TILE = 8
NEG = -1e30
UNSELECTED = -32768.0
VMEM_LIMIT = 48 * 1024 * 1024

P_GATES = 0
P_FOX = 3 * D_MODEL
P_MOBA = P_FOX + 3 * BRANCH_WIDTH
P_GLA = P_MOBA + 3 * BRANCH_WIDTH
P_WIDTH = P_GLA + 2 * GLA_KW + 2 * BRANCH_WIDTH
TAIL_FF = 0
TAIL_GA = 4

EXPERT_BLOCK = 512


def _params(sem):
    return pltpu.CompilerParams(dimension_semantics=sem, vmem_limit_bytes=VMEM_LIMIT)


def _split3(x):
    hi = x.astype(BF16)
    r1 = x - hi.astype(F32)
    mid = r1.astype(BF16)
    lo = (r1 - mid.astype(F32)).astype(BF16)
    return hi, mid, lo


def _split2(x):
    hi = x.astype(BF16)
    lo = (x - hi.astype(F32)).astype(BF16)
    return hi, lo


def _dot(a, b):
    return jnp.dot(a, b, preferred_element_type=F32)


def _dot_nt(a, b):
    return lax.dot_general(a, b, (((1,), (1,)), ((), ())), preferred_element_type=F32)


def _dot_tn(a, b):
    return lax.dot_general(a, b, (((0,), (0,)), ((), ())), preferred_element_type=F32)


def _log_sigmoid(t):
    return jnp.minimum(t, 0.0) - jnp.log1p(jnp.exp(-jnp.abs(t)))


def _sigmoid(t):
    return 0.5 * jnp.tanh(0.5 * t) + 0.5


def _iota(shape, axis):
    return lax.broadcasted_iota(jnp.int32, shape, axis)


def _transpose_bf16(x):
    return x.astype(F32).T.astype(BF16)


BF16_ROWS = 16
V_ROWS = HEAD_DIM + BF16_ROWS


def _values_t(v):
    ones_row = jnp.where(_iota((BF16_ROWS, v.shape[0]), 0) == 0, 1.0, 0.0).astype(BF16)
    return jnp.concatenate([_transpose_bf16(v), ones_row], axis=0)


def _aug_specs(s, t):
    aug = jax.ShapeDtypeStruct((N_HEADS, s, LANES), BF16)
    aug_spec = pl.BlockSpec((N_HEADS, t, LANES), lambda i: (0, i, 0))
    aug_t = jax.ShapeDtypeStruct((N_HEADS, LANES, s), BF16)
    aug_t_spec = pl.BlockSpec((N_HEADS, LANES, t), lambda i: (0, 0, i))
    aug_v = jax.ShapeDtypeStruct((N_HEADS, V_ROWS, s), BF16)
    aug_v_spec = pl.BlockSpec((N_HEADS, V_ROWS, t), lambda i: (0, 0, i))
    return aug, aug_spec, aug_t, aug_t_spec, aug_v, aug_v_spec


def _inproj_kernel(x_ref, w_ref, wt_ref, sc_ref, p_ref, t_ref, xb_ref):
    @pl.when(pl.program_id(1) == 0)
    def _():
        xb_ref[...] = x_ref[...].astype(BF16)
        t_ref[...] = _dot(xb_ref[...], wt_ref[...])

    p_ref[...] = (_dot(xb_ref[...], w_ref[...]) * sc_ref[...]).astype(BF16)


def _inproj(x, w, wt, tm=1024, tn=1792):
    s = x.shape[0]
    col = jnp.arange(P_WIDTH)
    is_q = ((col >= P_FOX) & (col < P_FOX + BRANCH_WIDTH)) | ((col >= P_MOBA) & (col < P_MOBA + BRANCH_WIDTH))
    scale = jnp.where(is_q, HEAD_DIM ** -0.5 * LOG2E, 1.0).astype(F32)[None, :]
    return pl.pallas_call(
        _inproj_kernel,
        grid=(s // tm, P_WIDTH // tn),
        in_specs=[pl.BlockSpec((tm, D_MODEL), lambda i, j: (i, 0)),
                  pl.BlockSpec((D_MODEL, tn), lambda i, j: (0, j)),
                  pl.BlockSpec((D_MODEL, LANES), lambda i, j: (0, 0)),
                  pl.BlockSpec((1, tn), lambda i, j: (0, j))],
        out_specs=[pl.BlockSpec((tm, tn), lambda i, j: (i, j)),
                   pl.BlockSpec((tm, LANES), lambda i, j: (i, 0))],
        out_shape=[jax.ShapeDtypeStruct((s, P_WIDTH), BF16),
                   jax.ShapeDtypeStruct((s, LANES), F32)],
        scratch_shapes=[pltpu.VMEM((tm, D_MODEL), BF16)],
        compiler_params=_params(("parallel", "arbitrary")),
    )(x, w, wt, scale)


def _fox_prep_kernel(q_ref, k_ref, v_ref, t_ref, fb_ref, qa_ref, ka_ref, va_ref, st_ref, carry_ref):
    t = q_ref.shape[0]

    @pl.when(pl.program_id(0) == 0)
    def _():
        carry_ref[...] = jnp.zeros_like(carry_ref)

    ls = _log_sigmoid(t_ref[...] + fb_ref[...])
    tri = (_iota((t, t), 0) >= _iota((t, t), 1)).astype(BF16)
    hi, mid, lo = _split3(ls)
    c = _dot(tri, hi) + _dot(tri, mid) + _dot(tri, lo) + carry_ref[...]
    carry_ref[...] = c[t - 1:t, :]

    c2 = LOG2E * c

    lane = _iota((t, HEAD_DIM), 1)
    q_aug = jnp.where(_iota((HEAD_DIM, t), 0) < 3, 1.0, 0.0).astype(BF16)
    q = q_ref[...]
    k = k_ref[...]
    v = v_ref[...]
    head_of = (_iota((BRANCH_WIDTH, LANES), 0) // HEAD_DIM == _iota((BRANCH_WIDTH, LANES), 1))
    head_of = head_of.astype(BF16)

    def max_norm(x):
        xf = x.astype(F32)
        hi, lo = _split2(xf * xf)
        norm2 = jnp.max(_dot(hi, head_of) + _dot(lo, head_of), axis=0, keepdims=True)
        return jnp.sqrt(norm2 * (1.0 + 2.0 ** -8))

    srow = _iota((ROW_TILE, LANES), 0)
    stat = jnp.where(srow == 0, max_norm(q), jnp.where(srow == 1, max_norm(k), jnp.where(
        srow == 2, c2[0:1, :], jnp.where(srow == 3, c2[t - 1:t, :], 0.0))))
    for h in range(N_HEADS):
        sl = slice(h * HEAD_DIM, (h + 1) * HEAD_DIM)
        nhi, nmid, nlo = (piece.astype(F32) for piece in _split3(-c2[:, h:h + 1]))
        k_aug = jnp.where(lane == 0, nhi, jnp.where(lane == 1, nmid, jnp.where(lane == 2, nlo, 0.0)))
        qa_ref[h] = jnp.concatenate([_transpose_bf16(q[:, sl]), q_aug], axis=0)
        ka_ref[h] = jnp.concatenate([k[:, sl], k_aug.astype(BF16)], axis=1)
        va_ref[h] = _values_t(v[:, sl])
    st_ref[0] = stat


def _fox_prep(p, tail, fb_row, t=512):
    s = p.shape[0]
    cb = P_FOX // BRANCH_WIDTH
    aug, aug_spec, aug_t, aug_t_spec, aug_v, aug_v_spec = _aug_specs(s, t)
    return pl.pallas_call(
        _fox_prep_kernel,
        grid=(s // t,),
        in_specs=[pl.BlockSpec((t, BRANCH_WIDTH), lambda i: (i, cb)),
                  pl.BlockSpec((t, BRANCH_WIDTH), lambda i: (i, cb + 1)),
                  pl.BlockSpec((t, BRANCH_WIDTH), lambda i: (i, cb + 2)),
                  pl.BlockSpec((t, LANES), lambda i: (i, 0)),
                  pl.BlockSpec((1, LANES), lambda i: (0, 0))],
        out_specs=[aug_t_spec, aug_spec, aug_v_spec,
                   pl.BlockSpec((1, ROW_TILE, LANES), lambda i: (i, 0, 0))],
        out_shape=[aug_t, aug, aug_v, jax.ShapeDtypeStruct((s // t, ROW_TILE, LANES), F32)],
        scratch_shapes=[pltpu.VMEM((1, LANES), F32)],
        compiler_params=_params(("arbitrary",)),
    )(p, p, p, tail, fb_row)


SKIP_LOG2 = 48.0


def _fox_first_tile(stats, s, tq, tk):
    tp = s // stats.shape[0]
    qn, kn, c_first, c_last = (stats[:, r, :N_HEADS] for r in range(4))
    nq, nk = s // tq, s // tk
    qn = jnp.max(qn.reshape(nq, tq // tp, N_HEADS), axis=1)
    c_q = c_first.reshape(nq, tq // tp, N_HEADS)[:, 0, :]
    c_k = c_last.reshape(nk, tk // tp, N_HEADS)[:, -1, :]
    bound = (2.0 * qn * jnp.max(kn, axis=0))[:, None, :] + c_q[:, None, :] - c_k[None, :, :]
    j = jnp.arange(nk, dtype=jnp.int32)[None, :, None]
    first = jnp.min(jnp.where(bound >= -SKIP_LOG2, j, nk), axis=1)
    return first.T.reshape(-1).astype(jnp.int32)


def _moba_prep_kernel(q_ref, k_ref, v_ref, qa_ref, ka_ref, va_ref, kmean_ref):
    i = pl.program_id(0)
    t = q_ref.shape[0]
    nbl = kmean_ref.shape[0]

    @pl.when(i == 0)
    def _():
        kmean_ref[...] = jnp.zeros_like(kmean_ref)

    q = q_ref[...]
    k = k_ref[...]
    v = v_ref[...]
    km = kmean_ref[...]
    k_aug = jnp.where(_iota((t, HEAD_DIM), 1) == i, 1.0, 0.0).astype(BF16)
    blk = _iota((HEAD_DIM, t), 0)
    for h in range(N_HEADS):
        sl = slice(h * HEAD_DIM, (h + 1) * HEAD_DIM)
        qt = _transpose_bf16(q[:, sl])
        kh_hi, kh_lo = _split2(km[:HEAD_DIM, sl])
        gate = _dot(kh_hi, qt) + _dot(kh_lo, qt)
        g = jnp.where(blk < i, gate, NEG)
        bias = jnp.where(blk == i, 0.0, UNSELECTED)
        for _ in range(MOBA_TOPK):
            mx = jnp.max(g, axis=0, keepdims=True)
            idx = jnp.min(jnp.where(g == mx, blk, HEAD_DIM), axis=0, keepdims=True)
            pick = jnp.logical_and(blk == idx, mx > 0.5 * NEG)
            bias = jnp.where(pick, 0.0, bias)
            g = jnp.where(blk == idx, NEG, g)
        qa_ref[h] = jnp.concatenate([qt, bias.astype(BF16)], axis=0)
        ka_ref[h] = jnp.concatenate([k[:, sl], k_aug], axis=1)
        va_ref[h] = _values_t(v[:, sl])
    kmean = jnp.mean(k.astype(F32), axis=0, keepdims=True)
    kmean_ref[...] = jnp.where(_iota((nbl, BRANCH_WIDTH), 0) == i, kmean, km)


def _moba_prep(p):
    s = p.shape[0]
    t = MOBA_BLOCK
    assert s % t == 0 and s // t <= HEAD_DIM, "block one-hot must fit the 64 augmentation lanes"
    cb = P_MOBA // BRANCH_WIDTH
    aug, aug_spec, aug_t, aug_t_spec, aug_v, aug_v_spec = _aug_specs(s, t)
    return pl.pallas_call(
        _moba_prep_kernel,
        grid=(s // t,),
        in_specs=[pl.BlockSpec((t, BRANCH_WIDTH), lambda i: (i, cb)),
                  pl.BlockSpec((t, BRANCH_WIDTH), lambda i: (i, cb + 1)),
                  pl.BlockSpec((t, BRANCH_WIDTH), lambda i: (i, cb + 2))],
        out_specs=[aug_t_spec, aug_spec, aug_v_spec],
        out_shape=[aug_t, aug, aug_v],
        scratch_shapes=[pltpu.VMEM((HEAD_DIM, BRANCH_WIDTH), F32)],
        compiler_params=_params(("arbitrary",)),
    )(p, p, p)


def _flash_kernel(first_ref, qt_ref, k_ref, vt_ref, o_ref, s0_ref, s1_ref, s2_ref, *, tq, tk):
    h = pl.program_id(0)
    i = pl.program_id(1)
    qt = qt_ref[0]
    n_diag = tq // tk
    n_full = i * n_diag

    def scores(j, first_query=0):
        return _dot(k_ref[0, pl.ds(pl.multiple_of(j * tk, tk), tk), :], qt[:, first_query:])

    def absorb(j, s, m, acc):
        m_new = jnp.maximum(m, jnp.max(s, axis=0, keepdims=True))
        p = jnp.exp2(s - m_new).astype(BF16)
        vt = vt_ref[0, :, pl.ds(pl.multiple_of(j * tk, tk), tk)]
        return m_new, jnp.exp2(m - m_new) * acc + _dot(vt, p)

    j0 = jnp.minimum(first_ref[h * pl.num_programs(1) + i], n_full)
    m = jnp.full((1, tq), NEG, F32)
    acc = jnp.zeros((V_ROWS, tq), F32)

    extra = (n_full - j0) % 3

    def plain(j, carry):
        return absorb(j, scores(j), *carry)

    m, acc = lax.fori_loop(j0, j0 + extra, plain, (m, acc))
    j0 = j0 + extra

    s0_ref[...] = scores(j0)

    def triple(g, carry):
        j = j0 + 3 * g
        s1_ref[...] = scores(j + 1)
        carry = absorb(j, s0_ref[...], *carry)
        s2_ref[...] = scores(j + 2)
        carry = absorb(j + 1, s1_ref[...], *carry)
        s0_ref[...] = scores(j + 3)
        return absorb(j + 2, s2_ref[...], *carry)

    m, acc = lax.fori_loop(0, (n_full - j0) // 3, triple, (m, acc))

    causal = _iota((tk, tk), 0) <= _iota((tk, tk), 1)
    s = s0_ref[...]
    for d in range(n_diag):
        lo = d * tk
        s_next = scores(n_full + d + 1, lo + tk) if d + 1 < n_diag else None
        tri = jnp.where(causal, s[:, :tk], NEG)
        s = tri if lo + tk == tq else jnp.concatenate([tri, s[:, tk:]], axis=1)
        m_d, acc_d = absorb(n_full + d, s, m[:, lo:], acc[:, lo:])
        m = m_d if lo == 0 else jnp.concatenate([m[:, :lo], m_d], axis=1)
        acc = acc_d if lo == 0 else jnp.concatenate([acc[:, :lo], acc_d], axis=1)
        s = s_next
    o_ref[0] = (acc[:HEAD_DIM, :] / acc[HEAD_DIM:HEAD_DIM + 1, :]).T.astype(o_ref.dtype)


def _flash(first_tile, qt, ka, vt, tq, tk):
    nh, s, _ = ka.shape
    return pl.pallas_call(
        functools.partial(_flash_kernel, tq=tq, tk=tk),
        grid_spec=pltpu.PrefetchScalarGridSpec(
            num_scalar_prefetch=1,
            grid=(nh, s // tq),
            in_specs=[pl.BlockSpec((1, LANES, tq), lambda h, i, f: (h, 0, i)),
                      pl.BlockSpec((1, s, LANES), lambda h, i, f: (h, 0, 0)),
                      pl.BlockSpec((1, V_ROWS, s), lambda h, i, f: (h, 0, 0))],
            out_specs=pl.BlockSpec((1, tq, HEAD_DIM), lambda h, i, f: (h, i, 0)),
            scratch_shapes=[pltpu.VMEM((tk, tq), F32)] * 3),
        out_shape=jax.ShapeDtypeStruct((nh, s, HEAD_DIM), BF16),
        compiler_params=_params(("parallel", "parallel")),
    )(first_tile, qt, ka, vt)


def _flash_tiles(s):
    tq = min(1024, s)
    return tq, min(512, tq)


GLA_SUB = 16
GLA_UNROLL = 4


def _gla_kernel(q_ref, k_ref, v_ref, r_ref, t_ref, aup_ref, ab_ref, g_ref, y_ref,
                st_ref, b_ref, o_ref):
    t = q_ref.shape[0]
    c = GLA_SUB

    @pl.when(pl.program_id(0) == 0)
    def _():
        st_ref[...] = jnp.zeros_like(st_ref)

    t_hi, t_lo = _split2(t_ref[...])
    a_hi, a_lo = _split2(aup_ref[...])
    z = _dot(t_hi, a_hi) + _dot(t_lo, a_hi) + _dot(t_hi, a_lo) + ab_ref[...]
    log_a = _log_sigmoid(z) * (1.0 / GLA_TAU)
    row = _iota((t, t), 0)
    col = _iota((t, t), 1)
    tri = jnp.logical_and(row // c == col // c, row >= col).astype(BF16)
    hi, mid, lo = _split3(log_a)
    b_ref[...] = _dot(tri, hi) + _dot(tri, mid) + _dot(tri, lo)

    expand = (_iota((GLA_KW, BRANCH_WIDTH), 0) // GLA_DK
              == _iota((GLA_KW, BRANCH_WIDTH), 1) // HEAD_DIM).astype(BF16)
    st_mask = (_iota((BRANCH_WIDTH, GLA_KW), 0) // HEAD_DIM
               == _iota((BRANCH_WIDTH, GLA_KW), 1) // GLA_DK).astype(F32)
    srow = _iota((c, GLA_KW), 0)
    scale = GLA_DK ** -0.5

    def step(r0, st):
        qs = q_ref[pl.ds(r0, c), :].astype(F32) * scale
        ks = k_ref[pl.ds(r0, c), :].astype(F32)
        vb = v_ref[pl.ds(r0, c), :]
        vf = vb.astype(F32)
        bs = b_ref[pl.ds(r0, c), :]
        b_last = bs[c - 1:c, :]

        o_inter = _dot_nt((qs * jnp.exp(bs)).astype(BF16), st.astype(BF16))

        pieces = []
        for ss in range(c):
            e = jnp.exp(jnp.where(srow >= ss, bs - bs[ss:ss + 1, :], NEG))
            pieces.append(e * qs * ks[ss:ss + 1, :])
        pm = jnp.concatenate(pieces, axis=0).astype(BF16)
        a = _dot(pm, expand)
        o_intra = a[:c, :] * vf[0:1, :]
        for ss in range(1, c):
            o_intra += a[ss * c:(ss + 1) * c, :] * vf[ss:ss + 1, :]
        o_ref[pl.ds(r0, c), :] = o_inter + o_intra

        ke = (ks * jnp.exp(b_last - bs)).astype(BF16)
        return st * jnp.exp(b_last) + _dot_tn(vb, ke) * st_mask

    def steps(g, _):
        st = st_ref[...]
        for u in range(GLA_UNROLL):
            st = step(pl.multiple_of((g * GLA_UNROLL + u) * c, c), st)
        st_ref[...] = st
        return 0

    lax.fori_loop(0, t // (c * GLA_UNROLL), steps, 0)

    o = o_ref[...]
    ones_bd = (_iota((BRANCH_WIDTH, BRANCH_WIDTH), 0) // HEAD_DIM
               == _iota((BRANCH_WIDTH, BRANCH_WIDTH), 1) // HEAD_DIM).astype(BF16)
    sq_hi, sq_mid, sq_lo = _split3(o * o)
    ms = (_dot(sq_hi, ones_bd) + _dot(sq_mid, ones_bd) + _dot(sq_lo, ones_bd)) * (1.0 / HEAD_DIM)
    gr = r_ref[...].astype(F32)
    y = o * lax.rsqrt(ms + RMS_EPS) * g_ref[...] * (gr * _sigmoid(gr))
    y_ref[...] = y.astype(y_ref.dtype)


def _gla(p, tail, aup_pad, ab_row, g_row, t=512):
    s = p.shape[0]
    cq = P_GLA // GLA_KW
    cv = (P_GLA + 2 * GLA_KW) // BRANCH_WIDTH
    return pl.pallas_call(
        _gla_kernel,
        grid=(s // t,),
        in_specs=[pl.BlockSpec((t, GLA_KW), lambda i: (i, cq)),
                  pl.BlockSpec((t, GLA_KW), lambda i: (i, cq + 1)),
                  pl.BlockSpec((t, BRANCH_WIDTH), lambda i: (i, cv)),
                  pl.BlockSpec((t, BRANCH_WIDTH), lambda i: (i, cv + 1)),
                  pl.BlockSpec((t, LANES), lambda i: (i, 0)),
                  pl.BlockSpec((LANES, GLA_KW), lambda i: (0, 0)),
                  pl.BlockSpec((1, GLA_KW), lambda i: (0, 0)),
                  pl.BlockSpec((1, BRANCH_WIDTH), lambda i: (0, 0))],
        out_specs=pl.BlockSpec((t, BRANCH_WIDTH), lambda i: (i, 0)),
        out_shape=jax.ShapeDtypeStruct((s, BRANCH_WIDTH), BF16),
        scratch_shapes=[pltpu.VMEM((BRANCH_WIDTH, GLA_KW), F32),
                        pltpu.VMEM((t, GLA_KW), F32),
                        pltpu.VMEM((t, BRANCH_WIDTH), F32)],
        compiler_params=_params(("arbitrary",)),
    )(p, p, p, p, tail, aup_pad, ab_row, g_row)


def _layer_norm(z, g, b):
    mu = jnp.mean(z, axis=1, keepdims=True)
    zc = z - mu
    var = jnp.mean(zc * zc, axis=1, keepdims=True)
    return zc * lax.rsqrt(var + LN_EPS) * g + b


def _store_row_tiles(ref, val):
    n = val.shape[0]
    for a in range(ROW_TILE):
        ref[pl.ds(a, n, stride=ROW_TILE), :] = val[:, a * LANES:(a + 1) * LANES]


def _load_row_tiles(ref, n):
    return jnp.concatenate(
        [ref[pl.ds(a, n, stride=ROW_TILE), :] for a in range(ROW_TILE)], axis=1)


def _merge_kernel(fox_ref, gla_ref, moba_ref, g0_ref, g1_ref, g2_ref, x_ref, wb_ref, wo_ref,
                  lg_ref, lb_ref, wrh_ref, wrl_ref, br_ref,
                  x1_ref, sel_ref, idx_ref, gw_ref):
    tm = x_ref.shape[0]

    def heads(ref):
        return jnp.concatenate([ref[h] for h in range(N_HEADS)], axis=1)

    merged = _sigmoid(g0_ref[...]) * _dot(heads(fox_ref), wb_ref[0]).astype(BF16)
    merged += _sigmoid(g1_ref[...]) * _dot(gla_ref[...], wb_ref[1]).astype(BF16)
    merged += _sigmoid(g2_ref[...]) * _dot(heads(moba_ref), wb_ref[2]).astype(BF16)
    z = ALPHA * x_ref[...] + _dot(merged, wo_ref[...])
    x1 = _layer_norm(z, lg_ref[...], lb_ref[...])
    _store_row_tiles(x1_ref, x1)

    x_hi, x_lo = _split2(x1)
    both = _dot(x_hi, wrl_ref[...])
    logits = both[:, :LANES] + both[:, LANES:] + _dot(x_lo, wrh_ref[...]) + br_ref[...]
    lane = _iota((tm, LANES), 1)
    lg = jnp.where(lane < N_EXPERTS, logits, NEG)
    sel = jnp.zeros((tm, LANES), F32)
    idxf = jnp.zeros((tm, LANES), F32)
    ew = jnp.zeros((tm, LANES), F32)
    top = None
    for r in range(TOP_K):
        mx = jnp.max(lg, axis=1, keepdims=True)
        idx = jnp.min(jnp.where(lg == mx, lane, LANES), axis=1, keepdims=True)
        hit = lane == idx
        top = mx if top is None else top
        sel = jnp.where(hit, 1.0, sel)
        idxf = jnp.where(lane == r, idx.astype(F32), idxf)
        ew = jnp.where(lane == r, jnp.exp(mx - top), ew)
        lg = jnp.where(hit, NEG, lg)
    sel_ref[...] = sel
    idx_ref[...] = idxf
    gw_ref[...] = ew / jnp.sum(ew, axis=1, keepdims=True)


def _merge(fox_o, y_gla, moba_o, p, x, wb, wo, lg, lb, wrh, wrl, br, tm=256):
    s = x.shape[0]
    head_spec = pl.BlockSpec((N_HEADS, tm, HEAD_DIM), lambda i: (0, i, 0))
    row = lambda n: pl.BlockSpec((1, n), lambda i: (0, 0))
    small = jax.ShapeDtypeStruct((s, LANES), F32)
    small_spec = pl.BlockSpec((tm, LANES), lambda i: (i, 0))
    return pl.pallas_call(
        _merge_kernel,
        grid=(s // tm,),
        in_specs=[head_spec,
                  pl.BlockSpec((tm, BRANCH_WIDTH), lambda i: (i, 0)),
                  head_spec,
                  pl.BlockSpec((tm, D_MODEL), lambda i: (i, P_GATES // D_MODEL)),
                  pl.BlockSpec((tm, D_MODEL), lambda i: (i, P_GATES // D_MODEL + 1)),
                  pl.BlockSpec((tm, D_MODEL), lambda i: (i, P_GATES // D_MODEL + 2)),
                  pl.BlockSpec((tm, D_MODEL), lambda i: (i, 0)),
                  pl.BlockSpec((3, BRANCH_WIDTH, D_MODEL), lambda i: (0, 0, 0)),
                  pl.BlockSpec((D_MODEL, D_MODEL), lambda i: (0, 0)),
                  row(D_MODEL), row(D_MODEL),
                  pl.BlockSpec((D_MODEL, LANES), lambda i: (0, 0)),
                  pl.BlockSpec((D_MODEL, 2 * LANES), lambda i: (0, 0)),
                  row(LANES)],
        out_specs=[pl.BlockSpec((tm * ROW_TILE, LANES), lambda i: (i, 0)),
                   small_spec, small_spec, small_spec],
        out_shape=[jax.ShapeDtypeStruct((s * ROW_TILE, LANES), F32), small, small, small],
        compiler_params=_params(("parallel",)),
    )(fox_o, y_gla, moba_o, p, p, p, x, wb, wo, lg, lb, wrh, wrl, br)


def _rank_kernel(sel_ref, idx_ref, dest_ref, tab_ref, carry_ref, start_ref):
    phase = pl.program_id(0)
    i = pl.program_id(1)
    t = sel_ref.shape[0]
    sel = sel_ref[...]
    chosen = jnp.sum(sel, axis=0, keepdims=True)

    @pl.when(jnp.logical_and(phase == 0, i == 0))
    def _():
        carry_ref[...] = jnp.zeros_like(carry_ref)

    @pl.when(jnp.logical_and(phase == 1, i == 0))
    def _():
        counts = carry_ref[...]
        blocks = jnp.floor((counts + (EXPERT_BLOCK - 1)) * (1.0 / EXPERT_BLOCK))
        before = (_iota((LANES, LANES), 0) < _iota((LANES, LANES), 1)).astype(BF16)
        first = _dot(jnp.broadcast_to(blocks, (ROW_TILE, LANES)).astype(BF16), before)[0:1, :]
        start_ref[...] = first * float(EXPERT_BLOCK)
        carry_ref[...] = jnp.zeros_like(carry_ref)

        nb = tab_ref.shape[0]
        b = _iota((nb, LANES), 0).astype(F32)
        lane = _iota((nb, LANES), 1)
        is_expert = lane < N_EXPERTS
        end = first + blocks
        owner = jnp.sum(jnp.where(jnp.logical_and(is_expert, b >= end), 1.0, 0.0),
                        axis=1, keepdims=True)
        total = jnp.max(jnp.where(lane[0:1, :] < N_EXPERTS, end, 0.0), axis=1, keepdims=True)
        last = jnp.max(jnp.where(counts > 0.0, lane[0:1, :].astype(F32), 0.0), axis=1, keepdims=True)
        active = b[:, 0:1] < total
        owner = jnp.where(active, owner, last)
        mine = lane.astype(F32) == owner
        rows_left = jnp.sum(jnp.where(mine, counts + (first - b) * float(EXPERT_BLOCK), 0.0),
                            axis=1, keepdims=True)
        valid = jnp.where(active, jnp.clip(rows_left, 0.0, float(EXPERT_BLOCK)), 0.0)
        tab_ref[...] = jnp.where(lane == 0, owner, jnp.where(lane == 1, valid, 0.0)).astype(jnp.int32)

    @pl.when(phase == 1)
    def _():
        stri = (_iota((t, t), 0) > _iota((t, t), 1)).astype(BF16)
        pos = _dot(stri, sel.astype(BF16)) + carry_ref[...] + start_ref[...]
        lane = _iota((t, LANES), 1).astype(F32)
        idxf = idx_ref[...]
        dest = jnp.zeros((t, LANES), F32)
        for r in range(TOP_K):
            mine = jnp.sum(jnp.where(lane == idxf[:, r:r + 1], pos, 0.0), axis=1, keepdims=True)
            dest = jnp.where(lane == float(r), mine, dest)
        dest_ref[...] = dest.T[:ROW_TILE, :].astype(jnp.int32)

    carry_ref[...] += chosen


def _n_expert_blocks(s):
    n = (s * TOP_K + N_EXPERTS * (EXPERT_BLOCK - 1)) // EXPERT_BLOCK + 1
    return -(-n // ROW_TILE) * ROW_TILE


def _rank(sel, idxf, t=512):
    s = sel.shape[0]
    nb = _n_expert_blocks(s)
    spec = pl.BlockSpec((t, LANES), lambda ph, i: (i, 0))
    return pl.pallas_call(
        _rank_kernel,
        grid=(2, s // t),
        in_specs=[spec, spec],
        out_specs=[pl.BlockSpec((ROW_TILE, t), lambda ph, i: (0, i * ph)),
                   pl.BlockSpec((nb, LANES), lambda ph, i: (0, 0))],
        out_shape=[jax.ShapeDtypeStruct((ROW_TILE, s), jnp.int32),
                   jax.ShapeDtypeStruct((nb, LANES), jnp.int32)],
        scratch_shapes=[pltpu.VMEM((1, LANES), F32), pltpu.VMEM((1, LANES), F32)],
        compiler_params=_params(("arbitrary", "arbitrary")),
    )(sel, idxf)


DMA_UNROLL = 4


def _token_rows(ref, token):
    return ref.at[pl.ds(pl.multiple_of(token * ROW_TILE, ROW_TILE), ROW_TILE), :]


def _dispatch_kernel(dest_ref, x_ref, xin_ref, sem):
    n = dest_ref.shape[1]

    def issue(g, _):
        for u in range(DMA_UNROLL):
            t = g * DMA_UNROLL + u
            for k in range(TOP_K):
                pltpu.make_async_copy(_token_rows(x_ref, t), _token_rows(xin_ref, dest_ref[k, t]),
                                      sem).start(priority=k % 2)
        return 0

    lax.fori_loop(0, n // DMA_UNROLL, issue, 0)
    for k in range(TOP_K):
        pltpu.make_async_copy(x_ref, xin_ref.at[pl.ds(0, n * ROW_TILE), :], sem).wait()


def _dispatch(dest, x1_2d, n_rows, td=512):
    s = x1_2d.shape[0] // ROW_TILE
    return pl.pallas_call(
        _dispatch_kernel,
        grid=(s // td,),
        in_specs=[pl.BlockSpec((ROW_TILE, td), lambda i: (0, i), memory_space=pltpu.SMEM),
                  pl.BlockSpec((td * ROW_TILE, LANES), lambda i: (i, 0))],
        out_specs=pl.BlockSpec(memory_space=pl.ANY),
        out_shape=jax.ShapeDtypeStruct((n_rows * ROW_TILE, LANES), F32),
        scratch_shapes=[pltpu.SemaphoreType.DMA(())],
        compiler_params=pltpu.CompilerParams(dimension_semantics=("arbitrary",),
                                             has_side_effects=True),
    )(dest, x1_2d)


HALF = LANES // 2
W_CAST_ROWS = 64


def _expert_kernel(be_ref, nv_ref, xin_ref, wgu_ref, bgu_ref, wdn_ref, bdn_ref, y_ref,
                   wgu_s, wdn_s, perm_s):
    b = pl.program_id(0)
    blk = y_ref.shape[0] // ROW_TILE
    nvalid = nv_ref[b]
    new_expert = jnp.logical_or(b == 0, be_ref[b] != be_ref[jnp.maximum(b - 1, 0)])

    @pl.when(jnp.logical_and(new_expert, nvalid > 0))
    def _():
        def cast(r, _):
            r0 = pl.multiple_of(r * W_CAST_ROWS, W_CAST_ROWS)
            wgu_s[pl.ds(r0, W_CAST_ROWS), :] = wgu_ref[0, pl.ds(r0, W_CAST_ROWS), :].astype(BF16)
            return 0

        lax.fori_loop(0, D_MODEL // W_CAST_ROWS, cast, 0)
        for c in range(D_MODEL // LANES):
            cols = slice(c * LANES, (c + 1) * LANES)
            for g in range(D_MODEL // LANES):
                lo = g * LANES
                perm_s[c, pl.ds(lo, HALF, stride=2), :] = wdn_ref[0, lo:lo + HALF, cols]
                perm_s[c, pl.ds(lo + 1, HALF, stride=2), :] = wdn_ref[0, lo + HALF:lo + LANES, cols]
            wdn_s[:, cols] = perm_s[c].astype(BF16)

    @pl.when(nvalid > 0)
    def _():
        x = _load_row_tiles(xin_ref, blk)
        x = jnp.where(_iota((blk, D_MODEL), 0) < nvalid, x, 0.0).astype(BF16)
        even = _iota((blk, LANES), 1) % 2 == 0
        acts = []
        for g in range(D_MODEL // LANES):
            lo = g * 2 * LANES
            h = _dot(x, wgu_s[:, lo:lo + 2 * LANES]) + bgu_ref[0, :, lo:lo + 2 * LANES]
            h_a = h[:, :LANES]
            h_b = h[:, LANES:]
            gate = jnp.where(even, h_a, pltpu.roll(h_b, 1, 1))
            up = jnp.where(even, pltpu.roll(h_a, LANES - 1, 1), h_b)
            gate = jnp.minimum(gate, SWIGLU_LIMIT)
            up = jnp.clip(up, -SWIGLU_LIMIT, SWIGLU_LIMIT)
            acts.append(((up + 1.0) * (gate * _sigmoid(SWIGLU_ALPHA * gate))).astype(BF16))
        act = jnp.concatenate(acts, axis=1)
        _store_row_tiles(y_ref, _dot(act, wdn_s[...]) + bdn_ref[0])

    @pl.when(nvalid <= 0)
    def _():
        y_ref[...] = jnp.zeros_like(y_ref)


def _experts(block_expert, block_valid, xin2d, w_gu, b_gu, w_dn, b_dn, layer):
    blk = EXPERT_BLOCK
    nb = block_expert.shape[0]
    x_spec = pl.BlockSpec((blk * ROW_TILE, LANES), lambda b, be, nv: (b, 0))
    which = lambda b, be, nv: (layer * N_EXPERTS + be[b], 0, 0)
    return pl.pallas_call(
        _expert_kernel,
        grid_spec=pltpu.PrefetchScalarGridSpec(
            num_scalar_prefetch=2,
            grid=(nb,),
            in_specs=[x_spec,
                      pl.BlockSpec((1, D_MODEL, 2 * D_MODEL), which),
                      pl.BlockSpec((1, 1, 2 * D_MODEL), which),
                      pl.BlockSpec((1, D_MODEL, D_MODEL), which),
                      pl.BlockSpec((1, 1, D_MODEL), which)],
            out_specs=x_spec,
            scratch_shapes=[pltpu.VMEM((D_MODEL, 2 * D_MODEL), BF16),
                            pltpu.VMEM((D_MODEL, D_MODEL), BF16),
                            pltpu.VMEM((D_MODEL // LANES, D_MODEL, LANES), F32)]),
        out_shape=jax.ShapeDtypeStruct((nb * blk * ROW_TILE, LANES), F32),
        compiler_params=pltpu.CompilerParams(dimension_semantics=("arbitrary",),
                                             vmem_limit_bytes=56 * 1024 * 1024),
    )(block_expert, block_valid, xin2d, w_gu, b_gu, w_dn, b_dn)


def _combine_kernel(dest_ref, next_ref, yb_ref, gw_ref, x1_ref, lg_ref, lb_ref, out_ref,
                    buf_ref, sem):
    i = pl.program_id(0)
    tc = out_ref.shape[0]
    slot = i % 2

    def gather(rows_ref, into):
        def body(g, _):
            for u in range(DMA_UNROLL):
                t = g * DMA_UNROLL + u
                for k in range(TOP_K):
                    pltpu.make_async_copy(_token_rows(yb_ref, rows_ref[k, t]),
                                          _token_rows(buf_ref.at[into, k], t),
                                          sem.at[into]).start(priority=k % 2)
            return 0

        lax.fori_loop(0, tc // DMA_UNROLL, body, 0)

    @pl.when(i == 0)
    def _():
        gather(dest_ref, 0)

    @pl.when(i + 1 < pl.num_programs(0))
    def _():
        gather(next_ref, 1 - slot)

    for k in range(TOP_K):
        pltpu.make_async_copy(yb_ref.at[pl.ds(0, tc * ROW_TILE), :], buf_ref.at[slot, k],
                              sem.at[slot]).wait()

    gw = gw_ref[...]
    f = jnp.zeros((tc, D_MODEL), F32)
    for k in range(TOP_K):
        f += gw[:, k:k + 1] * _load_row_tiles(buf_ref.at[slot, k], tc)
    z = ALPHA * _load_row_tiles(x1_ref, tc) + f
    out_ref[...] = _layer_norm(z, lg_ref[...], lb_ref[...])


def _combine(dest, yb_rows, gw, x1_2d, lg, lb, tc=256):
    s = gw.shape[0]
    n = s // tc
    row = pl.BlockSpec((1, D_MODEL), lambda i: (0, 0))
    return pl.pallas_call(
        _combine_kernel,
        grid=(n,),
        in_specs=[pl.BlockSpec((ROW_TILE, tc), lambda i: (0, i), memory_space=pltpu.SMEM),
                  pl.BlockSpec((ROW_TILE, tc), lambda i: (0, jnp.minimum(i + 1, n - 1)),
                               memory_space=pltpu.SMEM),
                  pl.BlockSpec(memory_space=pl.ANY),
                  pl.BlockSpec((tc, LANES), lambda i: (i, 0)),
                  pl.BlockSpec((tc * ROW_TILE, LANES), lambda i: (i, 0)),
                  row, row],
        out_specs=pl.BlockSpec((tc, D_MODEL), lambda i: (i, 0)),
        out_shape=jax.ShapeDtypeStruct((s, D_MODEL), F32),
        scratch_shapes=[pltpu.VMEM((2, TOP_K, tc * ROW_TILE, LANES), F32),
                        pltpu.SemaphoreType.DMA((2,))],
        compiler_params=_params(("arbitrary",)),
    )(dest, dest, yb_rows, gw, x1_2d, lg, lb)


W_IN_SIZES = (256, 256, 256, 4, 128, 128, 256, 16, 256, 256, 256, 256, 3 * D_MODEL)
W_IN_WIDTH = sum(W_IN_SIZES)


def _w_in_kernel(w_ref, main_ref, tail_ref):
    w = w_ref[0]
    offs = [0]
    for n in W_IN_SIZES:
        offs.append(offs[-1] + n)
    fq, fk, fv, ff, gq, gk, gv, ga, gr, mq, mk, mv, gates = (
        w[:, offs[i]:offs[i + 1]] for i in range(len(W_IN_SIZES)))
    main_ref[...] = jnp.concatenate(
        [gates, fq, fk, fv, mq, mk, mv, gq, gk, gv, gr], axis=1).astype(BF16)
    assert TAIL_FF == 0 and TAIL_GA == N_HEADS, "tail lanes: forget logits, then gate features"
    pad = jnp.zeros((w.shape[0], LANES - TAIL_GA - GLA_RANK), F32)
    tail_ref[...] = jnp.concatenate([ff, ga, pad], axis=1).astype(BF16)


def _permute_w_in(w_in, layer, tr=128):
    return pl.pallas_call(
        _w_in_kernel,
        grid=(D_MODEL // tr,),
        in_specs=[pl.BlockSpec((1, tr, W_IN_WIDTH), lambda i: (layer, i, 0))],
        out_specs=[pl.BlockSpec((tr, P_WIDTH), lambda i: (i, 0)),
                   pl.BlockSpec((tr, LANES), lambda i: (i, 0))],
        out_shape=[jax.ShapeDtypeStruct((D_MODEL, P_WIDTH), BF16),
                   jax.ShapeDtypeStruct((D_MODEL, LANES), BF16)],
        compiler_params=_params(("parallel",)),
    )(w_in)


def _pad_row(v, n=LANES, fill=0.0):
    return jnp.pad(v.astype(F32), (0, n - v.shape[0]), constant_values=fill)[None, :]


def _mixer_layer(x, w_in, layer, fox_fb, gla_a_up, gla_a_b, gla_norm_g, w_branch, w_out, ln_g, ln_b,
                 w_router, b_router):
    w_main, w_tail = _permute_w_in(w_in, layer)
    p, tail = _inproj(x, w_main, w_tail)

    s = x.shape[0]
    tq, tk = _flash_tiles(s)
    fox_qt, fox_k, fox_vt, fox_stats = _fox_prep(p, tail, _pad_row(fox_fb))
    fox_o = _flash(_fox_first_tile(fox_stats, s, tq, tk), fox_qt, fox_k, fox_vt, tq, tk)
    moba_o = _flash(jnp.zeros((N_HEADS * (s // tq),), jnp.int32), *_moba_prep(p), tq, tk)
    aup_pad = jnp.zeros((LANES, GLA_KW), F32).at[TAIL_GA:TAIL_GA + GLA_RANK].set(gla_a_up)
    y_gla = _gla(p, tail, aup_pad, gla_a_b[None, :], gla_norm_g[None, :])

    wr = jnp.pad(w_router, ((0, 0), (0, LANES - N_EXPERTS)))
    wr_hi = wr.astype(BF16)
    wr_lo = jnp.concatenate([wr_hi, (wr - wr_hi.astype(F32)).astype(BF16)], axis=1)
    return _merge(fox_o, y_gla, moba_o, p, x, w_branch.astype(BF16), w_out.astype(BF16),
                  ln_g[None, :], ln_b[None, :], wr_hi, wr_lo, _pad_row(b_router))


def _moe_layer(x1_2d, sel, idxf, gw, w_gu, b_gu, w_dn, b_dn, ln_g, ln_b, layer):
    s = sel.shape[0]
    blk = EXPERT_BLOCK
    dest, tables = _rank(sel, idxf)
    n_blocks = tables.shape[0]
    block_expert = tables[:, 0]
    block_valid = tables[:, 1]

    xin = _dispatch(dest, x1_2d, n_blocks * blk)
    yb = _experts(block_expert, block_valid, xin, w_gu, b_gu, w_dn, b_dn, layer)
    return _combine(dest, yb, gw, x1_2d, ln_g[None, :], ln_b[None, :])


def kernel(x, w_in, fox_fb, gla_a_up, gla_a_b, gla_norm_g, w_branch, w_out, ln1_g, ln1_b,
           w_router, b_router, w_gu, b_gu, w_dn, b_dn, ln2_g, ln2_b):
    b, s, d = x.shape
    assert b == 1 and d == D_MODEL
    h = x.reshape(s, d)
    n_le = DEPTH * N_EXPERTS
    w_gu = w_gu.reshape(n_le, D_MODEL, 2 * D_MODEL)
    b_gu = b_gu.reshape(n_le, 1, 2 * D_MODEL)
    w_dn = w_dn.reshape(n_le, D_MODEL, D_MODEL)
    b_dn = b_dn.reshape(n_le, 1, D_MODEL)
    for l in range(DEPTH):
        x1_2d, sel, idxf, gw = _mixer_layer(
            h, w_in, l, fox_fb[l], gla_a_up[l], gla_a_b[l], gla_norm_g[l], w_branch[l], w_out[l],
            ln1_g[l], ln1_b[l], w_router[l], b_router[l])
        h = _moe_layer(x1_2d, sel, idxf, gw, w_gu, b_gu, w_dn, b_dn, ln2_g[l], ln2_b[l], l)
    return h.reshape(b, s, d)
```
